```python
import jax, jax.numpy as jnp
from jax import lax
import numpy as np

D_MODEL = 2048
BATCH = 32
SEQ = 256
DEPTH = 4
DEC_BATCH = 2
DEC_SEQ = 1024
PAST_LEN = 512

GRID_W = 64
ADA_CHUNKS = 6
NORM_EPS = 1e-6
Q_BLOCK = 128
ROPE_THETA = 10000.0

MLA_HEADS = 8
MLA_NOPE = 128
MLA_ROPE = 64
MLA_QK_DIM = MLA_NOPE + MLA_ROPE
MLA_V_DIM = 128
MLA_Q_LORA = 512
MLA_KV_LORA = 256
MLA_WIDTH = MLA_HEADS * MLA_V_DIM

NA_HEADS = 4
NA_HEAD_DIM = 128
NA_WIDTH = NA_HEADS * NA_HEAD_DIM
NA_KR = 8
NA_KC = 16
NA_BAND = 2 * NA_KC
NA_NCB = GRID_W // NA_KC

CONV_CH = 512
CONV_K = 3

D_MIX = MLA_WIDTH + NA_WIDTH + CONV_CH
IN_SIZES = (MLA_Q_LORA, MLA_KV_LORA, MLA_ROPE, NA_WIDTH, NA_WIDTH, NA_WIDTH, CONV_CH, CONV_CH, CONV_CH)
D_IN = MLA_Q_LORA + MLA_KV_LORA + MLA_ROPE + 3 * NA_WIDTH + 3 * CONV_CH

PEER_HEADS = 8
PEER_N_KEYS = 128
PEER_N_EXPERTS = PEER_N_KEYS * PEER_N_KEYS
PEER_KEY_DIM = 256
PEER_KEY_HALF = PEER_KEY_DIM // 2
PEER_TOPK = 16
PEER_TOKEN_BLOCK = 128

kernel_name = "hybrid_mla_natten_shortconv_peer_dit_step"


def rms_norm(x, g):
    xf = x.astype(jnp.float32)
    y = xf * lax.rsqrt(jnp.mean(xf * xf, axis=-1, keepdims=True) + NORM_EPS)
    return (y * g.astype(jnp.float32)).astype(x.dtype)


def split_heads(x, n_heads):
    b, s, _ = x.shape
    return x.reshape(b, s, n_heads, -1).transpose(0, 2, 1, 3)


def merge_heads(x):
    b, n, s, d = x.shape
    return x.transpose(0, 2, 1, 3).reshape(b, s, n * d)


def ada_modulation(cvec, w, b):
    m = jnp.dot(jax.nn.silu(cvec), w) + b
    return jnp.split(m[:, None, :], ADA_CHUNKS, axis=-1)


def modulate(x, g, shift, scale):
    return rms_norm(x, g) * (1 + scale) + shift


def axial_rope_tables(n_tokens):
    t = jnp.arange(n_tokens)
    row = (t // GRID_W).astype(jnp.float32)
    col = (t % GRID_W).astype(jnp.float32)
    n_freq = MLA_ROPE // 4
    inv = ROPE_THETA ** (-jnp.arange(n_freq, dtype=jnp.float32) / n_freq)
    ang = jnp.concatenate([row[:, None] * inv, col[:, None] * inv], axis=-1)
    return jnp.cos(ang), jnp.sin(ang)


def rope_tail(x, cos, sin):
    x_pass, x_rot = x[..., :-MLA_ROPE], x[..., -MLA_ROPE:]
    xf = x_rot.astype(jnp.float32)
    half = MLA_ROPE // 2
    x1, x2 = xf[..., :half], xf[..., half:]
    rot = jnp.concatenate([x1 * cos - x2 * sin, x1 * sin + x2 * cos], axis=-1)
    return jnp.concatenate([x_pass, rot.astype(x.dtype)], axis=-1)


def blocked_attention(q, k, v, scale):
    b, h, sq, dq = q.shape
    nb = sq // Q_BLOCK
    qb = q.reshape(b, h, nb, Q_BLOCK, dq).transpose(2, 0, 1, 3, 4)

    def one_block(q_blk):
        s = jnp.einsum('bhqd,bhkd->bhqk', q_blk, k).astype(jnp.float32) * scale
        p = jax.nn.softmax(s, axis=-1).astype(v.dtype)
        return jnp.einsum('bhqk,bhkd->bhqd', p, v)

    o = lax.map(one_block, qb)
    return o.transpose(1, 2, 0, 3, 4).reshape(b, h, sq, v.shape[-1])


def neighbourhood_attention(q, k, v, k_ctx, v_ctx, rel_bias):
    b, h, t, d = q.shape
    rows = t // GRID_W
    kr = min(NA_KR, rows)
    r = jnp.arange(rows)
    row_start = jnp.clip(r - kr // 2, 0, rows - kr)
    row_idx = row_start[:, None] + jnp.arange(kr)[None, :]
    row_off = row_idx - r[:, None] + (NA_KR - 1)
    j = jnp.arange(NA_NCB)
    band_start = jnp.clip(j * NA_KC - NA_KC // 2, 0, GRID_W - NA_BAND)
    key_col = band_start[:, None] + jnp.arange(NA_BAND)[None, :]
    q_col = j[:, None] * NA_KC + jnp.arange(NA_KC)[None, :]
    col_start = jnp.clip(q_col - NA_KC // 2, 0, GRID_W - NA_KC)
    kc3 = key_col[:, None, :]
    valid = (kc3 >= col_start[..., None]) & (kc3 < col_start[..., None] + NA_KC)
    col_off = jnp.clip(kc3 - q_col[..., None], -(NA_KC - 1), NA_KC - 1) + (NA_KC - 1)
    k_grid = k.reshape(b, h, rows, GRID_W, d)
    v_grid = v.reshape(b, h, rows, GRID_W, d)
    q_rows = q.reshape(b, h, rows, NA_NCB, NA_KC, d).transpose(2, 0, 1, 3, 4, 5)
    scale = NA_HEAD_DIM ** -0.5
    n_loc = kr * NA_BAND

    def one_row(args):
        q_r, ridx, roff = args
        kb = jnp.take(k_grid, ridx, axis=2)[:, :, :, key_col]
        vb = jnp.take(v_grid, ridx, axis=2)[:, :, :, key_col]
        bias = rel_bias[:, roff[None, None, :, None], col_off[:, :, None, :]]
        s_loc = jnp.einsum('bhjqd,bhajcd->bhjqac', q_r, kb).astype(jnp.float32) * scale
        s_loc = jnp.where(valid[:, :, None, :], s_loc + bias.astype(jnp.float32), -jnp.inf)
        s_ctx = jnp.einsum('bhjqd,bhkd->bhjqk', q_r, k_ctx).astype(jnp.float32) * scale
        s = jnp.concatenate([s_loc.reshape(b, h, NA_NCB, NA_KC, n_loc), s_ctx], axis=-1)
        p = jax.nn.softmax(s, axis=-1).astype(v.dtype)
        p_loc = p[..., :n_loc].reshape(b, h, NA_NCB, NA_KC, kr, NA_BAND)
        return (jnp.einsum('bhjqac,bhajcd->bhjqd', p_loc, vb)
                + jnp.einsum('bhjqk,bhkd->bhjqd', p[..., n_loc:], v_ctx))

    o = lax.map(one_row, (q_rows, row_idx, row_off))
    return o.transpose(1, 2, 0, 3, 4, 5).reshape(b, h, t, d)


def short_conv(u, w):
    s = u.shape[1]
    up = jnp.pad(u, ((0, 0), (1, 1), (0, 0)))
    return up[:, :s] * w[:, 0] + up[:, 1:s + 1] * w[:, 1] + up[:, 2:] * w[:, 2]


def mla_queries(cq, lp):
    q = jnp.dot(rms_norm(cq, lp['mla_q_norm_g']), lp['mla_w_uq'])
    return rms_norm(split_heads(q, MLA_HEADS), lp['mla_q_head_g'])


def mla_keys_values(ckv_n, kpe, lp):
    b, l, _ = ckv_n.shape
    kv = jnp.dot(ckv_n, lp['mla_w_ukv']).reshape(b, l, MLA_HEADS, MLA_NOPE + MLA_V_DIM)
    k_nope, v = kv[..., :MLA_NOPE], kv[..., MLA_NOPE:]
    k_pe = jnp.broadcast_to(kpe[:, :, None, :], (b, l, MLA_HEADS, MLA_ROPE))
    k = jnp.concatenate([k_nope, k_pe], axis=-1).transpose(0, 2, 1, 3)
    return rms_norm(k, lp['mla_k_head_g']), v.transpose(0, 2, 1, 3)


def project_mixers(h, lp):
    z = jnp.dot(h, lp['w_in'])
    offs, acc = [], 0
    for sz in IN_SIZES[:-1]:
        acc += sz
        offs.append(acc)
    cq, ckv, kpe, na_q, na_k, na_v, g_b, g_c, u = jnp.split(z, offs, axis=-1)
    q_mla = mla_queries(cq, lp)
    ckv_n = rms_norm(ckv, lp['mla_kv_norm_g'])
    q_na = rms_norm(split_heads(na_q, NA_HEADS), lp['na_q_head_g'])
    k_na = rms_norm(split_heads(na_k, NA_HEADS), lp['na_k_head_g'])
    v_na = split_heads(na_v, NA_HEADS)
    conv_out = g_b * short_conv(g_c * u, lp['conv_w'])
    return q_mla, ckv_n, kpe, q_na, k_na, v_na, conv_out


def peer(h, lp):
    b, s, dm = h.shape
    x = h.reshape(b * s, dm)
    t = x.shape[0]
    q = jnp.dot(x, lp['peer_w_q']).reshape(t, PEER_HEADS, 2, PEER_KEY_HALF)
    sc = jnp.einsum('thpd,hpnd->thpn', q, lp['peer_sub_keys']).astype(jnp.float32)
    s1, i1 = lax.top_k(sc[:, :, 0], PEER_TOPK)
    s2, i2 = lax.top_k(sc[:, :, 1], PEER_TOPK)
    cand = (s1[..., :, None] + s2[..., None, :]).reshape(t, PEER_HEADS, PEER_TOPK * PEER_TOPK)
    cidx = (i1[..., :, None] * PEER_N_KEYS + i2[..., None, :]).reshape(t, PEER_HEADS, PEER_TOPK * PEER_TOPK)
    top_s, pos = lax.top_k(cand, PEER_TOPK)
    eidx = jnp.take_along_axis(cidx, pos, axis=-1)
    gate = jax.nn.softmax(top_s, axis=-1)
    nb = t // PEER_TOKEN_BLOCK
    ne = PEER_HEADS * PEER_TOPK

    def one_block(args):
        xb, eb, gb = args
        e = eb.reshape(PEER_TOKEN_BLOCK, ne)
        u = jnp.take(lp['peer_u'], e, axis=0)
        a = jax.nn.gelu(jnp.einsum('td,ted->te', xb, u), approximate=False)
        wgt = gb.reshape(PEER_TOKEN_BLOCK, ne).astype(a.dtype) * a
        v = jnp.take(lp['peer_v'], e, axis=0)
        return jnp.einsum('te,ted->td', wgt, v)

    out = lax.map(one_block, (x.reshape(nb, PEER_TOKEN_BLOCK, dm),
                              eidx.reshape(nb, PEER_TOKEN_BLOCK, PEER_HEADS, PEER_TOPK),
                              gate.reshape(nb, PEER_TOKEN_BLOCK, PEER_HEADS, PEER_TOPK)))
    return out.reshape(b, s, dm)


def context_layer(x, c_ctx, lp):
    sh1, sc1, g1, sh2, sc2, g2 = ada_modulation(c_ctx[None, :], lp['ada_w'], lp['ada_b'])
    h = modulate(x, lp['norm_mix_g'], sh1, sc1)
    q_mla, ckv_n, kpe, q_na, k_na, v_na, conv_out = project_mixers(h, lp)
    k_mla, v_mla = mla_keys_values(ckv_n, kpe, lp)
    o_mla = blocked_attention(q_mla, k_mla, v_mla, MLA_QK_DIM ** -0.5)
    o_na = blocked_attention(q_na, k_na, v_na, NA_HEAD_DIM ** -0.5)
    mix = jnp.concatenate([merge_heads(o_mla), merge_heads(o_na), conv_out], axis=-1)
    x = x + g1 * jnp.dot(mix, lp['w_out'])
    x = x + g2 * peer(modulate(x, lp['norm_ffn_g'], sh2, sc2), lp)
    return x, ckv_n, kpe, k_na, v_na


def latent_layer(x, c, lp, ckv_ctx, kpe_ctx, k_na_ctx, v_na_ctx, cos, sin):
    sh1, sc1, g1, sh2, sc2, g2 = ada_modulation(c, lp['ada_w'], lp['ada_b'])
    h = modulate(x, lp['norm_mix_g'], sh1, sc1)
    q_mla, ckv_n, kpe, q_na, k_na, v_na, conv_out = project_mixers(h, lp)
    q_mla = rope_tail(q_mla, cos, sin)
    k_lat, v_lat = mla_keys_values(ckv_n, kpe, lp)
    k_lat = rope_tail(k_lat, cos, sin)
    k_ctx, v_ctx = mla_keys_values(ckv_ctx, kpe_ctx, lp)
    o_mla = blocked_attention(q_mla, jnp.concatenate([k_ctx, k_lat], axis=2),
                              jnp.concatenate([v_ctx, v_lat], axis=2), MLA_QK_DIM ** -0.5)
    o_na = neighbourhood_attention(q_na, k_na, v_na, k_na_ctx, v_na_ctx, lp['na_rel_bias'])
    mix = jnp.concatenate([merge_heads(o_mla), merge_heads(o_na), conv_out], axis=-1)
    x = x + g1 * jnp.dot(mix, lp['w_out'])
    x = x + g2 * peer(modulate(x, lp['norm_ffn_g'], sh2, sc2), lp)
    return x


def setup_inputs(seed: int = 0) -> dict:
    key = jax.random.key(seed)
    ks = jax.random.split(key, 32)
    f32 = jnp.float32

    def nrm(k, shape, scale):
        return jax.random.normal(k, shape, f32) * scale

    def gain(k, shape):
        return 1.0 + 0.02 * jax.random.normal(k, shape, f32)

    return {
        "x_prompt": nrm(ks[0], (BATCH, SEQ, D_MODEL), 1.0),
        "x_sample": nrm(ks[1], (DEC_BATCH, DEC_SEQ, D_MODEL), 1.0),
        "cache_mla_ckv": nrm(ks[2], (DEC_BATCH, DEPTH, PAST_LEN, MLA_KV_LORA), 1.0),
        "cache_mla_kpe": nrm(ks[3], (DEC_BATCH, DEPTH, PAST_LEN, MLA_ROPE), 1.0),
        "cache_na_k": nrm(ks[4], (DEC_BATCH, DEPTH, NA_HEADS, PAST_LEN, NA_HEAD_DIM), 1.0),
        "cache_na_v": nrm(ks[5], (DEC_BATCH, DEPTH, NA_HEADS, PAST_LEN, NA_HEAD_DIM), 1.0),
        "c": nrm(ks[6], (DEC_BATCH, D_MODEL), 1.0),
        "c_ctx": nrm(ks[7], (D_MODEL,), 1.0),
        "ada_w": nrm(ks[8], (DEPTH, D_MODEL, ADA_CHUNKS * D_MODEL), D_MODEL ** -0.5),
        "ada_b": nrm(ks[9], (DEPTH, ADA_CHUNKS * D_MODEL), 0.02),
        "norm_mix_g": gain(ks[10], (DEPTH, D_MODEL)),
        "norm_ffn_g": gain(ks[11], (DEPTH, D_MODEL)),
        "w_in": nrm(ks[12], (DEPTH, D_MODEL, D_IN), D_MODEL ** -0.5),
        "mla_q_norm_g": gain(ks[13], (DEPTH, MLA_Q_LORA)),
        "mla_w_uq": nrm(ks[14], (DEPTH, MLA_Q_LORA, MLA_HEADS * MLA_QK_DIM), MLA_Q_LORA ** -0.5),
        "mla_kv_norm_g": gain(ks[15], (DEPTH, MLA_KV_LORA)),
        "mla_w_ukv": nrm(ks[16], (DEPTH, MLA_KV_LORA, MLA_HEADS * (MLA_NOPE + MLA_V_DIM)), MLA_KV_LORA ** -0.5),
        "mla_q_head_g": gain(ks[17], (DEPTH, MLA_QK_DIM)),
        "mla_k_head_g": gain(ks[18], (DEPTH, MLA_QK_DIM)),
        "na_q_head_g": gain(ks[19], (DEPTH, NA_HEAD_DIM)),
        "na_k_head_g": gain(ks[20], (DEPTH, NA_HEAD_DIM)),
        "na_rel_bias": nrm(ks[21], (DEPTH, NA_HEADS, 2 * NA_KR - 1, 2 * NA_KC - 1), 0.02),
        "conv_w": nrm(ks[22], (DEPTH, CONV_CH, CONV_K), CONV_K ** -0.5),
        "w_out": nrm(ks[23], (DEPTH, D_MIX, D_MODEL), D_MIX ** -0.5),
        "peer_w_q": nrm(ks[24], (DEPTH, D_MODEL, PEER_HEADS * PEER_KEY_DIM), D_MODEL ** -0.5),
        "peer_sub_keys": nrm(ks[25], (DEPTH, PEER_HEADS, 2, PEER_N_KEYS, PEER_KEY_HALF), PEER_KEY_HALF ** -0.5),
        "peer_u": nrm(ks[26], (DEPTH, PEER_N_EXPERTS, D_MODEL), D_MODEL ** -0.5),
        "peer_v": nrm(ks[27], (DEPTH, PEER_N_EXPERTS, D_MODEL), 0.05),
    }


def reference(x_prompt, x_sample, cache_mla_ckv, cache_mla_kpe, cache_na_k, cache_na_v, c, c_ctx,
              ada_w, ada_b, norm_mix_g, norm_ffn_g, w_in, mla_q_norm_g, mla_w_uq, mla_kv_norm_g,
              mla_w_ukv, mla_q_head_g, mla_k_head_g, na_q_head_g, na_k_head_g, na_rel_bias, conv_w,
              w_out, peer_w_q, peer_sub_keys, peer_u, peer_v):
    y_prompt = x_prompt
    y_sample = x_sample
    cos, sin = axial_rope_tables(x_sample.shape[1])
    ckv_list, kpe_list, nak_list, nav_list = [], [], [], []
    for i in range(DEPTH):
        lp = {
            'ada_w': ada_w[i], 'ada_b': ada_b[i], 'norm_mix_g': norm_mix_g[i], 'norm_ffn_g': norm_ffn_g[i],
            'w_in': w_in[i], 'mla_q_norm_g': mla_q_norm_g[i], 'mla_w_uq': mla_w_uq[i],
            'mla_kv_norm_g': mla_kv_norm_g[i], 'mla_w_ukv': mla_w_ukv[i],
            'mla_q_head_g': mla_q_head_g[i], 'mla_k_head_g': mla_k_head_g[i],
            'na_q_head_g': na_q_head_g[i], 'na_k_head_g': na_k_head_g[i], 'na_rel_bias': na_rel_bias[i],
            'conv_w': conv_w[i], 'w_out': w_out[i], 'peer_w_q': peer_w_q[i],
            'peer_sub_keys': peer_sub_keys[i], 'peer_u': peer_u[i], 'peer_v': peer_v[i],
        }
        y_prompt, ckv_n, kpe, k_na, v_na = context_layer(y_prompt, c_ctx, lp)
        ckv_list.append(ckv_n)
        kpe_list.append(kpe)
        nak_list.append(k_na)
        nav_list.append(v_na)
        y_sample = latent_layer(y_sample, c, lp, cache_mla_ckv[:, i], cache_mla_kpe[:, i],
                                cache_na_k[:, i], cache_na_v[:, i], cos, sin)
    new_mla_ckv = jnp.stack(ckv_list, axis=1)
    new_mla_kpe = jnp.stack(kpe_list, axis=1)
    new_na_k = jnp.stack(nak_list, axis=1)
    new_na_v = jnp.stack(nav_list, axis=1)
    return (y_prompt, y_sample, new_mla_ckv, new_mla_kpe, new_na_k, new_na_v)
```

```python
import functools

import numpy as np
import jax
import jax.numpy as jnp
from jax import lax
from jax.experimental import pallas as pl
from jax.experimental.pallas import tpu as pltpu

F32 = jnp.float32
BF16 = jnp.bfloat16

D_MODEL = 2048
DEPTH = 4
N_CTX_SEQ = 32
CTX_SEQ = 256
N_LAT_SEQ = 2
LAT_SEQ = 1024
PAST_LEN = 512
T_CTX = N_CTX_SEQ * CTX_SEQ
T_LAT = N_LAT_SEQ * LAT_SEQ
T_ALL = T_CTX + T_LAT
T_CACHE = N_LAT_SEQ * PAST_LEN
GRID_W = 64
NORM_EPS = 1e-6
ROPE_THETA = 10000.0
ADA_CHUNKS = 6

MLA_HEADS = 8
MLA_NOPE = 128
MLA_ROPE = 64
MLA_QK_DIM = MLA_NOPE + MLA_ROPE
MLA_Q_LORA = 512
MLA_KV_LORA = 256
MLA_HEAD_PAD = 256
NA_HEADS = 4
NA_HEAD_DIM = 128
NA_WIDTH = NA_HEADS * NA_HEAD_DIM
NA_KR = 8
NA_KC = 16
CONV_CH = 512
D_IN_PAD = 4096
Z_CKV_BLOCK = 7
Z_KPE_COL = 3840

PEER_HEADS = 8
PEER_N_KEYS = 128
PEER_N_EXPERTS = PEER_N_KEYS * PEER_N_KEYS
PEER_TOPK = 16

LANES = 128
VMEM_LIMIT = 56 * 1024 * 1024

ROW_TILE = 512
SEQ_TILE = 256
PEER_TOK_TILE = 512
PEER_EXP_TILE = 512


def _params(sem):
    return pltpu.CompilerParams(dimension_semantics=sem, vmem_limit_bytes=VMEM_LIMIT)


def _group_of_row(row):
    return jnp.where(row < T_CTX, 0, 1 + (row - T_CTX) // LAT_SEQ)


def _rms(x, g, n):
    ms = jnp.sum(x * x, axis=-1, keepdims=True) / n
    return x * lax.rsqrt(ms + NORM_EPS) * g


def _ada_kernel(c_ref, w_ref, b_ref, o_ref):
    c = c_ref[...]
    s = c * jax.nn.sigmoid(c)
    o_ref[...] = jnp.dot(s.astype(BF16), w_ref[...].astype(BF16), preferred_element_type=F32) + b_ref[...]


def ada_all(cpad, ada_w, ada_b):
    tn = 1536
    n = ADA_CHUNKS * D_MODEL
    return pl.pallas_call(
        _ada_kernel,
        grid=(DEPTH, n // tn),
        in_specs=[pl.BlockSpec((8, D_MODEL), lambda l, j: (0, 0)),
                  pl.BlockSpec((None, D_MODEL, tn), lambda l, j: (l, 0, j)),
                  pl.BlockSpec((None, 1, tn), lambda l, j: (l, 0, j))],
        out_specs=pl.BlockSpec((None, 8, tn), lambda l, j: (l, 0, j)),
        out_shape=jax.ShapeDtypeStruct((DEPTH, 8, n), F32),
        compiler_params=_params(("arbitrary", "arbitrary")),
        name="ada",
    )(cpad, ada_w, ada_b.reshape(DEPTH, 1, n))


def _mod_matmul_kernel(x_ref, g_ref, sh_ref, sc_ref, w_ref, z_ref, *rest, tm, emit_ht):
    if emit_ht:
        ht_ref, h_scr = rest
    else:
        (h_scr,) = rest
    i = pl.program_id(0)

    @pl.when(pl.program_id(1) == 0)
    def _():
        grp = _group_of_row(i * tm)
        y = _rms(x_ref[...], g_ref[...], D_MODEL)
        h = y * (1.0 + sc_ref[pl.ds(grp, 1), :]) + sh_ref[pl.ds(grp, 1), :]
        h_scr[...] = h.astype(BF16)
        if emit_ht:
            ht_ref[...] = h.T.astype(BF16)

    z_ref[...] = jnp.dot(h_scr[...], w_ref[...], preferred_element_type=F32)


def mod_matmul(x, gain, mods, w, layer, shift_chunk, scale_chunk, *, tn, emit_ht):
    t = x.shape[0]
    n = w.shape[-1]
    tm = ROW_TILE
    out_shape = [jax.ShapeDtypeStruct((t, n), F32)]
    out_specs = [pl.BlockSpec((tm, tn), lambda i, j: (i, j))]
    if emit_ht:
        out_shape.append(jax.ShapeDtypeStruct((D_MODEL, t), BF16))
        out_specs.append(pl.BlockSpec((D_MODEL, tm), lambda i, j: (0, i)))
    res = pl.pallas_call(
        functools.partial(_mod_matmul_kernel, tm=tm, emit_ht=emit_ht),
        grid=(t // tm, n // tn),
        in_specs=[pl.BlockSpec((tm, D_MODEL), lambda i, j: (i, 0)),
                  pl.BlockSpec((None, 1, D_MODEL), lambda i, j: (layer, 0, 0)),
                  pl.BlockSpec((None, 8, D_MODEL), lambda i, j: (layer, 0, shift_chunk)),
                  pl.BlockSpec((None, 8, D_MODEL), lambda i, j: (layer, 0, scale_chunk)),
                  pl.BlockSpec((None, D_MODEL, tn), lambda i, j: (layer, 0, j))],
        out_specs=out_specs,
        out_shape=out_shape,
        scratch_shapes=[pltpu.VMEM((tm, D_MODEL), BF16)],
        compiler_params=_params(("arbitrary", "arbitrary")),
        name="mod_matmul_ht" if emit_ht else "mod_matmul",
    )(x, gain, mods, mods, w)
    return res


def _rope128(x, c, s1, s2):
    return x * c + pltpu.roll(x, 96, 1) * s1 + pltpu.roll(x, 32, 1) * s2


def _proj_prep_kernel(cq_ref, naq_ref, nak_ref, nav_ref, ckv_ref, rc_ref, rs1_ref, rs2_ref,
                      qng_ref, wuq_ref, qhg_ref, kvng_ref, naqg_ref, nakg_ref,
                      qmla_ref, ckvn_ref, qna_ref, knaf_ref, knab_ref, vnab_ref):
    cqn = _rms(cq_ref[...], qng_ref[...], MLA_Q_LORA)
    q = jnp.dot(cqn.astype(BF16), wuq_ref[...], preferred_element_type=F32)
    rc, rs1, rs2 = rc_ref[...], rs1_ref[...], rs2_ref[...]
    for h in range(MLA_HEADS):
        qh = _rms(q[:, h * MLA_HEAD_PAD:(h + 1) * MLA_HEAD_PAD], qhg_ref[...], MLA_QK_DIM)
        qmla_ref[:, h * MLA_HEAD_PAD:h * MLA_HEAD_PAD + LANES] = qh[:, :LANES].astype(BF16)
        qmla_ref[:, h * MLA_HEAD_PAD + LANES:(h + 1) * MLA_HEAD_PAD] = (
            _rope128(qh[:, LANES:], rc, rs1, rs2).astype(BF16))
    ckvn_ref[...] = _rms(ckv_ref[:, :MLA_KV_LORA], kvng_ref[...], MLA_KV_LORA)
    for h in range(NA_HEADS):
        sl = slice(h * NA_HEAD_DIM, (h + 1) * NA_HEAD_DIM)
        qna_ref[:, sl] = _rms(naq_ref[:, sl], naqg_ref[...], NA_HEAD_DIM).astype(BF16)
        kn = _rms(nak_ref[:, sl], nakg_ref[...], NA_HEAD_DIM)
        knaf_ref[:, sl] = kn
        knab_ref[:, sl] = kn.astype(BF16)
    vnab_ref[...] = nav_ref[...].astype(BF16)


def proj_prep(z, rope_c, rope_s1, rope_s2, qng, wuq, qhg, kvng, naqg, nakg, layer):
    t = z.shape[0]
    tm = SEQ_TILE
    zb = lambda k: pl.BlockSpec((tm, 512), lambda i: (i, k))
    rb = pl.BlockSpec((tm, LANES), lambda i: (i, 0))
    wl = lambda *shape: pl.BlockSpec((None,) + shape, lambda i: (layer,) + (0,) * len(shape))
    ob = lambda w: pl.BlockSpec((tm, w), lambda i: (i, 0))
    return pl.pallas_call(
        _proj_prep_kernel,
        grid=(t // tm,),
        in_specs=[zb(0), zb(1), zb(2), zb(3), zb(Z_CKV_BLOCK), rb, rb, rb,
                  wl(1, MLA_Q_LORA), wl(MLA_Q_LORA, MLA_HEADS * MLA_HEAD_PAD), wl(1, MLA_HEAD_PAD),
                  wl(1, MLA_KV_LORA), wl(1, NA_HEAD_DIM), wl(1, NA_HEAD_DIM)],
        out_specs=[ob(MLA_HEADS * MLA_HEAD_PAD), ob(MLA_KV_LORA), ob(NA_WIDTH), ob(NA_WIDTH), ob(NA_WIDTH),
                   ob(NA_WIDTH)],
        out_shape=[jax.ShapeDtypeStruct((t, MLA_HEADS * MLA_HEAD_PAD), BF16),
                   jax.ShapeDtypeStruct((t, MLA_KV_LORA), F32),
                   jax.ShapeDtypeStruct((t, NA_WIDTH), BF16),
                   jax.ShapeDtypeStruct((t, NA_WIDTH), F32),
                   jax.ShapeDtypeStruct((t, NA_WIDTH), BF16),
                   jax.ShapeDtypeStruct((t, NA_WIDTH), BF16)],
        compiler_params=_params(("arbitrary",)),
        name="proj_prep",
    )(z, z, z, z, z, rope_c, rope_s1, rope_s2, qng, wuq, qhg, kvng, naqg, nakg)


def _kv_prep_kernel(ckv_ref, kpe_ref, rc_ref, rs1_ref, rs2_ref, wukv_ref, gn_ref, gp_ref, k_ref, v_ref):
    kv = jnp.dot(ckv_ref[...].astype(BF16), wukv_ref[...], preferred_element_type=F32)
    kpe = kpe_ref[...]
    pe_ss = jnp.sum(kpe * kpe, axis=-1, keepdims=True)
    rc, rs1, rs2 = rc_ref[...], rs1_ref[...], rs2_ref[...]
    for h in range(MLA_HEADS):
        kn = kv[:, h * MLA_NOPE:(h + 1) * MLA_NOPE]
        ms = (jnp.sum(kn * kn, axis=-1, keepdims=True) + pe_ss) / MLA_QK_DIM
        r = lax.rsqrt(ms + NORM_EPS)
        k_ref[:, h * MLA_HEAD_PAD:h * MLA_HEAD_PAD + LANES] = (kn * r * gn_ref[...]).astype(BF16)
        k_ref[:, h * MLA_HEAD_PAD + LANES:(h + 1) * MLA_HEAD_PAD] = (
            _rope128(kpe * r * gp_ref[...], rc, rs1, rs2).astype(BF16))
    v_ref[...] = kv[:, MLA_HEADS * MLA_NOPE:].astype(BF16)


def kv_prep(ckv_all, kpe_all, rope_c, rope_s1, rope_s2, wukv, gn, gp, layer):
    t = ckv_all.shape[0]
    tm = SEQ_TILE
    rb = lambda w: pl.BlockSpec((tm, w), lambda i: (i, 0))
    wl = lambda *shape: pl.BlockSpec((None,) + shape, lambda i: (layer,) + (0,) * len(shape))
    return pl.pallas_call(
        _kv_prep_kernel,
        grid=(t // tm,),
        in_specs=[rb(MLA_KV_LORA), rb(LANES), rb(LANES), rb(LANES), rb(LANES),
                  wl(MLA_KV_LORA, 2 * MLA_HEADS * MLA_NOPE), wl(1, LANES), wl(1, LANES)],
        out_specs=[rb(MLA_HEADS * MLA_HEAD_PAD), rb(MLA_HEADS * MLA_NOPE)],
        out_shape=[jax.ShapeDtypeStruct((t, MLA_HEADS * MLA_HEAD_PAD), BF16),
                   jax.ShapeDtypeStruct((t, MLA_HEADS * MLA_NOPE), BF16)],
        compiler_params=_params(("arbitrary",)),
        name="kv_prep",
    )(ckv_all, kpe_all, rope_c, rope_s1, rope_s2, wukv, gn, gp)


def _attn_kernel(q_ref, k_ref, v_ref, *rest, scale, has_bias):
    if has_bias:
        b_ref, o_ref = rest
    else:
        (o_ref,) = rest
    s = lax.dot_general(q_ref[...], k_ref[...], (((1,), (1,)), ((), ())), preferred_element_type=F32) * scale
    if has_bias:
        s = s + b_ref[...]
    m = jnp.max(s, axis=-1, keepdims=True)
    p = jnp.exp(s - m)
    p = p / jnp.sum(p, axis=-1, keepdims=True)
    o_ref[...] = jnp.dot(p.astype(BF16), v_ref[...], preferred_element_type=F32).astype(BF16)


def attention(q, k, v, *, n_seq, n_heads, dq, sq_total, sk, q_row_off, k_row_off, scale, bias=None):
    sq = 256
    nq = sq_total // sq
    qoff = q_row_off // sq
    koff = k_row_off // sk
    in_specs = [pl.BlockSpec((sq, dq), lambda b, h, qi: (qoff + b * nq + qi, h)),
                pl.BlockSpec((sk, dq), lambda b, h, qi: (koff + b, h)),
                pl.BlockSpec((sk, NA_HEAD_DIM), lambda b, h, qi: (koff + b, h))]
    args = [q, k, v]
    if bias is not None:
        in_specs.append(pl.BlockSpec((None, sq, sk), lambda b, h, qi: (h, qi, 0)))
        args.append(bias)
    return pl.pallas_call(
        functools.partial(_attn_kernel, scale=scale, has_bias=bias is not None),
        grid=(n_seq, n_heads, nq),
        in_specs=in_specs,
        out_specs=pl.BlockSpec((sq, NA_HEAD_DIM), lambda b, h, qi: (b * nq + qi, h)),
        out_shape=jax.ShapeDtypeStruct((n_seq * sq_total, n_heads * NA_HEAD_DIM), BF16),
        compiler_params=_params(("arbitrary", "arbitrary", "arbitrary")),
        name="attention_bias" if bias is not None else "attention",
    )(*args)


def _conv_kernel(gb_ref, gc_ref, u_ref, w_ref, o_ref):
    gu = gc_ref[...] * u_ref[...]
    s = gu.shape[0]
    row = lax.broadcasted_iota(jnp.int32, gu.shape, 0)
    prev = jnp.where(row == 0, 0.0, pltpu.roll(gu, 1, 0))
    nxt = jnp.where(row == s - 1, 0.0, pltpu.roll(gu, s - 1, 0))
    y = prev * w_ref[0:1, :] + gu * w_ref[1:2, :] + nxt * w_ref[2:3, :]
    o_ref[...] = (gb_ref[...] * y).astype(BF16)


def short_conv(z, conv_w8, layer, *, n_seq, seq, row_off):
    off = row_off // seq
    zb = lambda k: pl.BlockSpec((seq, CONV_CH), lambda i: (off + i, k))
    return pl.pallas_call(
        _conv_kernel,
        grid=(n_seq,),
        in_specs=[zb(4), zb(5), zb(6), pl.BlockSpec((None, 8, CONV_CH), lambda i: (layer, 0, 0))],
        out_specs=pl.BlockSpec((seq, CONV_CH), lambda i: (i, 0)),
        out_shape=jax.ShapeDtypeStruct((n_seq * seq, CONV_CH), BF16),
        compiler_params=_params(("arbitrary",)),
        name="short_conv",
    )(z, z, z, conv_w8)


def _out_matmul_kernel(x_ref, g_ref, a0_ref, a1_ref, a2_ref, w0_ref, w1_ref, w2_ref, o_ref, *, tm):
    grp = _group_of_row(pl.program_id(0) * tm)
    acc = jnp.dot(a0_ref[...], w0_ref[...], preferred_element_type=F32)
    acc = acc + jnp.dot(a1_ref[...], w1_ref[...], preferred_element_type=F32)
    acc = acc + jnp.dot(a2_ref[...], w2_ref[...], preferred_element_type=F32)
    o_ref[...] = x_ref[...] + g_ref[pl.ds(grp, 1), :] * acc


def out_matmul(x, mods, o_mla, o_na, conv, w_out, layer, gate_chunk):
    t = x.shape[0]
    tm, tn = ROW_TILE, 1024
    nj = D_MODEL // tn
    w_mla = MLA_HEADS * MLA_NOPE
    return pl.pallas_call(
        functools.partial(_out_matmul_kernel, tm=tm),
        grid=(t // tm, nj),
        in_specs=[pl.BlockSpec((tm, tn), lambda i, j: (i, j)),
                  pl.BlockSpec((None, 8, tn), lambda i, j: (layer, 0, gate_chunk * nj + j)),
                  pl.BlockSpec((tm, w_mla), lambda i, j: (i, 0)),
                  pl.BlockSpec((tm, NA_WIDTH), lambda i, j: (i, 0)),
                  pl.BlockSpec((tm, CONV_CH), lambda i, j: (i, 0)),
                  pl.BlockSpec((None, w_mla, tn), lambda i, j: (layer, 0, j)),
                  pl.BlockSpec((None, NA_WIDTH, tn), lambda i, j: (layer, w_mla // NA_WIDTH, j)),
                  pl.BlockSpec((None, CONV_CH, tn), lambda i, j: (layer, (w_mla + NA_WIDTH) // CONV_CH, j))],
        out_specs=pl.BlockSpec((tm, tn), lambda i, j: (i, j)),
        out_shape=jax.ShapeDtypeStruct((t, D_MODEL), F32),
        compiler_params=_params(("arbitrary", "arbitrary")),
        name="out_matmul",
    )(x, mods, o_mla, o_na, conv, w_out, w_out, w_out)


def _top16(s, want_rank):
    n, width = s.shape
    iota = lax.broadcasted_iota(jnp.int32, (n, width), 0).astype(F32)
    iota16 = lax.broadcasted_iota(jnp.int32, (PEER_TOPK, width), 0)

    def body(j, carry):
        s, mark, vals = carry
        m = jnp.max(s, axis=0, keepdims=True)
        idx = jnp.min(jnp.where(s == m, iota, float(n)), axis=0, keepdims=True)
        hit = iota == idx
        mark = jnp.where(hit, j.astype(F32) if want_rank else 1.0, mark)
        s = jnp.where(hit, -jnp.inf, s)
        vals = jnp.where(iota16 == j, m, vals)
        return s, mark, vals

    mark0 = jnp.full((n, width), float(PEER_TOPK) if want_rank else 0.0, F32)
    _, mark, vals = lax.fori_loop(0, PEER_TOPK, body, (s, mark0, jnp.zeros((PEER_TOPK, width), F32)))
    return mark, vals


def _peer_topk_kernel(q_ref, keys_ref, e1_ref, lb1_ref, r2_ref, e2_ref):
    q = q_ref[...].astype(BF16)
    nt = (((1,), (1,)), ((), ()))
    s1 = lax.dot_general(keys_ref[0], q[:, :LANES], nt, preferred_element_type=F32)
    s2 = lax.dot_general(keys_ref[1], q[:, LANES:], nt, preferred_element_type=F32)
    r1, v1 = _top16(s1, True)
    r2, v2 = _top16(s2, True)
    cand = jnp.concatenate([v1[j:j + 1] + v2 for j in range(PEER_TOPK)], axis=0)
    sel, _ = _top16(cand, False)
    e1s = jnp.exp(v1 - v1[0:1])
    e2s = jnp.exp(v2 - v2[0:1])
    z = jnp.zeros_like(v1[0:1])
    lb1 = jnp.zeros_like(s1)
    for j in range(PEER_TOPK):
        sj = sel[j * PEER_TOPK:(j + 1) * PEER_TOPK]
        z = z + e1s[j:j + 1] * jnp.sum(sj * e2s, axis=0, keepdims=True)
        lb1 = jnp.where(r1 == float(j), jnp.sum(sj, axis=0, keepdims=True), lb1)
    e1_ref[...] = jnp.exp(s1 - v1[0:1])
    lb1_ref[...] = lb1
    r2_ref[...] = r2
    e2_ref[...] = jnp.exp(s2 - v2[0:1]) / z


def peer_topk(q, sub_keys, layer):
    t = q.shape[0]
    ob = pl.BlockSpec((None, PEER_N_KEYS, LANES), lambda i, h: (h, 0, i))
    shp = jax.ShapeDtypeStruct((PEER_HEADS, PEER_N_KEYS, t), F32)
    return pl.pallas_call(
        _peer_topk_kernel,
        grid=(t // LANES, PEER_HEADS),
        in_specs=[pl.BlockSpec((LANES, 2 * LANES), lambda i, h: (i, h)),
                  pl.BlockSpec((None, None, 2, PEER_N_KEYS, LANES), lambda i, h: (layer, h, 0, 0, 0))],
        out_specs=[ob, ob, ob, ob],
        out_shape=[shp, shp, shp, shp],
        compiler_params=_params(("arbitrary", "arbitrary")),
        name="peer_topk",
    )(q, sub_keys)


_SQRT_HALF = float(np.sqrt(0.5))


def _peer_dense_kernel(ht_ref, u_ref, v_ref, e1_ref, lb1_ref, r2_ref, e2_ref, x_ref, g_ref, o_ref,
                       acc_ref, at_ref, wg_ref, *, tt, ec):
    i = pl.program_id(0)
    c = pl.program_id(1)

    @pl.when(c == 0)
    def _():
        acc_ref[...] = jnp.zeros_like(acc_ref)

    at_ref[...] = jnp.dot(u_ref[...], ht_ref[...], preferred_element_type=F32)
    n_a = ec // PEER_N_KEYS
    grp0 = pl.multiple_of((c * n_a) // 8 * 8, 8)
    off = (c * n_a) % 8

    def key_row(ref, h, al, lanes):
        blk = ref[h, pl.ds(grp0, 8), lanes]
        row = blk[al:al + 1]
        for o in range(n_a, 8, n_a):
            row = jnp.where(off == o, blk[o + al:o + al + 1], row)
        return row

    for al in range(n_a):
        rows = slice(al * PEER_N_KEYS, (al + 1) * PEER_N_KEYS)
        for lt in range(tt // LANES):
            lanes = slice(lt * LANES, (lt + 1) * LANES)
            gate = jnp.zeros((PEER_N_KEYS, LANES), F32)
            for h in range(PEER_HEADS):
                lb = key_row(lb1_ref, h, al, lanes)
                e1 = key_row(e1_ref, h, al, lanes)
                gate = gate + jnp.where(r2_ref[h, :, lanes] < lb, e2_ref[h, :, lanes], 0.0) * e1
            act = at_ref[rows, lanes]
            wg_ref[rows, lanes] = 0.5 * act * (1.0 + lax.erf(act * _SQRT_HALF)) * gate
    acc_ref[...] += jnp.dot(wg_ref[...].T.astype(BF16), v_ref[...], preferred_element_type=F32)

    @pl.when(c == pl.num_programs(1) - 1)
    def _():
        grp = _group_of_row(i * tt)
        o_ref[...] = x_ref[...] + g_ref[pl.ds(grp, 1), :] * acc_ref[...]


def peer_dense(ht, peer_u, peer_v, e1, lb1, r2, e2, x, mods, layer, gate_chunk):
    t = x.shape[0]
    tt, ec = PEER_TOK_TILE, PEER_EXP_TILE
    kb = pl.BlockSpec((PEER_HEADS, PEER_N_KEYS, tt), lambda i, c: (0, 0, i))
    return pl.pallas_call(
        functools.partial(_peer_dense_kernel, tt=tt, ec=ec),
        grid=(t // tt, PEER_N_EXPERTS // ec),
        in_specs=[pl.BlockSpec((D_MODEL, tt), lambda i, c: (0, i)),
                  pl.BlockSpec((None, ec, D_MODEL), lambda i, c: (layer, c, 0)),
                  pl.BlockSpec((None, ec, D_MODEL), lambda i, c: (layer, c, 0)),
                  kb, kb, kb, kb,
                  pl.BlockSpec((tt, D_MODEL), lambda i, c: (i, 0)),
                  pl.BlockSpec((None, 8, D_MODEL), lambda i, c: (layer, 0, gate_chunk))],
        out_specs=pl.BlockSpec((tt, D_MODEL), lambda i, c: (i, 0)),
        out_shape=jax.ShapeDtypeStruct((t, D_MODEL), F32),
        scratch_shapes=[pltpu.VMEM((tt, D_MODEL), F32),
                        pltpu.VMEM((ec, tt), F32),
                        pltpu.VMEM((ec, tt), F32)],
        compiler_params=_params(("arbitrary", "arbitrary")),
        name="peer_dense",
    )(ht, peer_u, peer_v, e1, lb1, r2, e2, x, mods)


def _rope_tables():
    t = jnp.arange(LAT_SEQ)
    row = (t // GRID_W).astype(F32)
    col = (t % GRID_W).astype(F32)
    n_freq = MLA_ROPE // 4
    inv = ROPE_THETA ** (-jnp.arange(n_freq, dtype=F32) / n_freq)
    ang = jnp.concatenate([row[:, None] * inv, col[:, None] * inv], axis=-1)
    cos, sin = jnp.cos(ang), jnp.sin(ang)
    zero = jnp.zeros_like(cos)
    pad = jnp.zeros((LAT_SEQ, LANES - MLA_ROPE), F32)
    c_lat = jnp.concatenate([cos, cos, pad], axis=-1)
    s1_lat = jnp.concatenate([-sin, zero, pad], axis=-1)
    s2_lat = jnp.concatenate([zero, sin, pad], axis=-1)
    ones = jnp.concatenate([jnp.ones((1, MLA_ROPE), F32), jnp.zeros((1, LANES - MLA_ROPE), F32)], axis=-1)

    def full(lat, ident):
        return jnp.concatenate([jnp.broadcast_to(ident, (T_CTX, LANES)), jnp.tile(lat, (N_LAT_SEQ, 1)),
                                jnp.broadcast_to(ident, (T_CACHE, LANES))], axis=0)

    zeros = jnp.zeros((1, LANES), F32)
    return full(c_lat, ones), full(s1_lat, zeros), full(s2_lat, zeros)


def _na_bias(rel_bias):
    rows = LAT_SEQ // GRID_W
    tq = np.arange(LAT_SEQ)
    rq, cq = tq // GRID_W, tq % GRID_W
    rk, ck = rq, cq
    row_start = np.clip(rq - NA_KR // 2, 0, rows - NA_KR)
    col_start = np.clip(cq - NA_KC // 2, 0, GRID_W - NA_KC)
    valid = ((rk[None, :] >= row_start[:, None]) & (rk[None, :] < row_start[:, None] + NA_KR)
             & (ck[None, :] >= col_start[:, None]) & (ck[None, :] < col_start[:, None] + NA_KC))
    roff = np.clip(rk[None, :] - rq[:, None] + (NA_KR - 1), 0, 2 * NA_KR - 2)
    coff = np.clip(ck[None, :] - cq[:, None], -(NA_KC - 1), NA_KC - 1) + (NA_KC - 1)
    b = rel_bias[:, :, roff, coff]
    b = jnp.where(valid[None, None], b, -jnp.inf)
    return jnp.concatenate([jnp.zeros(b.shape[:3] + (PAST_LEN,), F32), b], axis=-1)


def kernel(x_prompt, x_sample, cache_mla_ckv, cache_mla_kpe, cache_na_k, cache_na_v, c, c_ctx, ada_w, ada_b, norm_mix_g, norm_ffn_g, w_in, mla_q_norm_g, mla_w_uq, mla_kv_norm_g, mla_w_ukv, mla_q_head_g, mla_k_head_g, na_q_head_g, na_k_head_g, na_rel_bias, conv_w, w_out, peer_w_q, peer_sub_keys, peer_u, peer_v):
    w_in_p = jnp.concatenate([w_in[..., :512], w_in[..., 832:3904], w_in[..., 512:832],
                              jnp.zeros((DEPTH, D_MODEL, D_IN_PAD - 3904), F32)], axis=-1).astype(BF16)
    wuq_p = jnp.pad(mla_w_uq.reshape(DEPTH, MLA_Q_LORA, MLA_HEADS, MLA_QK_DIM),
                    ((0, 0), (0, 0), (0, 0), (0, MLA_HEAD_PAD - MLA_QK_DIM))
                    ).reshape(DEPTH, MLA_Q_LORA, MLA_HEADS * MLA_HEAD_PAD).astype(BF16)
    qhg_p = jnp.pad(mla_q_head_g, ((0, 0), (0, MLA_HEAD_PAD - MLA_QK_DIM))).reshape(DEPTH, 1, MLA_HEAD_PAD)
    wukv4 = mla_w_ukv.reshape(DEPTH, MLA_KV_LORA, MLA_HEADS, 2 * MLA_NOPE)
    wukv_p = jnp.concatenate([wukv4[..., :MLA_NOPE].reshape(DEPTH, MLA_KV_LORA, -1),
                              wukv4[..., MLA_NOPE:].reshape(DEPTH, MLA_KV_LORA, -1)], axis=-1).astype(BF16)
    khg_n = mla_k_head_g[:, :MLA_NOPE].reshape(DEPTH, 1, LANES)
    khg_p = jnp.pad(mla_k_head_g[:, MLA_NOPE:], ((0, 0), (0, LANES - MLA_ROPE))).reshape(DEPTH, 1, LANES)
    conv_w8 = jnp.pad(conv_w.transpose(0, 2, 1), ((0, 0), (0, 5), (0, 0)))
    w_out_b = w_out.astype(BF16)
    peer_wq_b = peer_w_q.astype(BF16)
    sub_keys_b = peer_sub_keys.astype(BF16)
    peer_u_b = peer_u.astype(BF16)
    peer_v_b = peer_v.astype(BF16)
    g_mix = norm_mix_g.reshape(DEPTH, 1, D_MODEL)
    g_ffn = norm_ffn_g.reshape(DEPTH, 1, D_MODEL)
    qng = mla_q_norm_g.reshape(DEPTH, 1, MLA_Q_LORA)
    kvng = mla_kv_norm_g.reshape(DEPTH, 1, MLA_KV_LORA)
    naqg = na_q_head_g.reshape(DEPTH, 1, NA_HEAD_DIM)
    nakg = na_k_head_g.reshape(DEPTH, 1, NA_HEAD_DIM)
    rope_c, rope_s1, rope_s2 = _rope_tables()
    na_bias = _na_bias(na_rel_bias)
    cache_kpe_p = jnp.pad(cache_mla_kpe, ((0, 0), (0, 0), (0, 0), (0, LANES - MLA_ROPE)))
    cache_nak = cache_na_k.transpose(1, 0, 3, 2, 4).reshape(DEPTH, N_LAT_SEQ, PAST_LEN, NA_WIDTH).astype(BF16)
    cache_nav = cache_na_v.transpose(1, 0, 3, 2, 4).reshape(DEPTH, N_LAT_SEQ, PAST_LEN, NA_WIDTH).astype(BF16)

    cpad = jnp.concatenate([c_ctx[None, :], c, jnp.zeros((8 - 1 - N_LAT_SEQ, D_MODEL), F32)], axis=0)
    mods = ada_all(cpad, ada_w, ada_b)

    x = jnp.concatenate([x_prompt.reshape(T_CTX, D_MODEL), x_sample.reshape(T_LAT, D_MODEL)], axis=0)
    ckv_out, kpe_out, nak_out, nav_out = [], [], [], []
    mla_scale = MLA_QK_DIM ** -0.5
    na_scale = NA_HEAD_DIM ** -0.5

    def lat_keys(lat, cached):
        lat = lat.reshape(N_LAT_SEQ, LAT_SEQ, -1)
        return jnp.concatenate([cached, lat], axis=1).reshape(N_LAT_SEQ * (PAST_LEN + LAT_SEQ), -1)

    for l in range(DEPTH):
        (z,) = mod_matmul(x, g_mix, mods, w_in_p, l, 0, 1, tn=1024, emit_ht=False)
        q_mla, ckv_n, q_na, k_na_f, k_na_b, v_na_b = proj_prep(
            z, rope_c[:T_ALL], rope_s1[:T_ALL], rope_s2[:T_ALL], qng, wuq_p, qhg_p, kvng, naqg, nakg, l)
        ckv_all = jnp.concatenate([ckv_n, cache_mla_ckv[:, l].reshape(T_CACHE, MLA_KV_LORA)], axis=0)
        kpe_all = jnp.concatenate([z[:, Z_KPE_COL:Z_KPE_COL + LANES], cache_kpe_p[:, l].reshape(T_CACHE, LANES)],
                                  axis=0)
        k_mla, v_mla = kv_prep(ckv_all, kpe_all, rope_c, rope_s1, rope_s2, wukv_p, khg_n, khg_p, l)

        o_mla_ctx = attention(q_mla, k_mla, v_mla, n_seq=N_CTX_SEQ, n_heads=MLA_HEADS, dq=MLA_HEAD_PAD,
                              sq_total=CTX_SEQ, sk=CTX_SEQ, q_row_off=0, k_row_off=0, scale=mla_scale)
        k_lat = lat_keys(k_mla[T_CTX:T_ALL], k_mla[T_ALL:].reshape(N_LAT_SEQ, PAST_LEN, -1))
        v_lat = lat_keys(v_mla[T_CTX:T_ALL], v_mla[T_ALL:].reshape(N_LAT_SEQ, PAST_LEN, -1))
        o_mla_lat = attention(q_mla, k_lat, v_lat, n_seq=N_LAT_SEQ, n_heads=MLA_HEADS, dq=MLA_HEAD_PAD,
                              sq_total=LAT_SEQ, sk=PAST_LEN + LAT_SEQ, q_row_off=T_CTX, k_row_off=0,
                              scale=mla_scale)
        o_na_ctx = attention(q_na, k_na_b, v_na_b, n_seq=N_CTX_SEQ, n_heads=NA_HEADS, dq=NA_HEAD_DIM,
                             sq_total=CTX_SEQ, sk=CTX_SEQ, q_row_off=0, k_row_off=0, scale=na_scale)
        kn_lat = lat_keys(k_na_b[T_CTX:], cache_nak[l])
        vn_lat = lat_keys(v_na_b[T_CTX:], cache_nav[l])
        o_na_lat = attention(q_na, kn_lat, vn_lat, n_seq=N_LAT_SEQ, n_heads=NA_HEADS, dq=NA_HEAD_DIM,
                             sq_total=LAT_SEQ, sk=PAST_LEN + LAT_SEQ, q_row_off=T_CTX, k_row_off=0,
                             scale=na_scale, bias=na_bias[l])
        conv_ctx = short_conv(z, conv_w8, l, n_seq=N_CTX_SEQ, seq=CTX_SEQ, row_off=0)
        conv_lat = short_conv(z, conv_w8, l, n_seq=N_LAT_SEQ, seq=LAT_SEQ, row_off=T_CTX)

        x = out_matmul(x, mods, jnp.concatenate([o_mla_ctx, o_mla_lat], axis=0),
                       jnp.concatenate([o_na_ctx, o_na_lat], axis=0),
                       jnp.concatenate([conv_ctx, conv_lat], axis=0), w_out_b, l, 2)

        q_peer, ht = mod_matmul(x, g_ffn, mods, peer_wq_b, l, 3, 4, tn=1024, emit_ht=True)
        e1, lb1, r2, e2 = peer_topk(q_peer, sub_keys_b, l)
        x = peer_dense(ht, peer_u_b, peer_v_b, e1, lb1, r2, e2, x, mods, l, 5)

        ckv_out.append(ckv_n[:T_CTX].reshape(N_CTX_SEQ, CTX_SEQ, MLA_KV_LORA))
        kpe_out.append(z[:T_CTX, Z_KPE_COL:Z_KPE_COL + MLA_ROPE].reshape(N_CTX_SEQ, CTX_SEQ, MLA_ROPE))
        nak_out.append(k_na_f[:T_CTX].reshape(N_CTX_SEQ, CTX_SEQ, NA_HEADS, NA_HEAD_DIM).transpose(0, 2, 1, 3))
        nav_out.append(z[:T_CTX, 1536:2048].reshape(N_CTX_SEQ, CTX_SEQ, NA_HEADS, NA_HEAD_DIM
                                                    ).transpose(0, 2, 1, 3))

    y_prompt = x[:T_CTX].reshape(N_CTX_SEQ, CTX_SEQ, D_MODEL)
    y_sample = x[T_CTX:].reshape(N_LAT_SEQ, LAT_SEQ, D_MODEL)
    return (y_prompt, y_sample, jnp.stack(ckv_out, axis=1), jnp.stack(kpe_out, axis=1),
            jnp.stack(nak_out, axis=1), jnp.stack(nav_out, axis=1))
```

```python
import functools

import numpy as np
import jax
import jax.numpy as jnp
from jax import lax
from jax.experimental import pallas as pl
from jax.experimental.pallas import tpu as pltpu

F32 = jnp.float32
BF16 = jnp.bfloat16

D_MODEL = 2048
DEPTH = 4
N_CTX_SEQ = 32
CTX_SEQ = 256
N_LAT_SEQ = 2
LAT_SEQ = 1024
PAST_LEN = 512
T_CTX = N_CTX_SEQ * CTX_SEQ
T_LAT = N_LAT_SEQ * LAT_SEQ
T_ALL = T_CTX + T_LAT
T_CACHE = N_LAT_SEQ * PAST_LEN
GRID_W = 64
NORM_EPS = 1e-6
ROPE_THETA = 10000.0
ADA_CHUNKS = 6

MLA_HEADS = 8
MLA_NOPE = 128
MLA_ROPE = 64
MLA_QK_DIM = MLA_NOPE + MLA_ROPE
MLA_Q_LORA = 512
MLA_KV_LORA = 256
MLA_HEAD_PAD = 256
NA_HEADS = 4
NA_HEAD_DIM = 128
NA_WIDTH = NA_HEADS * NA_HEAD_DIM
NA_KR = 8
NA_KC = 16
CONV_CH = 512
D_IN_PAD = 4096
Z_CKV_BLOCK = 7
Z_KPE_COL = 3840

PEER_HEADS = 8
PEER_N_KEYS = 128
PEER_N_EXPERTS = PEER_N_KEYS * PEER_N_KEYS
PEER_TOPK = 16

LANES = 128
VMEM_LIMIT = 56 * 1024 * 1024

ROW_TILE = 1024
SEQ_TILE = 256
PEER_TOK_TILE = 512
PEER_EXP_TILE = 512


def _params(sem, flags=None):
    return pltpu.CompilerParams(dimension_semantics=sem, vmem_limit_bytes=VMEM_LIMIT, flags=flags)


def _group_of_row(row):
    return jnp.where(row < T_CTX, 0, 1 + (row - T_CTX) // LAT_SEQ)


def _rms(x, g, n):
    ms = jnp.sum(x * x, axis=-1, keepdims=True) / n
    return x * lax.rsqrt(ms + NORM_EPS) * g


def _ada_kernel(c_ref, w_ref, b_ref, o_ref):
    c = c_ref[...]
    s = c * jax.nn.sigmoid(c)
    o_ref[...] = jnp.dot(s.astype(BF16), w_ref[...].astype(BF16), preferred_element_type=F32) + b_ref[...]


def ada_all(cpad, ada_w, ada_b):
    tn = 1536
    n = ADA_CHUNKS * D_MODEL
    return pl.pallas_call(
        _ada_kernel,
        grid=(DEPTH, n // tn),
        in_specs=[pl.BlockSpec((8, D_MODEL), lambda l, j: (0, 0)),
                  pl.BlockSpec((None, D_MODEL, tn), lambda l, j: (l, 0, j)),
                  pl.BlockSpec((None, 1, tn), lambda l, j: (l, 0, j))],
        out_specs=pl.BlockSpec((None, 8, tn), lambda l, j: (l, 0, j)),
        out_shape=jax.ShapeDtypeStruct((DEPTH, 8, n), F32),
        compiler_params=_params(("arbitrary", "arbitrary")),
        name="ada",
    )(cpad, ada_w, ada_b.reshape(DEPTH, 1, n))


def _mod_matmul_kernel(x_ref, g_ref, sh_ref, sc_ref, w_ref, z_ref, *rest, tm, emit_ht):
    if emit_ht:
        ht_ref, h_scr = rest
    else:
        (h_scr,) = rest
    i = pl.program_id(0)

    @pl.when(pl.program_id(1) == 0)
    def _():
        grp = _group_of_row(i * tm)
        y = _rms(x_ref[...], g_ref[...], D_MODEL)
        h = y * (1.0 + sc_ref[pl.ds(grp, 1), :]) + sh_ref[pl.ds(grp, 1), :]
        h_scr[...] = h.astype(BF16)
        if emit_ht:
            ht_ref[...] = h.T.astype(BF16)

    z_ref[...] = jnp.dot(h_scr[...], w_ref[...], preferred_element_type=F32)


def mod_matmul(x, gain, mods, w, layer, shift_chunk, scale_chunk, *, tn, emit_ht):
    t = x.shape[0]
    n = w.shape[-1]
    tm = ROW_TILE
    out_shape = [jax.ShapeDtypeStruct((t, n), F32)]
    out_specs = [pl.BlockSpec((tm, tn), lambda i, j: (i, j))]
    if emit_ht:
        out_shape.append(jax.ShapeDtypeStruct((D_MODEL, t), BF16))
        out_specs.append(pl.BlockSpec((D_MODEL, tm), lambda i, j: (0, i)))
    res = pl.pallas_call(
        functools.partial(_mod_matmul_kernel, tm=tm, emit_ht=emit_ht),
        grid=(t // tm, n // tn),
        in_specs=[pl.BlockSpec((tm, D_MODEL), lambda i, j: (i, 0)),
                  pl.BlockSpec((None, 1, D_MODEL), lambda i, j: (layer, 0, 0)),
                  pl.BlockSpec((None, 8, D_MODEL), lambda i, j: (layer, 0, shift_chunk)),
                  pl.BlockSpec((None, 8, D_MODEL), lambda i, j: (layer, 0, scale_chunk)),
                  pl.BlockSpec((None, D_MODEL, tn), lambda i, j: (layer, 0, j))],
        out_specs=out_specs,
        out_shape=out_shape,
        scratch_shapes=[pltpu.VMEM((tm, D_MODEL), BF16)],
        compiler_params=_params(("arbitrary", "arbitrary")),
        name="mod_matmul_ht" if emit_ht else "mod_matmul",
    )(x, gain, mods, mods, w)
    return res


def _rope128(x, c, s1, s2):
    return x * c + pltpu.roll(x, 96, 1) * s1 + pltpu.roll(x, 32, 1) * s2


def _proj_prep_kernel(cq_ref, naq_ref, nak_ref, nav_ref, ckv_ref, rc_ref, rs1_ref, rs2_ref,
                      qng_ref, wuq_ref, qhg_ref, kvng_ref, naqg_ref, nakg_ref,
                      qmla_ref, ckvn_ref, qna_ref, knaf_ref, knab_ref, vnab_ref):
    cqn = _rms(cq_ref[...], qng_ref[...], MLA_Q_LORA)
    q = jnp.dot(cqn.astype(BF16), wuq_ref[...], preferred_element_type=F32)
    rc, rs1, rs2 = rc_ref[...], rs1_ref[...], rs2_ref[...]
    for h in range(MLA_HEADS):
        qh = _rms(q[:, h * MLA_HEAD_PAD:(h + 1) * MLA_HEAD_PAD], qhg_ref[...], MLA_QK_DIM)
        qmla_ref[:, h * MLA_HEAD_PAD:h * MLA_HEAD_PAD + LANES] = qh[:, :LANES].astype(BF16)
        qmla_ref[:, h * MLA_HEAD_PAD + LANES:(h + 1) * MLA_HEAD_PAD] = (
            _rope128(qh[:, LANES:], rc, rs1, rs2).astype(BF16))
    ckvn_ref[...] = _rms(ckv_ref[:, :MLA_KV_LORA], kvng_ref[...], MLA_KV_LORA)
    for h in range(NA_HEADS):
        sl = slice(h * NA_HEAD_DIM, (h + 1) * NA_HEAD_DIM)
        qna_ref[:, sl] = _rms(naq_ref[:, sl], naqg_ref[...], NA_HEAD_DIM).astype(BF16)
        kn = _rms(nak_ref[:, sl], nakg_ref[...], NA_HEAD_DIM)
        knaf_ref[:, sl] = kn
        knab_ref[:, sl] = kn.astype(BF16)
    vnab_ref[...] = nav_ref[...].astype(BF16)


def proj_prep(z, rope_c, rope_s1, rope_s2, qng, wuq, qhg, kvng, naqg, nakg, layer):
    t = z.shape[0]
    tm = SEQ_TILE
    zb = lambda k: pl.BlockSpec((tm, 512), lambda i: (i, k))
    rb = pl.BlockSpec((tm, LANES), lambda i: (i, 0))
    wl = lambda *shape: pl.BlockSpec((None,) + shape, lambda i: (layer,) + (0,) * len(shape))
    ob = lambda w: pl.BlockSpec((tm, w), lambda i: (i, 0))
    return pl.pallas_call(
        _proj_prep_kernel,
        grid=(t // tm,),
        in_specs=[zb(0), zb(1), zb(2), zb(3), zb(Z_CKV_BLOCK), rb, rb, rb,
                  wl(1, MLA_Q_LORA), wl(MLA_Q_LORA, MLA_HEADS * MLA_HEAD_PAD), wl(1, MLA_HEAD_PAD),
                  wl(1, MLA_KV_LORA), wl(1, NA_HEAD_DIM), wl(1, NA_HEAD_DIM)],
        out_specs=[ob(MLA_HEADS * MLA_HEAD_PAD), ob(MLA_KV_LORA), ob(NA_WIDTH), ob(NA_WIDTH), ob(NA_WIDTH),
                   ob(NA_WIDTH)],
        out_shape=[jax.ShapeDtypeStruct((t, MLA_HEADS * MLA_HEAD_PAD), BF16),
                   jax.ShapeDtypeStruct((t, MLA_KV_LORA), F32),
                   jax.ShapeDtypeStruct((t, NA_WIDTH), BF16),
                   jax.ShapeDtypeStruct((t, NA_WIDTH), F32),
                   jax.ShapeDtypeStruct((t, NA_WIDTH), BF16),
                   jax.ShapeDtypeStruct((t, NA_WIDTH), BF16)],
        compiler_params=_params(("arbitrary",)),
        name="proj_prep",
    )(z, z, z, z, z, rope_c, rope_s1, rope_s2, qng, wuq, qhg, kvng, naqg, nakg)


def _kv_prep_kernel(ckv_ref, kpe_ref, rc_ref, rs1_ref, rs2_ref, wukv_ref, gn_ref, gp_ref, k_ref, v_ref):
    kv = jnp.dot(ckv_ref[...].astype(BF16), wukv_ref[...], preferred_element_type=F32)
    kpe = kpe_ref[...]
    pe_ss = jnp.sum(kpe * kpe, axis=-1, keepdims=True)
    rc, rs1, rs2 = rc_ref[...], rs1_ref[...], rs2_ref[...]
    for h in range(MLA_HEADS):
        kn = kv[:, h * MLA_NOPE:(h + 1) * MLA_NOPE]
        ms = (jnp.sum(kn * kn, axis=-1, keepdims=True) + pe_ss) / MLA_QK_DIM
        r = lax.rsqrt(ms + NORM_EPS)
        k_ref[:, h * MLA_HEAD_PAD:h * MLA_HEAD_PAD + LANES] = (kn * r * gn_ref[...]).astype(BF16)
        k_ref[:, h * MLA_HEAD_PAD + LANES:(h + 1) * MLA_HEAD_PAD] = (
            _rope128(kpe * r * gp_ref[...], rc, rs1, rs2).astype(BF16))
    v_ref[...] = kv[:, MLA_HEADS * MLA_NOPE:].astype(BF16)


def kv_prep(ckv_n, z, rope_c, rope_s1, rope_s2, wukv, gn, gp, layer):
    t = ckv_n.shape[0]
    tm = SEQ_TILE
    rb = lambda w: pl.BlockSpec((tm, w), lambda i: (i, 0))
    wl = lambda *shape: pl.BlockSpec((None,) + shape, lambda i: (layer,) + (0,) * len(shape))
    return pl.pallas_call(
        _kv_prep_kernel,
        grid=(t // tm,),
        in_specs=[rb(MLA_KV_LORA), pl.BlockSpec((tm, LANES), lambda i: (i, Z_KPE_COL // LANES)),
                  rb(LANES), rb(LANES), rb(LANES),
                  wl(MLA_KV_LORA, 2 * MLA_HEADS * MLA_NOPE), wl(1, LANES), wl(1, LANES)],
        out_specs=[rb(MLA_HEADS * MLA_HEAD_PAD), rb(MLA_HEADS * MLA_NOPE)],
        out_shape=[jax.ShapeDtypeStruct((t, MLA_HEADS * MLA_HEAD_PAD), BF16),
                   jax.ShapeDtypeStruct((t, MLA_HEADS * MLA_NOPE), BF16)],
        compiler_params=_params(("arbitrary",)),
        name="kv_prep",
    )(ckv_n, z, rope_c, rope_s1, rope_s2, wukv, gn, gp)


def kv_prep_cache(ckv, kpe, ident_c, ident_s, wukv, gn, gp):
    tm = SEQ_TILE
    rb = lambda w: pl.BlockSpec((None, tm, w), lambda l, i: (l, i, 0))
    tb = pl.BlockSpec((tm, LANES), lambda l, i: (0, 0))
    wl = lambda *shape: pl.BlockSpec((None,) + shape, lambda l, i: (l,) + (0,) * len(shape))
    return pl.pallas_call(
        _kv_prep_kernel,
        grid=(DEPTH, T_CACHE // tm),
        in_specs=[rb(MLA_KV_LORA), rb(LANES), tb, tb, tb,
                  wl(MLA_KV_LORA, 2 * MLA_HEADS * MLA_NOPE), wl(1, LANES), wl(1, LANES)],
        out_specs=[rb(MLA_HEADS * MLA_HEAD_PAD), rb(MLA_HEADS * MLA_NOPE)],
        out_shape=[jax.ShapeDtypeStruct((DEPTH, T_CACHE, MLA_HEADS * MLA_HEAD_PAD), BF16),
                   jax.ShapeDtypeStruct((DEPTH, T_CACHE, MLA_HEADS * MLA_NOPE), BF16)],
        compiler_params=_params(("arbitrary", "arbitrary")),
        name="kv_prep_cache",
    )(ckv, kpe, ident_c, ident_s, ident_s, wukv, gn, gp)


_NT = (((1,), (1,)), ((), ()))


def _attn_kernel(q_ref, k_ref, v_ref, o_ref, *, scale, n_heads, dq):
    dv = NA_HEAD_DIM
    for h in range(n_heads):
        q = q_ref[:, h * dq:(h + 1) * dq]
        k = k_ref[:, h * dq:(h + 1) * dq]
        s = lax.dot_general(q, k, _NT, preferred_element_type=F32) * scale
        m = jnp.max(s, axis=-1, keepdims=True)
        p = jnp.exp(s - m)
        p = p / jnp.sum(p, axis=-1, keepdims=True)
        o = jnp.dot(p.astype(BF16), v_ref[:, h * dv:(h + 1) * dv], preferred_element_type=F32)
        o_ref[:, h * dv:(h + 1) * dv] = o.astype(BF16)


def attention(q, k, v, *, n_seq, n_heads, dq, sq_total, sk, q_row_off, k_row_off, scale):
    sq = 256
    nq = sq_total // sq
    qoff = q_row_off // sq
    koff = k_row_off // sk
    dv = NA_HEAD_DIM
    return pl.pallas_call(
        functools.partial(_attn_kernel, scale=scale, n_heads=n_heads, dq=dq),
        grid=(n_seq, nq),
        in_specs=[pl.BlockSpec((sq, n_heads * dq), lambda b, qi: (qoff + b * nq + qi, 0)),
                  pl.BlockSpec((sk, n_heads * dq), lambda b, qi: (koff + b, 0)),
                  pl.BlockSpec((sk, n_heads * dv), lambda b, qi: (koff + b, 0))],
        out_specs=pl.BlockSpec((sq, n_heads * dv), lambda b, qi: (b * nq + qi, 0)),
        out_shape=jax.ShapeDtypeStruct((n_seq * sq_total, n_heads * dv), BF16),
        compiler_params=_params(("arbitrary", "arbitrary")),
        name="attention",
    )(q, k, v)


def _attn_cached_kernel(q_ref, k_ref, v_ref, kc_ref, vc_ref, o_ref, *, scale, n_heads, dq):
    dv = NA_HEAD_DIM
    for h in range(n_heads):
        qk = slice(h * dq, (h + 1) * dq)
        vv = slice(h * dv, (h + 1) * dv)
        q = q_ref[:, qk]
        s_own = lax.dot_general(q, k_ref[:, qk], _NT, preferred_element_type=F32) * scale
        s_ctx = lax.dot_general(q, kc_ref[:, qk], _NT, preferred_element_type=F32) * scale
        m = jnp.maximum(jnp.max(s_own, axis=-1, keepdims=True), jnp.max(s_ctx, axis=-1, keepdims=True))
        p_own = jnp.exp(s_own - m)
        p_ctx = jnp.exp(s_ctx - m)
        denom = jnp.sum(p_own, axis=-1, keepdims=True) + jnp.sum(p_ctx, axis=-1, keepdims=True)
        o = jnp.dot((p_ctx / denom).astype(BF16), vc_ref[:, vv], preferred_element_type=F32)
        o = o + jnp.dot((p_own / denom).astype(BF16), v_ref[:, vv], preferred_element_type=F32)
        o_ref[:, vv] = o.astype(BF16)


def latent_mla_attention(q, k, v, k_cache, v_cache, layer, *, scale):
    sq = 256
    nq = LAT_SEQ // sq
    qoff = T_CTX // sq
    koff = T_CTX // LAT_SEQ
    wq = MLA_HEADS * MLA_HEAD_PAD
    wv = MLA_HEADS * NA_HEAD_DIM
    return pl.pallas_call(
        functools.partial(_attn_cached_kernel, scale=scale, n_heads=MLA_HEADS, dq=MLA_HEAD_PAD),
        grid=(N_LAT_SEQ, nq),
        in_specs=[pl.BlockSpec((sq, wq), lambda b, qi: (qoff + b * nq + qi, 0)),
                  pl.BlockSpec((LAT_SEQ, wq), lambda b, qi: (koff + b, 0)),
                  pl.BlockSpec((LAT_SEQ, wv), lambda b, qi: (koff + b, 0)),
                  pl.BlockSpec((None, PAST_LEN, wq), lambda b, qi: (layer, b, 0)),
                  pl.BlockSpec((None, PAST_LEN, wv), lambda b, qi: (layer, b, 0))],
        out_specs=pl.BlockSpec((sq, wv), lambda b, qi: (b * nq + qi, 0)),
        out_shape=jax.ShapeDtypeStruct((T_LAT, wv), BF16),
        compiler_params=_params(("arbitrary", "arbitrary")),
        name="latent_mla",
    )(q, k, v, k_cache, v_cache)


NA_ROWS = LAT_SEQ // GRID_W
NA_LOCAL = NA_KR * GRID_W


def _na_lat_kernel(q_ref, k_ref, v_ref, kc_ref, vc_ref, b_ref, o_ref, *, scale):
    rq = pl.program_id(1)
    row_start = jnp.clip(rq - NA_KR // 2, 0, NA_ROWS - NA_KR)
    start = pl.multiple_of(row_start * GRID_W, GRID_W)
    for h in range(NA_HEADS):
        cols = slice(h * NA_HEAD_DIM, (h + 1) * NA_HEAD_DIM)
        q = q_ref[:, cols]
        k_loc = k_ref[pl.ds(start, NA_LOCAL), cols]
        v_loc = v_ref[pl.ds(start, NA_LOCAL), cols]
        s_loc = lax.dot_general(q, k_loc, _NT, preferred_element_type=F32) * scale + b_ref[h, rq - row_start]
        s_ctx = lax.dot_general(q, kc_ref[:, cols], _NT, preferred_element_type=F32) * scale
        m = jnp.maximum(jnp.max(s_loc, axis=-1, keepdims=True), jnp.max(s_ctx, axis=-1, keepdims=True))
        p_loc = jnp.exp(s_loc - m)
        p_ctx = jnp.exp(s_ctx - m)
        denom = jnp.sum(p_loc, axis=-1, keepdims=True) + jnp.sum(p_ctx, axis=-1, keepdims=True)
        o = jnp.dot((p_loc / denom).astype(BF16), v_loc, preferred_element_type=F32)
        o = o + jnp.dot((p_ctx / denom).astype(BF16), vc_ref[:, cols], preferred_element_type=F32)
        o_ref[:, cols] = o.astype(BF16)


def na_latent_attention(q, k, v, k_cache, v_cache, bias_tab, layer, *, scale):
    qoff = T_CTX // GRID_W
    koff = T_CTX // LAT_SEQ
    w = NA_WIDTH
    return pl.pallas_call(
        functools.partial(_na_lat_kernel, scale=scale),
        grid=(N_LAT_SEQ, NA_ROWS),
        in_specs=[pl.BlockSpec((GRID_W, w), lambda b, r: (qoff + b * NA_ROWS + r, 0)),
                  pl.BlockSpec((LAT_SEQ, w), lambda b, r: (koff + b, 0)),
                  pl.BlockSpec((LAT_SEQ, w), lambda b, r: (koff + b, 0)),
                  pl.BlockSpec((None, None, PAST_LEN, w), lambda b, r: (layer, b, 0, 0)),
                  pl.BlockSpec((None, None, PAST_LEN, w), lambda b, r: (layer, b, 0, 0)),
                  pl.BlockSpec((None, NA_HEADS, NA_KR, GRID_W, NA_LOCAL), lambda b, r: (layer, 0, 0, 0, 0))],
        out_specs=pl.BlockSpec((GRID_W, w), lambda b, r: (b * NA_ROWS + r, 0)),
        out_shape=jax.ShapeDtypeStruct((T_LAT, NA_WIDTH), BF16),
        compiler_params=_params(("arbitrary", "arbitrary")),
        name="na_latent",
    )(q, k, v, k_cache, v_cache, bias_tab)


def _conv_kernel(gb_ref, gc_ref, u_ref, w_ref, o_ref):
    gu = gc_ref[...] * u_ref[...]
    s = gu.shape[0]
    row = lax.broadcasted_iota(jnp.int32, gu.shape, 0)
    prev = jnp.where(row == 0, 0.0, pltpu.roll(gu, 1, 0))
    nxt = jnp.where(row == s - 1, 0.0, pltpu.roll(gu, s - 1, 0))
    y = prev * w_ref[0:1, :] + gu * w_ref[1:2, :] + nxt * w_ref[2:3, :]
    o_ref[...] = (gb_ref[...] * y).astype(BF16)


def short_conv(z, conv_w8, layer, *, n_seq, seq, row_off):
    off = row_off // seq
    zb = lambda k: pl.BlockSpec((seq, CONV_CH), lambda i: (off + i, k))
    return pl.pallas_call(
        _conv_kernel,
        grid=(n_seq,),
        in_specs=[zb(4), zb(5), zb(6), pl.BlockSpec((None, 8, CONV_CH), lambda i: (layer, 0, 0))],
        out_specs=pl.BlockSpec((seq, CONV_CH), lambda i: (i, 0)),
        out_shape=jax.ShapeDtypeStruct((n_seq * seq, CONV_CH), BF16),
        compiler_params=_params(("arbitrary",)),
        name="short_conv",
    )(z, z, z, conv_w8)


def _out_matmul_kernel(x_ref, g_ref, a0c_ref, a0l_ref, a1c_ref, a1l_ref, a2c_ref, a2l_ref,
                       w0_ref, w1_ref, w2_ref, o_ref, *, tm):
    row = pl.program_id(0) * tm
    grp = _group_of_row(row)
    is_ctx = row < T_CTX
    acc = jnp.dot(jnp.where(is_ctx, a0c_ref[...], a0l_ref[...]), w0_ref[...], preferred_element_type=F32)
    acc = acc + jnp.dot(jnp.where(is_ctx, a1c_ref[...], a1l_ref[...]), w1_ref[...], preferred_element_type=F32)
    acc = acc + jnp.dot(jnp.where(is_ctx, a2c_ref[...], a2l_ref[...]), w2_ref[...], preferred_element_type=F32)
    o_ref[...] = x_ref[...] + g_ref[pl.ds(grp, 1), :] * acc


def out_matmul(x, mods, o_mla, o_na, conv, w_out, layer, gate_chunk):
    t = x.shape[0]
    tm, tn = ROW_TILE, 1024
    nj = D_MODEL // tn
    w_mla = MLA_HEADS * MLA_NOPE
    n_ctx = T_CTX // tm
    ctx = lambda w: pl.BlockSpec((tm, w), lambda i, j: (jnp.minimum(i, n_ctx - 1), 0))
    lat = lambda w: pl.BlockSpec((tm, w), lambda i, j: (jnp.maximum(i - n_ctx, 0), 0))
    return pl.pallas_call(
        functools.partial(_out_matmul_kernel, tm=tm),
        grid=(t // tm, nj),
        in_specs=[pl.BlockSpec((tm, tn), lambda i, j: (i, j)),
                  pl.BlockSpec((None, 8, tn), lambda i, j: (layer, 0, gate_chunk * nj + j)),
                  ctx(w_mla), lat(w_mla), ctx(NA_WIDTH), lat(NA_WIDTH), ctx(CONV_CH), lat(CONV_CH),
                  pl.BlockSpec((None, w_mla, tn), lambda i, j: (layer, 0, j)),
                  pl.BlockSpec((None, NA_WIDTH, tn), lambda i, j: (layer, w_mla // NA_WIDTH, j)),
                  pl.BlockSpec((None, CONV_CH, tn), lambda i, j: (layer, (w_mla + NA_WIDTH) // CONV_CH, j))],
        out_specs=pl.BlockSpec((tm, tn), lambda i, j: (i, j)),
        out_shape=jax.ShapeDtypeStruct((t, D_MODEL), F32),
        compiler_params=_params(("arbitrary", "arbitrary")),
        name="out_matmul",
    )(x, mods, *o_mla, *o_na, *conv, w_out, w_out, w_out)


def _argmax_step(s, pos, big):
    m = jnp.max(s, axis=0, keepdims=True)
    first = jnp.min(jnp.where(s == m, pos, big), axis=0, keepdims=True)
    return m, pos == first


def _half_ranks(s1, s2):
    n, width = s1.shape
    pos = lax.broadcasted_iota(jnp.int32, (n, width), 0).astype(F32)
    iota16 = lax.broadcasted_iota(jnp.int32, (PEER_TOPK, width), 0)

    def body(j, carry):
        out = []
        for s, vals, idxs in (carry[:3], carry[3:]):
            m = jnp.max(s, axis=0, keepdims=True)
            first = jnp.min(jnp.where(s == m, pos, float(n)), axis=0, keepdims=True)
            out += [jnp.where(pos == first, -jnp.inf, s), jnp.where(iota16 == j, m, vals),
                    jnp.where(iota16 == j, first, idxs)]
        return tuple(out)

    zero16 = jnp.zeros((PEER_TOPK, width), F32)
    _, v1, i1, _, v2, i2 = lax.fori_loop(0, PEER_TOPK, body, (s1, zero16, zero16, s2, zero16, zero16))
    return pos, v1, i1, v2, i2


_CAND_ROWS = 16 + 7 * 8 + 8


def _cand_positions():
    p = [j2 for j2 in range(16)]
    p += [j1 * 16 + j2 for j1 in range(1, 8) for j2 in range(8)]
    p += [j1 * 16 for j1 in range(8, 16)]
    return np.tile(np.asarray(p, np.float32)[:, None], (1, LANES))


def _staircase(v1, v2, cpos):
    cand = jnp.concatenate([v1[0:1] + v2] + [v1[j:j + 1] + v2[0:8] for j in range(1, 8)] + [v1[8:16] + v2[0:1]],
                           axis=0)
    cmax = cand[0:1]

    def body(j, carry):
        s, sel = carry
        _, hit = _argmax_step(s, cpos, float(PEER_TOPK * PEER_TOPK))
        return jnp.where(hit, -jnp.inf, s), jnp.where(hit, 1.0, sel)

    _, sel = lax.fori_loop(0, PEER_TOPK, body, (cand, jnp.zeros_like(cand)))
    z = jnp.sum(sel * jnp.exp(cand - cmax), axis=0, keepdims=True)
    iota8 = lax.broadcasted_iota(jnp.int32, (8, cand.shape[1]), 0)
    low = jnp.zeros((8, cand.shape[1]), F32)
    low = jnp.where(iota8 == 0, jnp.sum(sel[0:16], axis=0, keepdims=True), low)
    for j in range(1, 8):
        low = jnp.where(iota8 == j, jnp.sum(sel[8 + 8 * j:16 + 8 * j], axis=0, keepdims=True), low)
    return jnp.concatenate([low, sel[72:80]], axis=0), z


def _peer_topk_kernel(q_ref, keys_ref, cpos_ref, e1_ref, lb1_ref, r2_ref, e2_ref, *, heads):
    nt = (((1,), (1,)), ((), ()))
    for h in range(heads):
        q = q_ref[:, h * 2 * LANES:(h + 1) * 2 * LANES].astype(BF16)
        s1 = lax.dot_general(keys_ref[h, 0], q[:, :LANES], nt, preferred_element_type=F32)
        s2 = lax.dot_general(keys_ref[h, 1], q[:, LANES:], nt, preferred_element_type=F32)
        pos, v1, i1, v2, i2 = _half_ranks(s1, s2)
        counts, z = _staircase(v1, v2, cpos_ref[...])
        lb1 = jnp.zeros_like(s1)
        r2 = jnp.full_like(s2, float(PEER_TOPK))
        for j in range(PEER_TOPK):
            lb1 = jnp.where(pos == i1[j:j + 1], counts[j:j + 1], lb1)
            r2 = jnp.where(pos == i2[j:j + 1], float(j), r2)
        e1_ref[h] = jnp.exp(s1 - v1[0:1])
        lb1_ref[h] = lb1
        r2_ref[h] = r2
        e2_ref[h] = jnp.exp(s2 - v2[0:1]) / z


def peer_topk(q, sub_keys, layer):
    t = q.shape[0]
    heads = 4
    ob = pl.BlockSpec((heads, PEER_N_KEYS, LANES), lambda i, h: (h, 0, i))
    shp = jax.ShapeDtypeStruct((PEER_HEADS, PEER_N_KEYS, t), F32)
    return pl.pallas_call(
        functools.partial(_peer_topk_kernel, heads=heads),
        grid=(t // LANES, PEER_HEADS // heads),
        in_specs=[pl.BlockSpec((LANES, heads * 2 * LANES), lambda i, h: (i, h)),
                  pl.BlockSpec((None, heads, 2, PEER_N_KEYS, LANES), lambda i, h: (layer, h, 0, 0, 0)),
                  pl.BlockSpec((_CAND_ROWS, LANES), lambda i, h: (0, 0))],
        out_specs=[ob, ob, ob, ob],
        out_shape=[shp, shp, shp, shp],
        compiler_params=_params(("arbitrary", "arbitrary")),
        name="peer_topk",
    )(q, sub_keys, jnp.asarray(_cand_positions()))


_SQRT_HALF = float(np.sqrt(0.5))


def _peer_dense_kernel(ht_ref, u_ref, v_ref, e1_ref, lb1_ref, r2_ref, e2_ref, x_ref, g_ref, o_ref,
                       acc_ref, at_ref, wg_ref, *, tt, ec):
    i = pl.program_id(0)
    c = pl.program_id(1)

    @pl.when(c == 0)
    def _():
        acc_ref[...] = jnp.zeros_like(acc_ref)

    at_ref[...] = jnp.dot(u_ref[...], ht_ref[...], preferred_element_type=F32)
    n_a = ec // PEER_N_KEYS
    grp0 = pl.multiple_of((c * n_a) // 8 * 8, 8)
    off = (c * n_a) % 8

    def key_row(ref, h, al, lanes):
        blk = ref[h, pl.ds(grp0, 8), lanes]
        row = blk[al:al + 1]
        for o in range(n_a, 8, n_a):
            row = jnp.where(off == o, blk[o + al:o + al + 1], row)
        return row

    for al in range(n_a):
        rows = slice(al * PEER_N_KEYS, (al + 1) * PEER_N_KEYS)
        for lt in range(tt // LANES):
            lanes = slice(lt * LANES, (lt + 1) * LANES)
            gate = jnp.zeros((PEER_N_KEYS, LANES), F32)
            for h in range(PEER_HEADS):
                lb = key_row(lb1_ref, h, al, lanes)
                e1 = key_row(e1_ref, h, al, lanes)
                gate = gate + jnp.where(r2_ref[h, :, lanes] < lb, e2_ref[h, :, lanes], 0.0) * e1
            act = at_ref[rows, lanes]
            wg_ref[rows, lanes] = 0.5 * act * (1.0 + lax.erf(act * _SQRT_HALF)) * gate
    acc_ref[...] += jnp.dot(wg_ref[...].T.astype(BF16), v_ref[...], preferred_element_type=F32)

    @pl.when(c == pl.num_programs(1) - 1)
    def _():
        grp = _group_of_row(i * tt)
        o_ref[...] = x_ref[...] + g_ref[pl.ds(grp, 1), :] * acc_ref[...]


def peer_dense(ht, peer_u, peer_v, e1, lb1, r2, e2, x, mods, layer, gate_chunk):
    t = x.shape[0]
    tt, ec = PEER_TOK_TILE, PEER_EXP_TILE
    kb = pl.BlockSpec((PEER_HEADS, PEER_N_KEYS, tt), lambda i, c: (0, 0, i))
    return pl.pallas_call(
        functools.partial(_peer_dense_kernel, tt=tt, ec=ec),
        grid=(t // tt, PEER_N_EXPERTS // ec),
        in_specs=[pl.BlockSpec((D_MODEL, tt), lambda i, c: (0, i)),
                  pl.BlockSpec((None, ec, D_MODEL), lambda i, c: (layer, c, 0)),
                  pl.BlockSpec((None, ec, D_MODEL), lambda i, c: (layer, c, 0)),
                  kb, kb, kb, kb,
                  pl.BlockSpec((tt, D_MODEL), lambda i, c: (i, 0)),
                  pl.BlockSpec((None, 8, D_MODEL), lambda i, c: (layer, 0, gate_chunk))],
        out_specs=pl.BlockSpec((tt, D_MODEL), lambda i, c: (i, 0)),
        out_shape=jax.ShapeDtypeStruct((t, D_MODEL), F32),
        scratch_shapes=[pltpu.VMEM((tt, D_MODEL), F32),
                        pltpu.VMEM((ec, tt), F32),
                        pltpu.VMEM((ec, tt), F32)],
        compiler_params=_params(("arbitrary", "arbitrary")),
        name="peer_dense",
    )(ht, peer_u, peer_v, e1, lb1, r2, e2, x, mods)


def _rope_tables():
    t = jnp.arange(LAT_SEQ)
    row = (t // GRID_W).astype(F32)
    col = (t % GRID_W).astype(F32)
    n_freq = MLA_ROPE // 4
    inv = ROPE_THETA ** (-jnp.arange(n_freq, dtype=F32) / n_freq)
    ang = jnp.concatenate([row[:, None] * inv, col[:, None] * inv], axis=-1)
    cos, sin = jnp.cos(ang), jnp.sin(ang)
    zero = jnp.zeros_like(cos)
    pad = jnp.zeros((LAT_SEQ, LANES - MLA_ROPE), F32)
    c_lat = jnp.concatenate([cos, cos, pad], axis=-1)
    s1_lat = jnp.concatenate([-sin, zero, pad], axis=-1)
    s2_lat = jnp.concatenate([zero, sin, pad], axis=-1)
    ones = jnp.concatenate([jnp.ones((1, MLA_ROPE), F32), jnp.zeros((1, LANES - MLA_ROPE), F32)], axis=-1)

    def full(lat, ident):
        return jnp.concatenate([jnp.broadcast_to(ident, (T_CTX, LANES)), jnp.tile(lat, (N_LAT_SEQ, 1)),
                                jnp.broadcast_to(ident, (T_CACHE, LANES))], axis=0)

    zeros = jnp.zeros((1, LANES), F32)
    return full(c_lat, ones), full(s1_lat, zeros), full(s2_lat, zeros)


def _na_bias_table(rel_bias):
    cq = np.arange(GRID_W)
    col_start = np.clip(cq - NA_KC // 2, 0, GRID_W - NA_KC)
    valid = (cq[None, :] >= col_start[:, None]) & (cq[None, :] < col_start[:, None] + NA_KC)
    coff = np.clip(cq[None, :] - cq[:, None], -(NA_KC - 1), NA_KC - 1) + (NA_KC - 1)
    onehot = (coff[:, :, None] == np.arange(2 * NA_KC - 1)[None, None, :]).astype(np.float32)
    toep = jnp.einsum('lhrc,qkc->lhrqk', rel_bias, jnp.asarray(onehot), precision=lax.Precision.HIGHEST)
    toep = jnp.where(jnp.asarray(valid)[None, None, None], toep, -jnp.inf)
    tabs = []
    for d in range(NA_KR):
        rows = toep[:, :, NA_KR - 1 - d:2 * NA_KR - 1 - d]
        tabs.append(rows.transpose(0, 1, 3, 2, 4).reshape(DEPTH, NA_HEADS, GRID_W, NA_LOCAL))
    return jnp.stack(tabs, axis=2)


def kernel(x_prompt, x_sample, cache_mla_ckv, cache_mla_kpe, cache_na_k, cache_na_v, c, c_ctx, ada_w, ada_b, norm_mix_g, norm_ffn_g, w_in, mla_q_norm_g, mla_w_uq, mla_kv_norm_g, mla_w_ukv, mla_q_head_g, mla_k_head_g, na_q_head_g, na_k_head_g, na_rel_bias, conv_w, w_out, peer_w_q, peer_sub_keys, peer_u, peer_v):
    w_in_p = jnp.concatenate([w_in[..., :512], w_in[..., 832:3904], w_in[..., 512:832],
                              jnp.zeros((DEPTH, D_MODEL, D_IN_PAD - 3904), F32)], axis=-1).astype(BF16)
    wuq_p = jnp.pad(mla_w_uq.reshape(DEPTH, MLA_Q_LORA, MLA_HEADS, MLA_QK_DIM),
                    ((0, 0), (0, 0), (0, 0), (0, MLA_HEAD_PAD - MLA_QK_DIM))
                    ).reshape(DEPTH, MLA_Q_LORA, MLA_HEADS * MLA_HEAD_PAD).astype(BF16)
    qhg_p = jnp.pad(mla_q_head_g, ((0, 0), (0, MLA_HEAD_PAD - MLA_QK_DIM))).reshape(DEPTH, 1, MLA_HEAD_PAD)
    wukv4 = mla_w_ukv.reshape(DEPTH, MLA_KV_LORA, MLA_HEADS, 2 * MLA_NOPE)
    wukv_p = jnp.concatenate([wukv4[..., :MLA_NOPE].reshape(DEPTH, MLA_KV_LORA, -1),
                              wukv4[..., MLA_NOPE:].reshape(DEPTH, MLA_KV_LORA, -1)], axis=-1).astype(BF16)
    khg_n = mla_k_head_g[:, :MLA_NOPE].reshape(DEPTH, 1, LANES)
    khg_p = jnp.pad(mla_k_head_g[:, MLA_NOPE:], ((0, 0), (0, LANES - MLA_ROPE))).reshape(DEPTH, 1, LANES)
    conv_w8 = jnp.pad(conv_w.transpose(0, 2, 1), ((0, 0), (0, 5), (0, 0)))
    w_out_b = w_out.astype(BF16)
    peer_wq_b = peer_w_q.astype(BF16)
    sub_keys_b = peer_sub_keys.astype(BF16)
    peer_u_b = peer_u.astype(BF16)
    peer_v_b = peer_v.astype(BF16)
    g_mix = norm_mix_g.reshape(DEPTH, 1, D_MODEL)
    g_ffn = norm_ffn_g.reshape(DEPTH, 1, D_MODEL)
    qng = mla_q_norm_g.reshape(DEPTH, 1, MLA_Q_LORA)
    kvng = mla_kv_norm_g.reshape(DEPTH, 1, MLA_KV_LORA)
    naqg = na_q_head_g.reshape(DEPTH, 1, NA_HEAD_DIM)
    nakg = na_k_head_g.reshape(DEPTH, 1, NA_HEAD_DIM)
    rope_c, rope_s1, rope_s2 = _rope_tables()
    na_bias_tab = _na_bias_table(na_rel_bias)
    cache_kpe_p = jnp.pad(cache_mla_kpe, ((0, 0), (0, 0), (0, 0), (0, LANES - MLA_ROPE)))
    cache_nak = cache_na_k.transpose(1, 0, 3, 2, 4).reshape(DEPTH, N_LAT_SEQ, PAST_LEN, NA_WIDTH).astype(BF16)
    cache_nav = cache_na_v.transpose(1, 0, 3, 2, 4).reshape(DEPTH, N_LAT_SEQ, PAST_LEN, NA_WIDTH).astype(BF16)

    cpad = jnp.concatenate([c_ctx[None, :], c, jnp.zeros((8 - 1 - N_LAT_SEQ, D_MODEL), F32)], axis=0)
    mods = ada_all(cpad, ada_w, ada_b)

    x = jnp.concatenate([x_prompt.reshape(T_CTX, D_MODEL), x_sample.reshape(T_LAT, D_MODEL)], axis=0)
    ckv_out, kpe_out, nak_out, nav_out = [], [], [], []
    mla_scale = MLA_QK_DIM ** -0.5
    na_scale = NA_HEAD_DIM ** -0.5

    ident_c = jnp.broadcast_to(rope_c[:1], (SEQ_TILE, LANES))
    ident_s = jnp.zeros((SEQ_TILE, LANES), F32)
    k_cache, v_cache = kv_prep_cache(
        cache_mla_ckv.transpose(1, 0, 2, 3).reshape(DEPTH, T_CACHE, MLA_KV_LORA),
        cache_kpe_p.transpose(1, 0, 2, 3).reshape(DEPTH, T_CACHE, LANES), ident_c, ident_s, wukv_p, khg_n, khg_p)

    for l in range(DEPTH):
        (z,) = mod_matmul(x, g_mix, mods, w_in_p, l, 0, 1, tn=1024, emit_ht=False)
        q_mla, ckv_n, q_na, k_na_f, k_na_b, v_na_b = proj_prep(
            z, rope_c, rope_s1, rope_s2, qng, wuq_p, qhg_p, kvng, naqg, nakg, l)
        k_mla, v_mla = kv_prep(ckv_n, z, rope_c, rope_s1, rope_s2, wukv_p, khg_n, khg_p, l)

        o_mla_ctx = attention(q_mla, k_mla, v_mla, n_seq=N_CTX_SEQ, n_heads=MLA_HEADS, dq=MLA_HEAD_PAD,
                              sq_total=CTX_SEQ, sk=CTX_SEQ, q_row_off=0, k_row_off=0, scale=mla_scale)
        o_mla_lat = latent_mla_attention(q_mla, k_mla, v_mla, k_cache, v_cache, l, scale=mla_scale)
        o_na_ctx = attention(q_na, k_na_b, v_na_b, n_seq=N_CTX_SEQ, n_heads=NA_HEADS, dq=NA_HEAD_DIM,
                             sq_total=CTX_SEQ, sk=CTX_SEQ, q_row_off=0, k_row_off=0, scale=na_scale)
        o_na_lat = na_latent_attention(q_na, k_na_b, v_na_b, cache_nak, cache_nav, na_bias_tab, l, scale=na_scale)
        conv_ctx = short_conv(z, conv_w8, l, n_seq=N_CTX_SEQ, seq=CTX_SEQ, row_off=0)
        conv_lat = short_conv(z, conv_w8, l, n_seq=N_LAT_SEQ, seq=LAT_SEQ, row_off=T_CTX)

        x = out_matmul(x, mods, (o_mla_ctx, o_mla_lat), (o_na_ctx, o_na_lat), (conv_ctx, conv_lat), w_out_b, l, 2)

        q_peer, ht = mod_matmul(x, g_ffn, mods, peer_wq_b, l, 3, 4, tn=1024, emit_ht=True)
        e1, lb1, r2, e2 = peer_topk(q_peer, sub_keys_b, l)
        x = peer_dense(ht, peer_u_b, peer_v_b, e1, lb1, r2, e2, x, mods, l, 5)

        ckv_out.append(ckv_n[:T_CTX].reshape(N_CTX_SEQ, CTX_SEQ, MLA_KV_LORA))
        kpe_out.append(z[:T_CTX, Z_KPE_COL:Z_KPE_COL + MLA_ROPE].reshape(N_CTX_SEQ, CTX_SEQ, MLA_ROPE))
        nak_out.append(k_na_f[:T_CTX].reshape(N_CTX_SEQ, CTX_SEQ, NA_HEADS, NA_HEAD_DIM).transpose(0, 2, 1, 3))
        nav_out.append(z[:T_CTX, 1536:2048].reshape(N_CTX_SEQ, CTX_SEQ, NA_HEADS, NA_HEAD_DIM
                                                    ).transpose(0, 2, 1, 3))

    y_prompt = x[:T_CTX].reshape(N_CTX_SEQ, CTX_SEQ, D_MODEL)
    y_sample = x[T_CTX:].reshape(N_LAT_SEQ, LAT_SEQ, D_MODEL)
    return (y_prompt, y_sample, jnp.stack(ckv_out, axis=1), jnp.stack(kpe_out, axis=1),
            jnp.stack(nak_out, axis=1), jnp.stack(nav_out, axis=1))
```

```python
import functools

import numpy as np
import jax
import jax.numpy as jnp
from jax import lax
from jax.experimental import pallas as pl
from jax.experimental.pallas import tpu as pltpu

F32 = jnp.float32
BF16 = jnp.bfloat16

D_MODEL = 2048
DEPTH = 4
N_CTX_SEQ = 32
CTX_SEQ = 256
N_LAT_SEQ = 2
LAT_SEQ = 1024
PAST_LEN = 512
T_CTX = N_CTX_SEQ * CTX_SEQ
T_LAT = N_LAT_SEQ * LAT_SEQ
T_ALL = T_CTX + T_LAT
T_CACHE = N_LAT_SEQ * PAST_LEN
GRID_W = 64
NORM_EPS = 1e-6
ROPE_THETA = 10000.0
ADA_CHUNKS = 6

MLA_HEADS = 8
MLA_NOPE = 128
MLA_ROPE = 64
MLA_QK_DIM = MLA_NOPE + MLA_ROPE
MLA_Q_LORA = 512
MLA_KV_LORA = 256
MLA_HEAD_PAD = 256
NA_HEADS = 4
NA_HEAD_DIM = 128
NA_WIDTH = NA_HEADS * NA_HEAD_DIM
NA_KR = 8
NA_KC = 16
CONV_CH = 512
D_IN_PAD = 4096
Z_CKV_BLOCK = 7
Z_KPE_COL = 3840

PEER_HEADS = 8
PEER_N_KEYS = 128
PEER_N_EXPERTS = PEER_N_KEYS * PEER_N_KEYS
PEER_TOPK = 16

LANES = 128
VMEM_LIMIT = 56 * 1024 * 1024

ROW_TILE = 1024
SEQ_TILE = 256
PEER_TOK_TILE = 512
PEER_EXP_TILE = 1024


def _params(sem, flags=None):
    return pltpu.CompilerParams(dimension_semantics=sem, vmem_limit_bytes=VMEM_LIMIT, flags=flags)


def _group_of_row(row):
    return jnp.where(row < T_CTX, 0, 1 + (row - T_CTX) // LAT_SEQ)


def _rms(x, g, n):
    ms = jnp.sum(x * x, axis=-1, keepdims=True) / n
    return x * lax.rsqrt(ms + NORM_EPS) * g


def _ada_kernel(c_ref, w_ref, b_ref, o_ref):
    c = c_ref[...]
    s = c * jax.nn.sigmoid(c)
    o_ref[...] = jnp.dot(s.astype(BF16), w_ref[...].astype(BF16), preferred_element_type=F32) + b_ref[...]


def ada_all(cpad, ada_w, ada_b):
    tn = 1536
    n = ADA_CHUNKS * D_MODEL
    return pl.pallas_call(
        _ada_kernel,
        grid=(DEPTH, n // tn),
        in_specs=[pl.BlockSpec((8, D_MODEL), lambda l, j: (0, 0)),
                  pl.BlockSpec((None, D_MODEL, tn), lambda l, j: (l, 0, j)),
                  pl.BlockSpec((None, 1, tn), lambda l, j: (l, 0, j))],
        out_specs=pl.BlockSpec((None, 8, tn), lambda l, j: (l, 0, j)),
        out_shape=jax.ShapeDtypeStruct((DEPTH, 8, n), F32),
        compiler_params=_params(("arbitrary", "arbitrary")),
        name="ada",
    )(cpad, ada_w, ada_b.reshape(DEPTH, 1, n))


def _mod_matmul_kernel(x_ref, g_ref, sh_ref, sc_ref, w_ref, z_ref, *rest, tm, emit_ht):
    if emit_ht:
        ht_ref, h_scr = rest
    else:
        (h_scr,) = rest
    i = pl.program_id(0)

    @pl.when(pl.program_id(1) == 0)
    def _():
        grp = _group_of_row(i * tm)
        y = _rms(x_ref[...], g_ref[...], D_MODEL)
        h = y * (1.0 + sc_ref[pl.ds(grp, 1), :]) + sh_ref[pl.ds(grp, 1), :]
        h_scr[...] = h.astype(BF16)
        if emit_ht:
            ht_ref[...] = h.T.astype(BF16)

    z_ref[...] = jnp.dot(h_scr[...], w_ref[...], preferred_element_type=F32)


def mod_matmul(x, gain, mods, w, layer, shift_chunk, scale_chunk, *, tn, emit_ht):
    t = x.shape[0]
    n = w.shape[-1]
    tm = ROW_TILE
    out_shape = [jax.ShapeDtypeStruct((t, n), F32)]
    out_specs = [pl.BlockSpec((tm, tn), lambda i, j: (i, j))]
    if emit_ht:
        out_shape.append(jax.ShapeDtypeStruct((D_MODEL, t), BF16))
        out_specs.append(pl.BlockSpec((D_MODEL, tm), lambda i, j: (0, i)))
    res = pl.pallas_call(
        functools.partial(_mod_matmul_kernel, tm=tm, emit_ht=emit_ht),
        grid=(t // tm, n // tn),
        in_specs=[pl.BlockSpec((tm, D_MODEL), lambda i, j: (i, 0)),
                  pl.BlockSpec((None, 1, D_MODEL), lambda i, j: (layer, 0, 0)),
                  pl.BlockSpec((None, 8, D_MODEL), lambda i, j: (layer, 0, shift_chunk)),
                  pl.BlockSpec((None, 8, D_MODEL), lambda i, j: (layer, 0, scale_chunk)),
                  pl.BlockSpec((None, D_MODEL, tn), lambda i, j: (layer, 0, j))],
        out_specs=out_specs,
        out_shape=out_shape,
        scratch_shapes=[pltpu.VMEM((tm, D_MODEL), BF16)],
        compiler_params=_params(("arbitrary", "arbitrary")),
        name="mod_matmul_ht" if emit_ht else "mod_matmul",
    )(x, gain, mods, mods, w)
    return res


def _rope128(x, c, s1, s2):
    return x * c + pltpu.roll(x, 96, 1) * s1 + pltpu.roll(x, 32, 1) * s2


def _proj_prep_kernel(cq_ref, naq_ref, nak_ref, nav_ref, ckv_ref, rc_ref, rs1_ref, rs2_ref,
                      qng_ref, wuq_ref, qhg_ref, kvng_ref, naqg_ref, nakg_ref,
                      qmla_ref, ckvn_ref, qna_ref, knaf_ref, knab_ref, vnab_ref):
    cqn = _rms(cq_ref[...], qng_ref[...], MLA_Q_LORA)
    q = jnp.dot(cqn.astype(BF16), wuq_ref[...], preferred_element_type=F32)
    rc, rs1, rs2 = rc_ref[...], rs1_ref[...], rs2_ref[...]
    for h in range(MLA_HEADS):
        qh = _rms(q[:, h * MLA_HEAD_PAD:(h + 1) * MLA_HEAD_PAD], qhg_ref[...], MLA_QK_DIM)
        qmla_ref[:, h * MLA_HEAD_PAD:h * MLA_HEAD_PAD + LANES] = qh[:, :LANES].astype(BF16)
        qmla_ref[:, h * MLA_HEAD_PAD + LANES:(h + 1) * MLA_HEAD_PAD] = (
            _rope128(qh[:, LANES:], rc, rs1, rs2).astype(BF16))
    ckvn_ref[...] = _rms(ckv_ref[:, :MLA_KV_LORA], kvng_ref[...], MLA_KV_LORA)
    for h in range(NA_HEADS):
        sl = slice(h * NA_HEAD_DIM, (h + 1) * NA_HEAD_DIM)
        qna_ref[:, sl] = _rms(naq_ref[:, sl], naqg_ref[...], NA_HEAD_DIM).astype(BF16)
        kn = _rms(nak_ref[:, sl], nakg_ref[...], NA_HEAD_DIM)
        knaf_ref[:, sl] = kn
        knab_ref[:, sl] = kn.astype(BF16)
    vnab_ref[...] = nav_ref[...].astype(BF16)


def proj_prep(z, rope_c, rope_s1, rope_s2, qng, wuq, qhg, kvng, naqg, nakg, layer):
    t = z.shape[0]
    tm = SEQ_TILE
    zb = lambda k: pl.BlockSpec((tm, 512), lambda i: (i, k))
    rb = pl.BlockSpec((tm, LANES), lambda i: (i, 0))
    wl = lambda *shape: pl.BlockSpec((None,) + shape, lambda i: (layer,) + (0,) * len(shape))
    ob = lambda w: pl.BlockSpec((tm, w), lambda i: (i, 0))
    return pl.pallas_call(
        _proj_prep_kernel,
        grid=(t // tm,),
        in_specs=[zb(0), zb(1), zb(2), zb(3), zb(Z_CKV_BLOCK), rb, rb, rb,
                  wl(1, MLA_Q_LORA), wl(MLA_Q_LORA, MLA_HEADS * MLA_HEAD_PAD), wl(1, MLA_HEAD_PAD),
                  wl(1, MLA_KV_LORA), wl(1, NA_HEAD_DIM), wl(1, NA_HEAD_DIM)],
        out_specs=[ob(MLA_HEADS * MLA_HEAD_PAD), ob(MLA_KV_LORA), ob(NA_WIDTH), ob(NA_WIDTH), ob(NA_WIDTH),
                   ob(NA_WIDTH)],
        out_shape=[jax.ShapeDtypeStruct((t, MLA_HEADS * MLA_HEAD_PAD), BF16),
                   jax.ShapeDtypeStruct((t, MLA_KV_LORA), F32),
                   jax.ShapeDtypeStruct((t, NA_WIDTH), BF16),
                   jax.ShapeDtypeStruct((t, NA_WIDTH), F32),
                   jax.ShapeDtypeStruct((t, NA_WIDTH), BF16),
                   jax.ShapeDtypeStruct((t, NA_WIDTH), BF16)],
        compiler_params=_params(("arbitrary",)),
        name="proj_prep",
    )(z, z, z, z, z, rope_c, rope_s1, rope_s2, qng, wuq, qhg, kvng, naqg, nakg)


def _kv_prep_kernel(ckv_ref, kpe_ref, rc_ref, rs1_ref, rs2_ref, wukv_ref, gn_ref, gp_ref, k_ref, v_ref):
    kv = jnp.dot(ckv_ref[...].astype(BF16), wukv_ref[...], preferred_element_type=F32)
    kpe = kpe_ref[...]
    pe_ss = jnp.sum(kpe * kpe, axis=-1, keepdims=True)
    rc, rs1, rs2 = rc_ref[...], rs1_ref[...], rs2_ref[...]
    for h in range(MLA_HEADS):
        kn = kv[:, h * MLA_NOPE:(h + 1) * MLA_NOPE]
        ms = (jnp.sum(kn * kn, axis=-1, keepdims=True) + pe_ss) / MLA_QK_DIM
        r = lax.rsqrt(ms + NORM_EPS)
        k_ref[:, h * MLA_HEAD_PAD:h * MLA_HEAD_PAD + LANES] = (kn * r * gn_ref[...]).astype(BF16)
        k_ref[:, h * MLA_HEAD_PAD + LANES:(h + 1) * MLA_HEAD_PAD] = (
            _rope128(kpe * r * gp_ref[...], rc, rs1, rs2).astype(BF16))
    v_ref[...] = kv[:, MLA_HEADS * MLA_NOPE:].astype(BF16)


def kv_prep(ckv_n, z, rope_c, rope_s1, rope_s2, wukv, gn, gp, layer):
    t = ckv_n.shape[0]
    tm = SEQ_TILE
    rb = lambda w: pl.BlockSpec((tm, w), lambda i: (i, 0))
    wl = lambda *shape: pl.BlockSpec((None,) + shape, lambda i: (layer,) + (0,) * len(shape))
    return pl.pallas_call(
        _kv_prep_kernel,
        grid=(t // tm,),
        in_specs=[rb(MLA_KV_LORA), pl.BlockSpec((tm, LANES), lambda i: (i, Z_KPE_COL // LANES)),
                  rb(LANES), rb(LANES), rb(LANES),
                  wl(MLA_KV_LORA, 2 * MLA_HEADS * MLA_NOPE), wl(1, LANES), wl(1, LANES)],
        out_specs=[rb(MLA_HEADS * MLA_HEAD_PAD), rb(MLA_HEADS * MLA_NOPE)],
        out_shape=[jax.ShapeDtypeStruct((t, MLA_HEADS * MLA_HEAD_PAD), BF16),
                   jax.ShapeDtypeStruct((t, MLA_HEADS * MLA_NOPE), BF16)],
        compiler_params=_params(("arbitrary",)),
        name="kv_prep",
    )(ckv_n, z, rope_c, rope_s1, rope_s2, wukv, gn, gp)


def kv_prep_cache(ckv, kpe, ident_c, ident_s, wukv, gn, gp):
    tm = SEQ_TILE
    rb = lambda w: pl.BlockSpec((None, tm, w), lambda l, i: (l, i, 0))
    tb = pl.BlockSpec((tm, LANES), lambda l, i: (0, 0))
    wl = lambda *shape: pl.BlockSpec((None,) + shape, lambda l, i: (l,) + (0,) * len(shape))
    return pl.pallas_call(
        _kv_prep_kernel,
        grid=(DEPTH, T_CACHE // tm),
        in_specs=[rb(MLA_KV_LORA), rb(LANES), tb, tb, tb,
                  wl(MLA_KV_LORA, 2 * MLA_HEADS * MLA_NOPE), wl(1, LANES), wl(1, LANES)],
        out_specs=[rb(MLA_HEADS * MLA_HEAD_PAD), rb(MLA_HEADS * MLA_NOPE)],
        out_shape=[jax.ShapeDtypeStruct((DEPTH, T_CACHE, MLA_HEADS * MLA_HEAD_PAD), BF16),
                   jax.ShapeDtypeStruct((DEPTH, T_CACHE, MLA_HEADS * MLA_NOPE), BF16)],
        compiler_params=_params(("arbitrary", "arbitrary")),
        name="kv_prep_cache",
    )(ckv, kpe, ident_c, ident_s, ident_s, wukv, gn, gp)


_NT = (((1,), (1,)), ((), ()))


def _attn_kernel(q_ref, k_ref, v_ref, o_ref, *, scale, n_heads, dq):
    dv = NA_HEAD_DIM
    for h in range(n_heads):
        q = q_ref[:, h * dq:(h + 1) * dq]
        k = k_ref[:, h * dq:(h + 1) * dq]
        s = lax.dot_general(q, k, _NT, preferred_element_type=F32) * scale
        m = jnp.max(s, axis=-1, keepdims=True)
        p = jnp.exp(s - m)
        p = p / jnp.sum(p, axis=-1, keepdims=True)
        o = jnp.dot(p.astype(BF16), v_ref[:, h * dv:(h + 1) * dv], preferred_element_type=F32)
        o_ref[:, h * dv:(h + 1) * dv] = o.astype(BF16)


def attention(q, k, v, *, n_seq, n_heads, dq, sq_total, sk, q_row_off, k_row_off, scale):
    sq = 256
    nq = sq_total // sq
    qoff = q_row_off // sq
    koff = k_row_off // sk
    dv = NA_HEAD_DIM
    return pl.pallas_call(
        functools.partial(_attn_kernel, scale=scale, n_heads=n_heads, dq=dq),
        grid=(n_seq, nq),
        in_specs=[pl.BlockSpec((sq, n_heads * dq), lambda b, qi: (qoff + b * nq + qi, 0)),
                  pl.BlockSpec((sk, n_heads * dq), lambda b, qi: (koff + b, 0)),
                  pl.BlockSpec((sk, n_heads * dv), lambda b, qi: (koff + b, 0))],
        out_specs=pl.BlockSpec((sq, n_heads * dv), lambda b, qi: (b * nq + qi, 0)),
        out_shape=jax.ShapeDtypeStruct((n_seq * sq_total, n_heads * dv), BF16),
        compiler_params=_params(("arbitrary", "arbitrary")),
        name="attention",
    )(q, k, v)


def _attn_cached_kernel(q_ref, k_ref, v_ref, kc_ref, vc_ref, o_ref, *, scale, n_heads, dq):
    dv = NA_HEAD_DIM
    for h in range(n_heads):
        qk = slice(h * dq, (h + 1) * dq)
        vv = slice(h * dv, (h + 1) * dv)
        q = q_ref[:, qk]
        s_own = lax.dot_general(q, k_ref[:, qk], _NT, preferred_element_type=F32) * scale
        s_ctx = lax.dot_general(q, kc_ref[:, qk], _NT, preferred_element_type=F32) * scale
        m = jnp.maximum(jnp.max(s_own, axis=-1, keepdims=True), jnp.max(s_ctx, axis=-1, keepdims=True))
        p_own = jnp.exp(s_own - m)
        p_ctx = jnp.exp(s_ctx - m)
        denom = jnp.sum(p_own, axis=-1, keepdims=True) + jnp.sum(p_ctx, axis=-1, keepdims=True)
        o = jnp.dot((p_ctx / denom).astype(BF16), vc_ref[:, vv], preferred_element_type=F32)
        o = o + jnp.dot((p_own / denom).astype(BF16), v_ref[:, vv], preferred_element_type=F32)
        o_ref[:, vv] = o.astype(BF16)


def latent_mla_attention(q, k, v, k_cache, v_cache, layer, *, scale):
    sq = 256
    nq = LAT_SEQ // sq
    qoff = T_CTX // sq
    koff = T_CTX // LAT_SEQ
    wq = MLA_HEADS * MLA_HEAD_PAD
    wv = MLA_HEADS * NA_HEAD_DIM
    return pl.pallas_call(
        functools.partial(_attn_cached_kernel, scale=scale, n_heads=MLA_HEADS, dq=MLA_HEAD_PAD),
        grid=(N_LAT_SEQ, nq),
        in_specs=[pl.BlockSpec((sq, wq), lambda b, qi: (qoff + b * nq + qi, 0)),
                  pl.BlockSpec((LAT_SEQ, wq), lambda b, qi: (koff + b, 0)),
                  pl.BlockSpec((LAT_SEQ, wv), lambda b, qi: (koff + b, 0)),
                  pl.BlockSpec((None, PAST_LEN, wq), lambda b, qi: (layer, b, 0)),
                  pl.BlockSpec((None, PAST_LEN, wv), lambda b, qi: (layer, b, 0))],
        out_specs=pl.BlockSpec((sq, wv), lambda b, qi: (b * nq + qi, 0)),
        out_shape=jax.ShapeDtypeStruct((T_LAT, wv), BF16),
        compiler_params=_params(("arbitrary", "arbitrary")),
        name="latent_mla",
    )(q, k, v, k_cache, v_cache)


NA_ROWS = LAT_SEQ // GRID_W
NA_LOCAL = NA_KR * GRID_W


def _na_lat_kernel(q_ref, k_ref, v_ref, kc_ref, vc_ref, b_ref, o_ref, *, scale):
    rq = pl.program_id(1)
    row_start = jnp.clip(rq - NA_KR // 2, 0, NA_ROWS - NA_KR)
    start = pl.multiple_of(row_start * GRID_W, GRID_W)
    for h in range(NA_HEADS):
        cols = slice(h * NA_HEAD_DIM, (h + 1) * NA_HEAD_DIM)
        q = q_ref[:, cols]
        k_loc = k_ref[pl.ds(start, NA_LOCAL), cols]
        v_loc = v_ref[pl.ds(start, NA_LOCAL), cols]
        s_loc = lax.dot_general(q, k_loc, _NT, preferred_element_type=F32) * scale + b_ref[h, rq - row_start]
        s_ctx = lax.dot_general(q, kc_ref[:, cols], _NT, preferred_element_type=F32) * scale
        m = jnp.maximum(jnp.max(s_loc, axis=-1, keepdims=True), jnp.max(s_ctx, axis=-1, keepdims=True))
        p_loc = jnp.exp(s_loc - m)
        p_ctx = jnp.exp(s_ctx - m)
        denom = jnp.sum(p_loc, axis=-1, keepdims=True) + jnp.sum(p_ctx, axis=-1, keepdims=True)
        o = jnp.dot((p_loc / denom).astype(BF16), v_loc, preferred_element_type=F32)
        o = o + jnp.dot((p_ctx / denom).astype(BF16), vc_ref[:, cols], preferred_element_type=F32)
        o_ref[:, cols] = o.astype(BF16)


def na_latent_attention(q, k, v, k_cache, v_cache, bias_tab, layer, *, scale):
    qoff = T_CTX // GRID_W
    koff = T_CTX // LAT_SEQ
    w = NA_WIDTH
    return pl.pallas_call(
        functools.partial(_na_lat_kernel, scale=scale),
        grid=(N_LAT_SEQ, NA_ROWS),
        in_specs=[pl.BlockSpec((GRID_W, w), lambda b, r: (qoff + b * NA_ROWS + r, 0)),
                  pl.BlockSpec((LAT_SEQ, w), lambda b, r: (koff + b, 0)),
                  pl.BlockSpec((LAT_SEQ, w), lambda b, r: (koff + b, 0)),
                  pl.BlockSpec((None, None, PAST_LEN, w), lambda b, r: (layer, b, 0, 0)),
                  pl.BlockSpec((None, None, PAST_LEN, w), lambda b, r: (layer, b, 0, 0)),
                  pl.BlockSpec((None, NA_HEADS, NA_KR, GRID_W, NA_LOCAL), lambda b, r: (layer, 0, 0, 0, 0))],
        out_specs=pl.BlockSpec((GRID_W, w), lambda b, r: (b * NA_ROWS + r, 0)),
        out_shape=jax.ShapeDtypeStruct((T_LAT, NA_WIDTH), BF16),
        compiler_params=_params(("arbitrary", "arbitrary")),
        name="na_latent",
    )(q, k, v, k_cache, v_cache, bias_tab)


def _conv_kernel(gb_ref, gc_ref, u_ref, w_ref, o_ref):
    gu = gc_ref[...] * u_ref[...]
    s = gu.shape[0]
    row = lax.broadcasted_iota(jnp.int32, gu.shape, 0)
    prev = jnp.where(row == 0, 0.0, pltpu.roll(gu, 1, 0))
    nxt = jnp.where(row == s - 1, 0.0, pltpu.roll(gu, s - 1, 0))
    y = prev * w_ref[0:1, :] + gu * w_ref[1:2, :] + nxt * w_ref[2:3, :]
    o_ref[...] = (gb_ref[...] * y).astype(BF16)


def short_conv(z, conv_w8, layer, *, n_seq, seq, row_off):
    off = row_off // seq
    zb = lambda k: pl.BlockSpec((seq, CONV_CH), lambda i: (off + i, k))
    return pl.pallas_call(
        _conv_kernel,
        grid=(n_seq,),
        in_specs=[zb(4), zb(5), zb(6), pl.BlockSpec((None, 8, CONV_CH), lambda i: (layer, 0, 0))],
        out_specs=pl.BlockSpec((seq, CONV_CH), lambda i: (i, 0)),
        out_shape=jax.ShapeDtypeStruct((n_seq * seq, CONV_CH), BF16),
        compiler_params=_params(("arbitrary",)),
        name="short_conv",
    )(z, z, z, conv_w8)


def _out_matmul_kernel(x_ref, g_ref, a0c_ref, a0l_ref, a1c_ref, a1l_ref, a2c_ref, a2l_ref,
                       w0_ref, w1_ref, w2_ref, o_ref, *, tm):
    row = pl.program_id(0) * tm
    grp = _group_of_row(row)
    is_ctx = row < T_CTX
    acc = jnp.dot(jnp.where(is_ctx, a0c_ref[...], a0l_ref[...]), w0_ref[...], preferred_element_type=F32)
    acc = acc + jnp.dot(jnp.where(is_ctx, a1c_ref[...], a1l_ref[...]), w1_ref[...], preferred_element_type=F32)
    acc = acc + jnp.dot(jnp.where(is_ctx, a2c_ref[...], a2l_ref[...]), w2_ref[...], preferred_element_type=F32)
    o_ref[...] = x_ref[...] + g_ref[pl.ds(grp, 1), :] * acc


def out_matmul(x, mods, o_mla, o_na, conv, w_out, layer, gate_chunk):
    t = x.shape[0]
    tm, tn = ROW_TILE, 1024
    nj = D_MODEL // tn
    w_mla = MLA_HEADS * MLA_NOPE
    n_ctx = T_CTX // tm
    ctx = lambda w: pl.BlockSpec((tm, w), lambda i, j: (jnp.minimum(i, n_ctx - 1), 0))
    lat = lambda w: pl.BlockSpec((tm, w), lambda i, j: (jnp.maximum(i - n_ctx, 0), 0))
    return pl.pallas_call(
        functools.partial(_out_matmul_kernel, tm=tm),
        grid=(t // tm, nj),
        in_specs=[pl.BlockSpec((tm, tn), lambda i, j: (i, j)),
                  pl.BlockSpec((None, 8, tn), lambda i, j: (layer, 0, gate_chunk * nj + j)),
                  ctx(w_mla), lat(w_mla), ctx(NA_WIDTH), lat(NA_WIDTH), ctx(CONV_CH), lat(CONV_CH),
                  pl.BlockSpec((None, w_mla, tn), lambda i, j: (layer, 0, j)),
                  pl.BlockSpec((None, NA_WIDTH, tn), lambda i, j: (layer, w_mla // NA_WIDTH, j)),
                  pl.BlockSpec((None, CONV_CH, tn), lambda i, j: (layer, (w_mla + NA_WIDTH) // CONV_CH, j))],
        out_specs=pl.BlockSpec((tm, tn), lambda i, j: (i, j)),
        out_shape=jax.ShapeDtypeStruct((t, D_MODEL), F32),
        compiler_params=_params(("arbitrary", "arbitrary")),
        name="out_matmul",
    )(x, mods, *o_mla, *o_na, *conv, w_out, w_out, w_out)


def _argmax_step(s, pos, big):
    m = jnp.max(s, axis=0, keepdims=True)
    first = jnp.min(jnp.where(s == m, pos, big), axis=0, keepdims=True)
    return m, pos == first


def _half_ranks(s1, s2):
    n, width = s1.shape
    pos = lax.broadcasted_iota(jnp.int32, (n, width), 0).astype(F32)
    iota16 = lax.broadcasted_iota(jnp.int32, (PEER_TOPK, width), 0)

    def body(j, carry):
        out = []
        for s, vals, idxs in (carry[:3], carry[3:]):
            m = jnp.max(s, axis=0, keepdims=True)
            first = jnp.min(jnp.where(s == m, pos, float(n)), axis=0, keepdims=True)
            out += [jnp.where(pos == first, -jnp.inf, s), jnp.where(iota16 == j, m, vals),
                    jnp.where(iota16 == j, first, idxs)]
        return tuple(out)

    zero16 = jnp.zeros((PEER_TOPK, width), F32)
    _, v1, i1, _, v2, i2 = lax.fori_loop(0, PEER_TOPK, body, (s1, zero16, zero16, s2, zero16, zero16),
                                         unroll=True)
    return pos, v1, i1, v2, i2


_CAND_ROWS = 16 + 7 * 8 + 8


def _cand_positions():
    p = [j2 for j2 in range(16)]
    p += [j1 * 16 + j2 for j1 in range(1, 8) for j2 in range(8)]
    p += [j1 * 16 for j1 in range(8, 16)]
    return np.tile(np.asarray(p, np.float32)[:, None], (1, LANES))


def _staircase(v1, v2, cpos):
    cand = jnp.concatenate([v1[0:1] + v2] + [v1[j:j + 1] + v2[0:8] for j in range(1, 8)] + [v1[8:16] + v2[0:1]],
                           axis=0)
    cmax = cand[0:1]

    def body(j, carry):
        s, sel = carry
        _, hit = _argmax_step(s, cpos, float(PEER_TOPK * PEER_TOPK))
        return jnp.where(hit, -jnp.inf, s), jnp.where(hit, 1.0, sel)

    _, sel = lax.fori_loop(0, PEER_TOPK, body, (cand, jnp.zeros_like(cand)), unroll=True)
    z = jnp.sum(sel * jnp.exp(cand - cmax), axis=0, keepdims=True)
    iota8 = lax.broadcasted_iota(jnp.int32, (8, cand.shape[1]), 0)
    low = jnp.zeros((8, cand.shape[1]), F32)
    low = jnp.where(iota8 == 0, jnp.sum(sel[0:16], axis=0, keepdims=True), low)
    for j in range(1, 8):
        low = jnp.where(iota8 == j, jnp.sum(sel[8 + 8 * j:16 + 8 * j], axis=0, keepdims=True), low)
    return jnp.concatenate([low, sel[72:80]], axis=0), z


def _peer_topk_kernel(q_ref, keys_ref, cpos_ref, e1_ref, lb1_ref, r2_ref, e2_ref, *, heads):
    nt = (((1,), (1,)), ((), ()))
    for h in range(heads):
        q = q_ref[:, h * 2 * LANES:(h + 1) * 2 * LANES].astype(BF16)
        s1 = lax.dot_general(keys_ref[h, 0], q[:, :LANES], nt, preferred_element_type=F32)
        s2 = lax.dot_general(keys_ref[h, 1], q[:, LANES:], nt, preferred_element_type=F32)
        pos, v1, i1, v2, i2 = _half_ranks(s1, s2)
        counts, z = _staircase(v1, v2, cpos_ref[...])
        lb1 = jnp.zeros_like(s1)
        r2 = jnp.full_like(s2, float(PEER_TOPK))
        for j in range(PEER_TOPK):
            lb1 = jnp.where(pos == i1[j:j + 1], counts[j:j + 1], lb1)
            r2 = jnp.where(pos == i2[j:j + 1], float(j), r2)
        e1_ref[h] = jnp.exp(s1 - v1[0:1])
        lb1_ref[h] = lb1
        r2_ref[h] = r2
        e2_ref[h] = jnp.exp(s2 - v2[0:1]) / z


def peer_topk(q, sub_keys, layer):
    t = q.shape[0]
    heads = PEER_HEADS
    ob = pl.BlockSpec((heads, PEER_N_KEYS, LANES), lambda i, h: (h, 0, i))
    shp = jax.ShapeDtypeStruct((PEER_HEADS, PEER_N_KEYS, t), F32)
    return pl.pallas_call(
        functools.partial(_peer_topk_kernel, heads=heads),
        grid=(t // LANES, PEER_HEADS // heads),
        in_specs=[pl.BlockSpec((LANES, heads * 2 * LANES), lambda i, h: (i, h)),
                  pl.BlockSpec((None, heads, 2, PEER_N_KEYS, LANES), lambda i, h: (layer, h, 0, 0, 0)),
                  pl.BlockSpec((_CAND_ROWS, LANES), lambda i, h: (0, 0))],
        out_specs=[ob, ob, ob, ob],
        out_shape=[shp, shp, shp, shp],
        compiler_params=_params(("arbitrary", "arbitrary")),
        name="peer_topk",
    )(q, sub_keys, jnp.asarray(_cand_positions()))


_SQRT_HALF = float(np.sqrt(0.5))


def _peer_dense_kernel(ht_ref, u_ref, v_ref, e1_ref, lb1_ref, r2_ref, e2_ref, x_ref, g_ref, o_ref,
                       acc_ref, at_ref, wg_ref, *, tt, ec):
    i = pl.program_id(0)
    c = pl.program_id(1)

    @pl.when(c == 0)
    def _():
        acc_ref[...] = jnp.zeros_like(acc_ref)

    at_ref[...] = jnp.dot(u_ref[...], ht_ref[...], preferred_element_type=F32)
    n_a = ec // PEER_N_KEYS
    grp0 = pl.multiple_of((c * n_a) // 8 * 8, 8)
    off = (c * n_a) % 8

    def key_row(ref, h, al, lanes):
        blk = ref[h, pl.ds(grp0, 8), lanes]
        row = blk[al:al + 1]
        for o in range(n_a, 8, n_a):
            row = jnp.where(off == o, blk[o + al:o + al + 1], row)
        return row

    for al in range(n_a):
        rows = slice(al * PEER_N_KEYS, (al + 1) * PEER_N_KEYS)
        for lt in range(tt // LANES):
            lanes = slice(lt * LANES, (lt + 1) * LANES)
            gate = jnp.zeros((PEER_N_KEYS, LANES), F32)
            for h in range(PEER_HEADS):
                lb = key_row(lb1_ref, h, al, lanes)
                e1 = key_row(e1_ref, h, al, lanes)
                gate = gate + jnp.where(r2_ref[h, :, lanes] < lb, e2_ref[h, :, lanes], 0.0) * e1
            act = at_ref[rows, lanes]
            wg_ref[rows, lanes] = 0.5 * act * (1.0 + lax.erf(act * _SQRT_HALF)) * gate
    acc_ref[...] += jnp.dot(wg_ref[...].T.astype(BF16), v_ref[...], preferred_element_type=F32)

    @pl.when(c == pl.num_programs(1) - 1)
    def _():
        grp = _group_of_row(i * tt)
        o_ref[...] = x_ref[...] + g_ref[pl.ds(grp, 1), :] * acc_ref[...]


def peer_dense(ht, peer_u, peer_v, e1, lb1, r2, e2, x, mods, layer, gate_chunk):
    t = x.shape[0]
    tt, ec = PEER_TOK_TILE, PEER_EXP_TILE
    once = pl.Buffered(1)
    kb = pl.BlockSpec((PEER_HEADS, PEER_N_KEYS, tt), lambda i, c: (0, 0, i), pipeline_mode=once)
    return pl.pallas_call(
        functools.partial(_peer_dense_kernel, tt=tt, ec=ec),
        grid=(t // tt, PEER_N_EXPERTS // ec),
        in_specs=[pl.BlockSpec((D_MODEL, tt), lambda i, c: (0, i)),
                  pl.BlockSpec((None, ec, D_MODEL), lambda i, c: (layer, c, 0)),
                  pl.BlockSpec((None, ec, D_MODEL), lambda i, c: (layer, c, 0)),
                  kb, kb, kb, kb,
                  pl.BlockSpec((tt, D_MODEL), lambda i, c: (i, 0), pipeline_mode=once),
                  pl.BlockSpec((None, 8, D_MODEL), lambda i, c: (layer, 0, gate_chunk))],
        out_specs=pl.BlockSpec((tt, D_MODEL), lambda i, c: (i, 0)),
        out_shape=jax.ShapeDtypeStruct((t, D_MODEL), F32),
        scratch_shapes=[pltpu.VMEM((tt, D_MODEL), F32),
                        pltpu.VMEM((ec, tt), F32),
                        pltpu.VMEM((ec, tt), F32)],
        compiler_params=_params(("arbitrary", "arbitrary")),
        name="peer_dense",
    )(ht, peer_u, peer_v, e1, lb1, r2, e2, x, mods)


def _rope_tables():
    t = jnp.arange(LAT_SEQ)
    row = (t // GRID_W).astype(F32)
    col = (t % GRID_W).astype(F32)
    n_freq = MLA_ROPE // 4
    inv = ROPE_THETA ** (-jnp.arange(n_freq, dtype=F32) / n_freq)
    ang = jnp.concatenate([row[:, None] * inv, col[:, None] * inv], axis=-1)
    cos, sin = jnp.cos(ang), jnp.sin(ang)
    zero = jnp.zeros_like(cos)
    pad = jnp.zeros((LAT_SEQ, LANES - MLA_ROPE), F32)
    c_lat = jnp.concatenate([cos, cos, pad], axis=-1)
    s1_lat = jnp.concatenate([-sin, zero, pad], axis=-1)
    s2_lat = jnp.concatenate([zero, sin, pad], axis=-1)
    ones = jnp.concatenate([jnp.ones((1, MLA_ROPE), F32), jnp.zeros((1, LANES - MLA_ROPE), F32)], axis=-1)

    def full(lat, ident):
        return jnp.concatenate([jnp.broadcast_to(ident, (T_CTX, LANES)), jnp.tile(lat, (N_LAT_SEQ, 1)),
                                jnp.broadcast_to(ident, (T_CACHE, LANES))], axis=0)

    zeros = jnp.zeros((1, LANES), F32)
    return full(c_lat, ones), full(s1_lat, zeros), full(s2_lat, zeros)


def _na_bias_table(rel_bias):
    cq = np.arange(GRID_W)
    col_start = np.clip(cq - NA_KC // 2, 0, GRID_W - NA_KC)
    valid = (cq[None, :] >= col_start[:, None]) & (cq[None, :] < col_start[:, None] + NA_KC)
    coff = np.clip(cq[None, :] - cq[:, None], -(NA_KC - 1), NA_KC - 1) + (NA_KC - 1)
    onehot = (coff[:, :, None] == np.arange(2 * NA_KC - 1)[None, None, :]).astype(np.float32)
    toep = jnp.einsum('lhrc,qkc->lhrqk', rel_bias, jnp.asarray(onehot), precision=lax.Precision.HIGHEST)
    toep = jnp.where(jnp.asarray(valid)[None, None, None], toep, -jnp.inf)
    tabs = []
    for d in range(NA_KR):
        rows = toep[:, :, NA_KR - 1 - d:2 * NA_KR - 1 - d]
        tabs.append(rows.transpose(0, 1, 3, 2, 4).reshape(DEPTH, NA_HEADS, GRID_W, NA_LOCAL))
    return jnp.stack(tabs, axis=2)


def kernel(x_prompt, x_sample, cache_mla_ckv, cache_mla_kpe, cache_na_k, cache_na_v, c, c_ctx, ada_w, ada_b, norm_mix_g, norm_ffn_g, w_in, mla_q_norm_g, mla_w_uq, mla_kv_norm_g, mla_w_ukv, mla_q_head_g, mla_k_head_g, na_q_head_g, na_k_head_g, na_rel_bias, conv_w, w_out, peer_w_q, peer_sub_keys, peer_u, peer_v):
    w_in_p = jnp.concatenate([w_in[..., :512], w_in[..., 832:3904], w_in[..., 512:832],
                              jnp.zeros((DEPTH, D_MODEL, D_IN_PAD - 3904), F32)], axis=-1).astype(BF16)
    wuq_p = jnp.pad(mla_w_uq.reshape(DEPTH, MLA_Q_LORA, MLA_HEADS, MLA_QK_DIM),
                    ((0, 0), (0, 0), (0, 0), (0, MLA_HEAD_PAD - MLA_QK_DIM))
                    ).reshape(DEPTH, MLA_Q_LORA, MLA_HEADS * MLA_HEAD_PAD).astype(BF16)
    qhg_p = jnp.pad(mla_q_head_g, ((0, 0), (0, MLA_HEAD_PAD - MLA_QK_DIM))).reshape(DEPTH, 1, MLA_HEAD_PAD)
    wukv4 = mla_w_ukv.reshape(DEPTH, MLA_KV_LORA, MLA_HEADS, 2 * MLA_NOPE)
    wukv_p = jnp.concatenate([wukv4[..., :MLA_NOPE].reshape(DEPTH, MLA_KV_LORA, -1),
                              wukv4[..., MLA_NOPE:].reshape(DEPTH, MLA_KV_LORA, -1)], axis=-1).astype(BF16)
    khg_n = mla_k_head_g[:, :MLA_NOPE].reshape(DEPTH, 1, LANES)
    khg_p = jnp.pad(mla_k_head_g[:, MLA_NOPE:], ((0, 0), (0, LANES - MLA_ROPE))).reshape(DEPTH, 1, LANES)
    conv_w8 = jnp.pad(conv_w.transpose(0, 2, 1), ((0, 0), (0, 5), (0, 0)))
    w_out_b = w_out.astype(BF16)
    peer_wq_b = peer_w_q.astype(BF16)
    sub_keys_b = peer_sub_keys.astype(BF16)
    peer_u_b = peer_u.astype(BF16)
    peer_v_b = peer_v.astype(BF16)
    g_mix = norm_mix_g.reshape(DEPTH, 1, D_MODEL)
    g_ffn = norm_ffn_g.reshape(DEPTH, 1, D_MODEL)
    qng = mla_q_norm_g.reshape(DEPTH, 1, MLA_Q_LORA)
    kvng = mla_kv_norm_g.reshape(DEPTH, 1, MLA_KV_LORA)
    naqg = na_q_head_g.reshape(DEPTH, 1, NA_HEAD_DIM)
    nakg = na_k_head_g.reshape(DEPTH, 1, NA_HEAD_DIM)
    rope_c, rope_s1, rope_s2 = _rope_tables()
    na_bias_tab = _na_bias_table(na_rel_bias)
    cache_kpe_p = jnp.pad(cache_mla_kpe, ((0, 0), (0, 0), (0, 0), (0, LANES - MLA_ROPE)))
    cache_nak = cache_na_k.transpose(1, 0, 3, 2, 4).reshape(DEPTH, N_LAT_SEQ, PAST_LEN, NA_WIDTH).astype(BF16)
    cache_nav = cache_na_v.transpose(1, 0, 3, 2, 4).reshape(DEPTH, N_LAT_SEQ, PAST_LEN, NA_WIDTH).astype(BF16)

    cpad = jnp.concatenate([c_ctx[None, :], c, jnp.zeros((8 - 1 - N_LAT_SEQ, D_MODEL), F32)], axis=0)
    mods = ada_all(cpad, ada_w, ada_b)

    x = jnp.concatenate([x_prompt.reshape(T_CTX, D_MODEL), x_sample.reshape(T_LAT, D_MODEL)], axis=0)
    ckv_out, kpe_out, nak_out, nav_out = [], [], [], []
    mla_scale = MLA_QK_DIM ** -0.5
    na_scale = NA_HEAD_DIM ** -0.5

    ident_c = jnp.broadcast_to(rope_c[:1], (SEQ_TILE, LANES))
    ident_s = jnp.zeros((SEQ_TILE, LANES), F32)
    k_cache, v_cache = kv_prep_cache(
        cache_mla_ckv.transpose(1, 0, 2, 3).reshape(DEPTH, T_CACHE, MLA_KV_LORA),
        cache_kpe_p.transpose(1, 0, 2, 3).reshape(DEPTH, T_CACHE, LANES), ident_c, ident_s, wukv_p, khg_n, khg_p)

    for l in range(DEPTH):
        (z,) = mod_matmul(x, g_mix, mods, w_in_p, l, 0, 1, tn=1024, emit_ht=False)
        q_mla, ckv_n, q_na, k_na_f, k_na_b, v_na_b = proj_prep(
            z, rope_c, rope_s1, rope_s2, qng, wuq_p, qhg_p, kvng, naqg, nakg, l)
        k_mla, v_mla = kv_prep(ckv_n, z, rope_c, rope_s1, rope_s2, wukv_p, khg_n, khg_p, l)

        o_mla_ctx = attention(q_mla, k_mla, v_mla, n_seq=N_CTX_SEQ, n_heads=MLA_HEADS, dq=MLA_HEAD_PAD,
                              sq_total=CTX_SEQ, sk=CTX_SEQ, q_row_off=0, k_row_off=0, scale=mla_scale)
        o_mla_lat = latent_mla_attention(q_mla, k_mla, v_mla, k_cache, v_cache, l, scale=mla_scale)
        o_na_ctx = attention(q_na, k_na_b, v_na_b, n_seq=N_CTX_SEQ, n_heads=NA_HEADS, dq=NA_HEAD_DIM,
                             sq_total=CTX_SEQ, sk=CTX_SEQ, q_row_off=0, k_row_off=0, scale=na_scale)
        o_na_lat = na_latent_attention(q_na, k_na_b, v_na_b, cache_nak, cache_nav, na_bias_tab, l, scale=na_scale)
        conv_ctx = short_conv(z, conv_w8, l, n_seq=N_CTX_SEQ, seq=CTX_SEQ, row_off=0)
        conv_lat = short_conv(z, conv_w8, l, n_seq=N_LAT_SEQ, seq=LAT_SEQ, row_off=T_CTX)

        x = out_matmul(x, mods, (o_mla_ctx, o_mla_lat), (o_na_ctx, o_na_lat), (conv_ctx, conv_lat), w_out_b, l, 2)

        q_peer, ht = mod_matmul(x, g_ffn, mods, peer_wq_b, l, 3, 4, tn=1024, emit_ht=True)
        e1, lb1, r2, e2 = peer_topk(q_peer, sub_keys_b, l)
        x = peer_dense(ht, peer_u_b, peer_v_b, e1, lb1, r2, e2, x, mods, l, 5)

        ckv_out.append(ckv_n[:T_CTX].reshape(N_CTX_SEQ, CTX_SEQ, MLA_KV_LORA))
        kpe_out.append(z[:T_CTX, Z_KPE_COL:Z_KPE_COL + MLA_ROPE].reshape(N_CTX_SEQ, CTX_SEQ, MLA_ROPE))
        nak_out.append(k_na_f[:T_CTX].reshape(N_CTX_SEQ, CTX_SEQ, NA_HEADS, NA_HEAD_DIM).transpose(0, 2, 1, 3))
        nav_out.append(z[:T_CTX, 1536:2048].reshape(N_CTX_SEQ, CTX_SEQ, NA_HEADS, NA_HEAD_DIM
                                                    ).transpose(0, 2, 1, 3))

    y_prompt = x[:T_CTX].reshape(N_CTX_SEQ, CTX_SEQ, D_MODEL)
    y_sample = x[T_CTX:].reshape(N_LAT_SEQ, LAT_SEQ, D_MODEL)
    return (y_prompt, y_sample, jnp.stack(ckv_out, axis=1), jnp.stack(kpe_out, axis=1),
            jnp.stack(nak_out, axis=1), jnp.stack(nav_out, axis=1))
```

```python
import functools

import numpy as np
import jax
import jax.numpy as jnp
from jax import lax
from jax.experimental import pallas as pl
from jax.experimental.pallas import tpu as pltpu

F32 = jnp.float32
BF16 = jnp.bfloat16

D_MODEL = 2048
DEPTH = 4
N_CTX_SEQ = 32
CTX_SEQ = 256
N_LAT_SEQ = 2
LAT_SEQ = 1024
PAST_LEN = 512
T_CTX = N_CTX_SEQ * CTX_SEQ
T_LAT = N_LAT_SEQ * LAT_SEQ
T_ALL = T_CTX + T_LAT
T_CACHE = N_LAT_SEQ * PAST_LEN
GRID_W = 64
NORM_EPS = 1e-6
ROPE_THETA = 10000.0
ADA_CHUNKS = 6

MLA_HEADS = 8
MLA_NOPE = 128
MLA_ROPE = 64
MLA_QK_DIM = MLA_NOPE + MLA_ROPE
MLA_Q_LORA = 512
MLA_KV_LORA = 256
MLA_HEAD_PAD = 256
NA_HEADS = 4
NA_HEAD_DIM = 128
NA_WIDTH = NA_HEADS * NA_HEAD_DIM
NA_KR = 8
NA_KC = 16
CONV_CH = 512
D_IN_PAD = 4096
Z_CKV_BLOCK = 7
Z_KPE_COL = 3840

PEER_HEADS = 8
PEER_N_KEYS = 128
PEER_N_EXPERTS = PEER_N_KEYS * PEER_N_KEYS
PEER_TOPK = 16

LANES = 128
VMEM_LIMIT = 56 * 1024 * 1024

ROW_TILE = 1024
SEQ_TILE = 256
PEER_TOK_TILE = 512
PEER_EXP_TILE = 1024


def _params(sem, flags=None):
    return pltpu.CompilerParams(dimension_semantics=sem, vmem_limit_bytes=VMEM_LIMIT, flags=flags)


def _group_of_row(row):
    return jnp.where(row < T_CTX, 0, 1 + (row - T_CTX) // LAT_SEQ)


def _rms(x, g, n):
    ms = jnp.sum(x * x, axis=-1, keepdims=True) / n
    return x * lax.rsqrt(ms + NORM_EPS) * g


def _ada_kernel(c_ref, w_ref, b_ref, o_ref):
    c = c_ref[...]
    s = c * jax.nn.sigmoid(c)
    o_ref[...] = jnp.dot(s.astype(BF16), w_ref[...].astype(BF16), preferred_element_type=F32) + b_ref[...]


def ada_all(cpad, ada_w, ada_b):
    tn = 1536
    n = ADA_CHUNKS * D_MODEL
    return pl.pallas_call(
        _ada_kernel,
        grid=(DEPTH, n // tn),
        in_specs=[pl.BlockSpec((8, D_MODEL), lambda l, j: (0, 0)),
                  pl.BlockSpec((None, D_MODEL, tn), lambda l, j: (l, 0, j)),
                  pl.BlockSpec((None, 1, tn), lambda l, j: (l, 0, j))],
        out_specs=pl.BlockSpec((None, 8, tn), lambda l, j: (l, 0, j)),
        out_shape=jax.ShapeDtypeStruct((DEPTH, 8, n), F32),
        compiler_params=_params(("arbitrary", "arbitrary")),
        name="ada",
    )(cpad, ada_w, ada_b.reshape(DEPTH, 1, n))


def _mod_matmul_kernel(x_ref, g_ref, sh_ref, sc_ref, w_ref, z_ref, *rest, tm, emit_ht):
    if emit_ht:
        ht_ref, h_scr = rest
    else:
        (h_scr,) = rest
    i = pl.program_id(0)

    @pl.when(pl.program_id(1) == 0)
    def _():
        grp = _group_of_row(i * tm)
        y = _rms(x_ref[...], g_ref[...], D_MODEL)
        h = y * (1.0 + sc_ref[pl.ds(grp, 1), :]) + sh_ref[pl.ds(grp, 1), :]
        h_scr[...] = h.astype(BF16)
        if emit_ht:
            ht_ref[...] = h.T.astype(BF16)

    z_ref[...] = jnp.dot(h_scr[...], w_ref[...], preferred_element_type=F32).astype(z_ref.dtype)


def mod_matmul(x, gain, mods, w, layer, shift_chunk, scale_chunk, *, tn, emit_ht, tm=ROW_TILE, out_dtype=F32):
    t = x.shape[0]
    n = w.shape[-1]
    out_shape = [jax.ShapeDtypeStruct((t, n), out_dtype)]
    out_specs = [pl.BlockSpec((tm, tn), lambda i, j: (i, j))]
    if emit_ht:
        out_shape.append(jax.ShapeDtypeStruct((D_MODEL, t), BF16))
        out_specs.append(pl.BlockSpec((D_MODEL, tm), lambda i, j: (0, i)))
    res = pl.pallas_call(
        functools.partial(_mod_matmul_kernel, tm=tm, emit_ht=emit_ht),
        grid=(t // tm, n // tn),
        in_specs=[pl.BlockSpec((tm, D_MODEL), lambda i, j: (i, 0)),
                  pl.BlockSpec((None, 1, D_MODEL), lambda i, j: (layer, 0, 0)),
                  pl.BlockSpec((None, 8, D_MODEL), lambda i, j: (layer, 0, shift_chunk)),
                  pl.BlockSpec((None, 8, D_MODEL), lambda i, j: (layer, 0, scale_chunk)),
                  pl.BlockSpec((None, D_MODEL, tn), lambda i, j: (layer, 0, j))],
        out_specs=out_specs,
        out_shape=out_shape,
        scratch_shapes=[pltpu.VMEM((tm, D_MODEL), BF16)],
        compiler_params=_params(("arbitrary", "arbitrary")),
        name="mod_matmul_ht" if emit_ht else "mod_matmul",
    )(x, gain, mods, mods, w)
    return res


def _rope128(x, c, s1, s2):
    return x * c + pltpu.roll(x, 96, 1) * s1 + pltpu.roll(x, 32, 1) * s2


def _proj_prep_kernel(cq_ref, naq_ref, nak_ref, nav_ref, ckv_ref, rc_ref, rs1_ref, rs2_ref,
                      qng_ref, wuq_ref, qhg_ref, kvng_ref, naqg_ref, nakg_ref,
                      qmla_ref, ckvn_ref, qna_ref, knaf_ref, knab_ref, vnab_ref):
    cqn = _rms(cq_ref[...], qng_ref[...], MLA_Q_LORA)
    q = jnp.dot(cqn.astype(BF16), wuq_ref[...], preferred_element_type=F32)
    rc, rs1, rs2 = rc_ref[...], rs1_ref[...], rs2_ref[...]
    for h in range(MLA_HEADS):
        qh = _rms(q[:, h * MLA_HEAD_PAD:(h + 1) * MLA_HEAD_PAD], qhg_ref[...], MLA_QK_DIM)
        qmla_ref[:, h * MLA_HEAD_PAD:h * MLA_HEAD_PAD + LANES] = qh[:, :LANES].astype(BF16)
        qmla_ref[:, h * MLA_HEAD_PAD + LANES:(h + 1) * MLA_HEAD_PAD] = (
            _rope128(qh[:, LANES:], rc, rs1, rs2).astype(BF16))
    ckvn_ref[...] = _rms(ckv_ref[:, :MLA_KV_LORA], kvng_ref[...], MLA_KV_LORA)
    for h in range(NA_HEADS):
        sl = slice(h * NA_HEAD_DIM, (h + 1) * NA_HEAD_DIM)
        qna_ref[:, sl] = _rms(naq_ref[:, sl], naqg_ref[...], NA_HEAD_DIM).astype(BF16)
        kn = _rms(nak_ref[:, sl], nakg_ref[...], NA_HEAD_DIM)
        knaf_ref[:, sl] = kn
        knab_ref[:, sl] = kn.astype(BF16)
    vnab_ref[...] = nav_ref[...].astype(BF16)


def proj_prep(z, rope_c, rope_s1, rope_s2, qng, wuq, qhg, kvng, naqg, nakg, layer):
    t = z.shape[0]
    tm = SEQ_TILE
    zb = lambda k: pl.BlockSpec((tm, 512), lambda i: (i, k))
    rb = pl.BlockSpec((tm, LANES), lambda i: (i, 0))
    wl = lambda *shape: pl.BlockSpec((None,) + shape, lambda i: (layer,) + (0,) * len(shape))
    ob = lambda w: pl.BlockSpec((tm, w), lambda i: (i, 0))
    return pl.pallas_call(
        _proj_prep_kernel,
        grid=(t // tm,),
        in_specs=[zb(0), zb(1), zb(2), zb(3), zb(Z_CKV_BLOCK), rb, rb, rb,
                  wl(1, MLA_Q_LORA), wl(MLA_Q_LORA, MLA_HEADS * MLA_HEAD_PAD), wl(1, MLA_HEAD_PAD),
                  wl(1, MLA_KV_LORA), wl(1, NA_HEAD_DIM), wl(1, NA_HEAD_DIM)],
        out_specs=[ob(MLA_HEADS * MLA_HEAD_PAD), ob(MLA_KV_LORA), ob(NA_WIDTH), ob(NA_WIDTH), ob(NA_WIDTH),
                   ob(NA_WIDTH)],
        out_shape=[jax.ShapeDtypeStruct((t, MLA_HEADS * MLA_HEAD_PAD), BF16),
                   jax.ShapeDtypeStruct((t, MLA_KV_LORA), F32),
                   jax.ShapeDtypeStruct((t, NA_WIDTH), BF16),
                   jax.ShapeDtypeStruct((t, NA_WIDTH), F32),
                   jax.ShapeDtypeStruct((t, NA_WIDTH), BF16),
                   jax.ShapeDtypeStruct((t, NA_WIDTH), BF16)],
        compiler_params=_params(("arbitrary",)),
        name="proj_prep",
    )(z, z, z, z, z, rope_c, rope_s1, rope_s2, qng, wuq, qhg, kvng, naqg, nakg)


def _kv_prep_kernel(ckv_ref, kpe_ref, rc_ref, rs1_ref, rs2_ref, wukv_ref, gn_ref, gp_ref, k_ref, v_ref):
    kv = jnp.dot(ckv_ref[...].astype(BF16), wukv_ref[...], preferred_element_type=F32)
    kpe = kpe_ref[...]
    pe_ss = jnp.sum(kpe * kpe, axis=-1, keepdims=True)
    rc, rs1, rs2 = rc_ref[...], rs1_ref[...], rs2_ref[...]
    for h in range(MLA_HEADS):
        kn = kv[:, h * MLA_NOPE:(h + 1) * MLA_NOPE]
        ms = (jnp.sum(kn * kn, axis=-1, keepdims=True) + pe_ss) / MLA_QK_DIM
        r = lax.rsqrt(ms + NORM_EPS)
        k_ref[:, h * MLA_HEAD_PAD:h * MLA_HEAD_PAD + LANES] = (kn * r * gn_ref[...]).astype(BF16)
        k_ref[:, h * MLA_HEAD_PAD + LANES:(h + 1) * MLA_HEAD_PAD] = (
            _rope128(kpe * r * gp_ref[...], rc, rs1, rs2).astype(BF16))
    v_ref[...] = kv[:, MLA_HEADS * MLA_NOPE:].astype(BF16)


def kv_prep(ckv_n, z, rope_c, rope_s1, rope_s2, wukv, gn, gp, layer):
    t = ckv_n.shape[0]
    tm = SEQ_TILE
    rb = lambda w: pl.BlockSpec((tm, w), lambda i: (i, 0))
    wl = lambda *shape: pl.BlockSpec((None,) + shape, lambda i: (layer,) + (0,) * len(shape))
    return pl.pallas_call(
        _kv_prep_kernel,
        grid=(t // tm,),
        in_specs=[rb(MLA_KV_LORA), pl.BlockSpec((tm, LANES), lambda i: (i, Z_KPE_COL // LANES)),
                  rb(LANES), rb(LANES), rb(LANES),
                  wl(MLA_KV_LORA, 2 * MLA_HEADS * MLA_NOPE), wl(1, LANES), wl(1, LANES)],
        out_specs=[rb(MLA_HEADS * MLA_HEAD_PAD), rb(MLA_HEADS * MLA_NOPE)],
        out_shape=[jax.ShapeDtypeStruct((t, MLA_HEADS * MLA_HEAD_PAD), BF16),
                   jax.ShapeDtypeStruct((t, MLA_HEADS * MLA_NOPE), BF16)],
        compiler_params=_params(("arbitrary",)),
        name="kv_prep",
    )(ckv_n, z, rope_c, rope_s1, rope_s2, wukv, gn, gp)


def kv_prep_cache(ckv, kpe, ident_c, ident_s, wukv, gn, gp):
    tm = SEQ_TILE
    rb = lambda w: pl.BlockSpec((None, tm, w), lambda l, i: (l, i, 0))
    tb = pl.BlockSpec((tm, LANES), lambda l, i: (0, 0))
    wl = lambda *shape: pl.BlockSpec((None,) + shape, lambda l, i: (l,) + (0,) * len(shape))
    return pl.pallas_call(
        _kv_prep_kernel,
        grid=(DEPTH, T_CACHE // tm),
        in_specs=[rb(MLA_KV_LORA), rb(LANES), tb, tb, tb,
                  wl(MLA_KV_LORA, 2 * MLA_HEADS * MLA_NOPE), wl(1, LANES), wl(1, LANES)],
        out_specs=[rb(MLA_HEADS * MLA_HEAD_PAD), rb(MLA_HEADS * MLA_NOPE)],
        out_shape=[jax.ShapeDtypeStruct((DEPTH, T_CACHE, MLA_HEADS * MLA_HEAD_PAD), BF16),
                   jax.ShapeDtypeStruct((DEPTH, T_CACHE, MLA_HEADS * MLA_NOPE), BF16)],
        compiler_params=_params(("arbitrary", "arbitrary")),
        name="kv_prep_cache",
    )(ckv, kpe, ident_c, ident_s, ident_s, wukv, gn, gp)


_NT = (((1,), (1,)), ((), ()))


def _attn_kernel(q_ref, k_ref, v_ref, o_ref, *, scale, n_heads, dq):
    dv = NA_HEAD_DIM
    for h in range(n_heads):
        q = q_ref[:, h * dq:(h + 1) * dq]
        k = k_ref[:, h * dq:(h + 1) * dq]
        s = lax.dot_general(q, k, _NT, preferred_element_type=F32) * scale
        m = jnp.max(s, axis=-1, keepdims=True)
        p = jnp.exp(s - m)
        p = p / jnp.sum(p, axis=-1, keepdims=True)
        o = jnp.dot(p.astype(BF16), v_ref[:, h * dv:(h + 1) * dv], preferred_element_type=F32)
        o_ref[:, h * dv:(h + 1) * dv] = o.astype(BF16)


def attention(q, k, v, *, n_seq, n_heads, dq, sq_total, sk, q_row_off, k_row_off, scale):
    sq = 256
    nq = sq_total // sq
    qoff = q_row_off // sq
    koff = k_row_off // sk
    dv = NA_HEAD_DIM
    return pl.pallas_call(
        functools.partial(_attn_kernel, scale=scale, n_heads=n_heads, dq=dq),
        grid=(n_seq, nq),
        in_specs=[pl.BlockSpec((sq, n_heads * dq), lambda b, qi: (qoff + b * nq + qi, 0)),
                  pl.BlockSpec((sk, n_heads * dq), lambda b, qi: (koff + b, 0)),
                  pl.BlockSpec((sk, n_heads * dv), lambda b, qi: (koff + b, 0))],
        out_specs=pl.BlockSpec((sq, n_heads * dv), lambda b, qi: (b * nq + qi, 0)),
        out_shape=jax.ShapeDtypeStruct((n_seq * sq_total, n_heads * dv), BF16),
        compiler_params=_params(("arbitrary", "arbitrary")),
        name="attention",
    )(q, k, v)


def _attn_cached_kernel(q_ref, k_ref, v_ref, kc_ref, vc_ref, o_ref, *, scale, n_heads, dq):
    dv = NA_HEAD_DIM
    for h in range(n_heads):
        qk = slice(h * dq, (h + 1) * dq)
        vv = slice(h * dv, (h + 1) * dv)
        q = q_ref[:, qk]
        s_own = lax.dot_general(q, k_ref[:, qk], _NT, preferred_element_type=F32) * scale
        s_ctx = lax.dot_general(q, kc_ref[:, qk], _NT, preferred_element_type=F32) * scale
        m = jnp.maximum(jnp.max(s_own, axis=-1, keepdims=True), jnp.max(s_ctx, axis=-1, keepdims=True))
        p_own = jnp.exp(s_own - m)
        p_ctx = jnp.exp(s_ctx - m)
        denom = jnp.sum(p_own, axis=-1, keepdims=True) + jnp.sum(p_ctx, axis=-1, keepdims=True)
        o = jnp.dot((p_ctx / denom).astype(BF16), vc_ref[:, vv], preferred_element_type=F32)
        o = o + jnp.dot((p_own / denom).astype(BF16), v_ref[:, vv], preferred_element_type=F32)
        o_ref[:, vv] = o.astype(BF16)


def latent_mla_attention(q, k, v, k_cache, v_cache, layer, *, scale):
    sq = 256
    nq = LAT_SEQ // sq
    qoff = T_CTX // sq
    koff = T_CTX // LAT_SEQ
    wq = MLA_HEADS * MLA_HEAD_PAD
    wv = MLA_HEADS * NA_HEAD_DIM
    return pl.pallas_call(
        functools.partial(_attn_cached_kernel, scale=scale, n_heads=MLA_HEADS, dq=MLA_HEAD_PAD),
        grid=(N_LAT_SEQ, nq),
        in_specs=[pl.BlockSpec((sq, wq), lambda b, qi: (qoff + b * nq + qi, 0)),
                  pl.BlockSpec((LAT_SEQ, wq), lambda b, qi: (koff + b, 0)),
                  pl.BlockSpec((LAT_SEQ, wv), lambda b, qi: (koff + b, 0)),
                  pl.BlockSpec((None, PAST_LEN, wq), lambda b, qi: (layer, b, 0)),
                  pl.BlockSpec((None, PAST_LEN, wv), lambda b, qi: (layer, b, 0))],
        out_specs=pl.BlockSpec((sq, wv), lambda b, qi: (b * nq + qi, 0)),
        out_shape=jax.ShapeDtypeStruct((T_LAT, wv), BF16),
        compiler_params=_params(("arbitrary", "arbitrary")),
        name="latent_mla",
    )(q, k, v, k_cache, v_cache)


NA_ROWS = LAT_SEQ // GRID_W
NA_LOCAL = NA_KR * GRID_W


def _na_lat_kernel(q_ref, k_ref, v_ref, kc_ref, vc_ref, b_ref, o_ref, *, scale):
    rq = pl.program_id(1)
    row_start = jnp.clip(rq - NA_KR // 2, 0, NA_ROWS - NA_KR)
    start = pl.multiple_of(row_start * GRID_W, GRID_W)
    for h in range(NA_HEADS):
        cols = slice(h * NA_HEAD_DIM, (h + 1) * NA_HEAD_DIM)
        q = q_ref[:, cols]
        k_loc = k_ref[pl.ds(start, NA_LOCAL), cols]
        v_loc = v_ref[pl.ds(start, NA_LOCAL), cols]
        s_loc = lax.dot_general(q, k_loc, _NT, preferred_element_type=F32) * scale + b_ref[h, rq - row_start]
        s_ctx = lax.dot_general(q, kc_ref[:, cols], _NT, preferred_element_type=F32) * scale
        m = jnp.maximum(jnp.max(s_loc, axis=-1, keepdims=True), jnp.max(s_ctx, axis=-1, keepdims=True))
        p_loc = jnp.exp(s_loc - m)
        p_ctx = jnp.exp(s_ctx - m)
        denom = jnp.sum(p_loc, axis=-1, keepdims=True) + jnp.sum(p_ctx, axis=-1, keepdims=True)
        o = jnp.dot((p_loc / denom).astype(BF16), v_loc, preferred_element_type=F32)
        o = o + jnp.dot((p_ctx / denom).astype(BF16), vc_ref[:, cols], preferred_element_type=F32)
        o_ref[:, cols] = o.astype(BF16)


def na_latent_attention(q, k, v, k_cache, v_cache, bias_tab, layer, *, scale):
    qoff = T_CTX // GRID_W
    koff = T_CTX // LAT_SEQ
    w = NA_WIDTH
    return pl.pallas_call(
        functools.partial(_na_lat_kernel, scale=scale),
        grid=(N_LAT_SEQ, NA_ROWS),
        in_specs=[pl.BlockSpec((GRID_W, w), lambda b, r: (qoff + b * NA_ROWS + r, 0)),
                  pl.BlockSpec((LAT_SEQ, w), lambda b, r: (koff + b, 0)),
                  pl.BlockSpec((LAT_SEQ, w), lambda b, r: (koff + b, 0)),
                  pl.BlockSpec((None, None, PAST_LEN, w), lambda b, r: (layer, b, 0, 0)),
                  pl.BlockSpec((None, None, PAST_LEN, w), lambda b, r: (layer, b, 0, 0)),
                  pl.BlockSpec((None, NA_HEADS, NA_KR, GRID_W, NA_LOCAL), lambda b, r: (layer, 0, 0, 0, 0))],
        out_specs=pl.BlockSpec((GRID_W, w), lambda b, r: (b * NA_ROWS + r, 0)),
        out_shape=jax.ShapeDtypeStruct((T_LAT, NA_WIDTH), BF16),
        compiler_params=_params(("arbitrary", "arbitrary")),
        name="na_latent",
    )(q, k, v, k_cache, v_cache, bias_tab)


def _conv_kernel(gb_ref, gc_ref, u_ref, w_ref, o_ref):
    gu = gc_ref[...] * u_ref[...]
    s = gu.shape[0]
    row = lax.broadcasted_iota(jnp.int32, gu.shape, 0)
    prev = jnp.where(row == 0, 0.0, pltpu.roll(gu, 1, 0))
    nxt = jnp.where(row == s - 1, 0.0, pltpu.roll(gu, s - 1, 0))
    y = prev * w_ref[0:1, :] + gu * w_ref[1:2, :] + nxt * w_ref[2:3, :]
    o_ref[...] = (gb_ref[...] * y).astype(BF16)


def short_conv(z, conv_w8, layer, *, n_seq, seq, row_off):
    off = row_off // seq
    zb = lambda k: pl.BlockSpec((seq, CONV_CH), lambda i: (off + i, k))
    return pl.pallas_call(
        _conv_kernel,
        grid=(n_seq,),
        in_specs=[zb(4), zb(5), zb(6), pl.BlockSpec((None, 8, CONV_CH), lambda i: (layer, 0, 0))],
        out_specs=pl.BlockSpec((seq, CONV_CH), lambda i: (i, 0)),
        out_shape=jax.ShapeDtypeStruct((n_seq * seq, CONV_CH), BF16),
        compiler_params=_params(("arbitrary",)),
        name="short_conv",
    )(z, z, z, conv_w8)


def _out_matmul_kernel(x_ref, g_ref, a0c_ref, a0l_ref, a1c_ref, a1l_ref, a2c_ref, a2l_ref,
                       w0_ref, w1_ref, w2_ref, o_ref, *, tm):
    row = pl.program_id(0) * tm
    grp = _group_of_row(row)
    is_ctx = row < T_CTX
    acc = jnp.dot(jnp.where(is_ctx, a0c_ref[...], a0l_ref[...]), w0_ref[...], preferred_element_type=F32)
    acc = acc + jnp.dot(jnp.where(is_ctx, a1c_ref[...], a1l_ref[...]), w1_ref[...], preferred_element_type=F32)
    acc = acc + jnp.dot(jnp.where(is_ctx, a2c_ref[...], a2l_ref[...]), w2_ref[...], preferred_element_type=F32)
    o_ref[...] = x_ref[...] + g_ref[pl.ds(grp, 1), :] * acc


def out_matmul(x, mods, o_mla, o_na, conv, w_out, layer, gate_chunk):
    t = x.shape[0]
    tm, tn = ROW_TILE, 1024
    nj = D_MODEL // tn
    w_mla = MLA_HEADS * MLA_NOPE
    n_ctx = T_CTX // tm
    ctx = lambda w: pl.BlockSpec((tm, w), lambda i, j: (jnp.minimum(i, n_ctx - 1), 0))
    lat = lambda w: pl.BlockSpec((tm, w), lambda i, j: (jnp.maximum(i - n_ctx, 0), 0))
    return pl.pallas_call(
        functools.partial(_out_matmul_kernel, tm=tm),
        grid=(t // tm, nj),
        in_specs=[pl.BlockSpec((tm, tn), lambda i, j: (i, j)),
                  pl.BlockSpec((None, 8, tn), lambda i, j: (layer, 0, gate_chunk * nj + j)),
                  ctx(w_mla), lat(w_mla), ctx(NA_WIDTH), lat(NA_WIDTH), ctx(CONV_CH), lat(CONV_CH),
                  pl.BlockSpec((None, w_mla, tn), lambda i, j: (layer, 0, j)),
                  pl.BlockSpec((None, NA_WIDTH, tn), lambda i, j: (layer, w_mla // NA_WIDTH, j)),
                  pl.BlockSpec((None, CONV_CH, tn), lambda i, j: (layer, (w_mla + NA_WIDTH) // CONV_CH, j))],
        out_specs=pl.BlockSpec((tm, tn), lambda i, j: (i, j)),
        out_shape=jax.ShapeDtypeStruct((t, D_MODEL), F32),
        compiler_params=_params(("arbitrary", "arbitrary")),
        name="out_matmul",
    )(x, mods, *o_mla, *o_na, *conv, w_out, w_out, w_out)


def _argmax_step(s, pos, big):
    m = jnp.max(s, axis=0, keepdims=True)
    first = jnp.min(jnp.where(s == m, pos, big), axis=0, keepdims=True)
    return m, first, pos == first


def _half_ranks(s1, s2):
    n, width = s1.shape
    pos = lax.broadcasted_iota(jnp.int32, (n, width), 0).astype(F32)
    iota16 = lax.broadcasted_iota(jnp.int32, (PEER_TOPK, width), 0)

    def body(j, carry):
        s1c, v1c, i1c, s2c, v2c, r2c = carry
        m1, first1, hit1 = _argmax_step(s1c, pos, float(n))
        m2, _, hit2 = _argmax_step(s2c, pos, float(n))
        return (jnp.where(hit1, -jnp.inf, s1c), jnp.where(iota16 == j, m1, v1c), jnp.where(iota16 == j, first1, i1c),
                jnp.where(hit2, -jnp.inf, s2c), jnp.where(iota16 == j, m2, v2c), jnp.where(hit2, j, r2c))

    zero16 = jnp.zeros((PEER_TOPK, width), F32)
    rank0 = jnp.full((n, width), float(PEER_TOPK), F32)
    _, v1, i1, _, v2, r2 = lax.fori_loop(0, PEER_TOPK, body, (s1, zero16, zero16, s2, zero16, rank0), unroll=True)
    return pos, v1, i1, v2, r2


_CAND_ROWS = 16 + 7 * 8 + 8


def _cand_positions():
    p = [j2 for j2 in range(16)]
    p += [j1 * 16 + j2 for j1 in range(1, 8) for j2 in range(8)]
    p += [j1 * 16 for j1 in range(8, 16)]
    return np.tile(np.asarray(p, np.float32)[:, None], (1, LANES))


def _staircase(v1, v2, cpos):
    cand = jnp.concatenate([v1[0:1] + v2] + [v1[j:j + 1] + v2[0:8] for j in range(1, 8)] + [v1[8:16] + v2[0:1]],
                           axis=0)
    cmax = cand[0:1]

    def body(j, carry):
        s, sel = carry
        _, _, hit = _argmax_step(s, cpos, float(PEER_TOPK * PEER_TOPK))
        return jnp.where(hit, -jnp.inf, s), jnp.where(hit, 1.0, sel)

    _, sel = lax.fori_loop(0, PEER_TOPK, body, (cand, jnp.zeros_like(cand)), unroll=True)
    z = jnp.sum(sel * jnp.exp(cand - cmax), axis=0, keepdims=True)
    iota8 = lax.broadcasted_iota(jnp.int32, (8, cand.shape[1]), 0)
    low = jnp.zeros((8, cand.shape[1]), F32)
    low = jnp.where(iota8 == 0, jnp.sum(sel[0:16], axis=0, keepdims=True), low)
    for j in range(1, 8):
        low = jnp.where(iota8 == j, jnp.sum(sel[8 + 8 * j:16 + 8 * j], axis=0, keepdims=True), low)
    return jnp.concatenate([low, sel[72:80]], axis=0), z


def _peer_topk_kernel(q_ref, keys_ref, cpos_ref, e1_ref, lb1_ref, r2_ref, e2_ref, *, heads):
    nt = (((1,), (1,)), ((), ()))
    for h in range(heads):
        q = q_ref[:, h * 2 * LANES:(h + 1) * 2 * LANES]
        s1 = lax.dot_general(keys_ref[h, 0], q[:, :LANES], nt, preferred_element_type=F32)
        s2 = lax.dot_general(keys_ref[h, 1], q[:, LANES:], nt, preferred_element_type=F32)
        pos, v1, i1, v2, r2 = _half_ranks(s1, s2)
        counts, z = _staircase(v1, v2, cpos_ref[...])
        lb1 = jnp.zeros_like(s1)
        for j in range(PEER_TOPK):
            lb1 = jnp.where(pos == i1[j:j + 1], counts[j:j + 1], lb1)
        e1_ref[h] = jnp.exp(s1 - v1[0:1])
        lb1_ref[h] = lb1
        r2_ref[h] = r2
        e2_ref[h] = jnp.exp(s2 - v2[0:1]) / z


def peer_topk(q, sub_keys, layer):
    t = q.shape[0]
    heads = PEER_HEADS
    ob = pl.BlockSpec((heads, PEER_N_KEYS, LANES), lambda i, h: (h, 0, i))
    shp = jax.ShapeDtypeStruct((PEER_HEADS, PEER_N_KEYS, t), F32)
    return pl.pallas_call(
        functools.partial(_peer_topk_kernel, heads=heads),
        grid=(t // LANES, PEER_HEADS // heads),
        in_specs=[pl.BlockSpec((LANES, heads * 2 * LANES), lambda i, h: (i, h)),
                  pl.BlockSpec((None, heads, 2, PEER_N_KEYS, LANES), lambda i, h: (layer, h, 0, 0, 0)),
                  pl.BlockSpec((_CAND_ROWS, LANES), lambda i, h: (0, 0))],
        out_specs=[ob, ob, ob, ob],
        out_shape=[shp, shp, shp, shp],
        compiler_params=_params(("arbitrary", "arbitrary")),
        name="peer_topk",
    )(q, sub_keys, jnp.asarray(_cand_positions()))


_SQRT_HALF = float(np.sqrt(0.5))


def _peer_dense_kernel(ht_ref, u_ref, v_ref, e1_ref, lb1_ref, r2_ref, e2_ref, x_ref, g_ref, o_ref,
                       acc_ref, at_ref, wg_ref, *, tt, ec):
    i = pl.program_id(0)
    c = pl.program_id(1)

    @pl.when(c == 0)
    def _():
        acc_ref[...] = jnp.zeros_like(acc_ref)

    at_ref[...] = jnp.dot(u_ref[...], ht_ref[...], preferred_element_type=F32)
    n_a = ec // PEER_N_KEYS
    grp0 = pl.multiple_of((c * n_a) // 8 * 8, 8)
    off = (c * n_a) % 8

    def key_row(ref, h, al, lanes):
        blk = ref[h, pl.ds(grp0, 8), lanes]
        row = blk[al:al + 1]
        for o in range(n_a, 8, n_a):
            row = jnp.where(off == o, blk[o + al:o + al + 1], row)
        return row

    for al in range(n_a):
        rows = slice(al * PEER_N_KEYS, (al + 1) * PEER_N_KEYS)
        for lt in range(tt // LANES):
            lanes = slice(lt * LANES, (lt + 1) * LANES)
            gate = jnp.zeros((PEER_N_KEYS, LANES), F32)
            for h in range(PEER_HEADS):
                lb = key_row(lb1_ref, h, al, lanes)
                e1 = key_row(e1_ref, h, al, lanes)
                gate = gate + jnp.where(r2_ref[h, :, lanes] < lb, e2_ref[h, :, lanes], 0.0) * e1
            act = at_ref[rows, lanes]
            wg_ref[rows, lanes] = 0.5 * act * (1.0 + lax.erf(act * _SQRT_HALF)) * gate
    acc_ref[...] += jnp.dot(wg_ref[...].T.astype(BF16), v_ref[...], preferred_element_type=F32)

    @pl.when(c == pl.num_programs(1) - 1)
    def _():
        grp = _group_of_row(i * tt)
        o_ref[...] = x_ref[...] + g_ref[pl.ds(grp, 1), :] * acc_ref[...]


def peer_dense(ht, peer_u, peer_v, e1, lb1, r2, e2, x, mods, layer, gate_chunk):
    t = x.shape[0]
    tt, ec = PEER_TOK_TILE, PEER_EXP_TILE
    once = pl.Buffered(1)
    kb = pl.BlockSpec((PEER_HEADS, PEER_N_KEYS, tt), lambda i, c: (0, 0, i), pipeline_mode=once)
    return pl.pallas_call(
        functools.partial(_peer_dense_kernel, tt=tt, ec=ec),
        grid=(t // tt, PEER_N_EXPERTS // ec),
        in_specs=[pl.BlockSpec((D_MODEL, tt), lambda i, c: (0, i)),
                  pl.BlockSpec((None, ec, D_MODEL), lambda i, c: (layer, c, 0)),
                  pl.BlockSpec((None, ec, D_MODEL), lambda i, c: (layer, c, 0)),
                  kb, kb, kb, kb,
                  pl.BlockSpec((tt, D_MODEL), lambda i, c: (i, 0), pipeline_mode=once),
                  pl.BlockSpec((None, 8, D_MODEL), lambda i, c: (layer, 0, gate_chunk))],
        out_specs=pl.BlockSpec((tt, D_MODEL), lambda i, c: (i, 0)),
        out_shape=jax.ShapeDtypeStruct((t, D_MODEL), F32),
        scratch_shapes=[pltpu.VMEM((tt, D_MODEL), F32),
                        pltpu.VMEM((ec, tt), F32),
                        pltpu.VMEM((ec, tt), F32)],
        compiler_params=_params(("arbitrary", "arbitrary")),
        name="peer_dense",
    )(ht, peer_u, peer_v, e1, lb1, r2, e2, x, mods)


def _rope_tables():
    t = jnp.arange(LAT_SEQ)
    row = (t // GRID_W).astype(F32)
    col = (t % GRID_W).astype(F32)
    n_freq = MLA_ROPE // 4
    inv = ROPE_THETA ** (-jnp.arange(n_freq, dtype=F32) / n_freq)
    ang = jnp.concatenate([row[:, None] * inv, col[:, None] * inv], axis=-1)
    cos, sin = jnp.cos(ang), jnp.sin(ang)
    zero = jnp.zeros_like(cos)
    pad = jnp.zeros((LAT_SEQ, LANES - MLA_ROPE), F32)
    c_lat = jnp.concatenate([cos, cos, pad], axis=-1)
    s1_lat = jnp.concatenate([-sin, zero, pad], axis=-1)
    s2_lat = jnp.concatenate([zero, sin, pad], axis=-1)
    ones = jnp.concatenate([jnp.ones((1, MLA_ROPE), F32), jnp.zeros((1, LANES - MLA_ROPE), F32)], axis=-1)

    def full(lat, ident):
        return jnp.concatenate([jnp.broadcast_to(ident, (T_CTX, LANES)), jnp.tile(lat, (N_LAT_SEQ, 1)),
                                jnp.broadcast_to(ident, (T_CACHE, LANES))], axis=0)

    zeros = jnp.zeros((1, LANES), F32)
    return full(c_lat, ones), full(s1_lat, zeros), full(s2_lat, zeros)


def _na_bias_table(rel_bias):
    cq = np.arange(GRID_W)
    col_start = np.clip(cq - NA_KC // 2, 0, GRID_W - NA_KC)
    valid = (cq[None, :] >= col_start[:, None]) & (cq[None, :] < col_start[:, None] + NA_KC)
    coff = np.clip(cq[None, :] - cq[:, None], -(NA_KC - 1), NA_KC - 1) + (NA_KC - 1)
    onehot = (coff[:, :, None] == np.arange(2 * NA_KC - 1)[None, None, :]).astype(np.float32)
    toep = jnp.einsum('lhrc,qkc->lhrqk', rel_bias, jnp.asarray(onehot), precision=lax.Precision.HIGHEST)
    toep = jnp.where(jnp.asarray(valid)[None, None, None], toep, -jnp.inf)
    tabs = []
    for d in range(NA_KR):
        rows = toep[:, :, NA_KR - 1 - d:2 * NA_KR - 1 - d]
        tabs.append(rows.transpose(0, 1, 3, 2, 4).reshape(DEPTH, NA_HEADS, GRID_W, NA_LOCAL))
    return jnp.stack(tabs, axis=2)


def kernel(x_prompt, x_sample, cache_mla_ckv, cache_mla_kpe, cache_na_k, cache_na_v, c, c_ctx, ada_w, ada_b, norm_mix_g, norm_ffn_g, w_in, mla_q_norm_g, mla_w_uq, mla_kv_norm_g, mla_w_ukv, mla_q_head_g, mla_k_head_g, na_q_head_g, na_k_head_g, na_rel_bias, conv_w, w_out, peer_w_q, peer_sub_keys, peer_u, peer_v):
    w_in_p = jnp.concatenate([w_in[..., :512], w_in[..., 832:3904], w_in[..., 512:832],
                              jnp.zeros((DEPTH, D_MODEL, D_IN_PAD - 3904), F32)], axis=-1).astype(BF16)
    wuq_p = jnp.pad(mla_w_uq.reshape(DEPTH, MLA_Q_LORA, MLA_HEADS, MLA_QK_DIM),
                    ((0, 0), (0, 0), (0, 0), (0, MLA_HEAD_PAD - MLA_QK_DIM))
                    ).reshape(DEPTH, MLA_Q_LORA, MLA_HEADS * MLA_HEAD_PAD).astype(BF16)
    qhg_p = jnp.pad(mla_q_head_g, ((0, 0), (0, MLA_HEAD_PAD - MLA_QK_DIM))).reshape(DEPTH, 1, MLA_HEAD_PAD)
    wukv4 = mla_w_ukv.reshape(DEPTH, MLA_KV_LORA, MLA_HEADS, 2 * MLA_NOPE)
    wukv_p = jnp.concatenate([wukv4[..., :MLA_NOPE].reshape(DEPTH, MLA_KV_LORA, -1),
                              wukv4[..., MLA_NOPE:].reshape(DEPTH, MLA_KV_LORA, -1)], axis=-1).astype(BF16)
    khg_n = mla_k_head_g[:, :MLA_NOPE].reshape(DEPTH, 1, LANES)
    khg_p = jnp.pad(mla_k_head_g[:, MLA_NOPE:], ((0, 0), (0, LANES - MLA_ROPE))).reshape(DEPTH, 1, LANES)
    conv_w8 = jnp.pad(conv_w.transpose(0, 2, 1), ((0, 0), (0, 5), (0, 0)))
    w_out_b = w_out.astype(BF16)
    peer_wq_b = peer_w_q.astype(BF16)
    sub_keys_b = peer_sub_keys.astype(BF16)
    peer_u_b = peer_u.astype(BF16)
    peer_v_b = peer_v.astype(BF16)
    g_mix = norm_mix_g.reshape(DEPTH, 1, D_MODEL)
    g_ffn = norm_ffn_g.reshape(DEPTH, 1, D_MODEL)
    qng = mla_q_norm_g.reshape(DEPTH, 1, MLA_Q_LORA)
    kvng = mla_kv_norm_g.reshape(DEPTH, 1, MLA_KV_LORA)
    naqg = na_q_head_g.reshape(DEPTH, 1, NA_HEAD_DIM)
    nakg = na_k_head_g.reshape(DEPTH, 1, NA_HEAD_DIM)
    rope_c, rope_s1, rope_s2 = _rope_tables()
    na_bias_tab = _na_bias_table(na_rel_bias)
    cache_kpe_p = jnp.pad(cache_mla_kpe, ((0, 0), (0, 0), (0, 0), (0, LANES - MLA_ROPE)))
    cache_nak = cache_na_k.transpose(1, 0, 3, 2, 4).reshape(DEPTH, N_LAT_SEQ, PAST_LEN, NA_WIDTH).astype(BF16)
    cache_nav = cache_na_v.transpose(1, 0, 3, 2, 4).reshape(DEPTH, N_LAT_SEQ, PAST_LEN, NA_WIDTH).astype(BF16)

    cpad = jnp.concatenate([c_ctx[None, :], c, jnp.zeros((8 - 1 - N_LAT_SEQ, D_MODEL), F32)], axis=0)
    mods = ada_all(cpad, ada_w, ada_b)

    x = jnp.concatenate([x_prompt.reshape(T_CTX, D_MODEL), x_sample.reshape(T_LAT, D_MODEL)], axis=0)
    ckv_out, kpe_out, nak_out, nav_out = [], [], [], []
    mla_scale = MLA_QK_DIM ** -0.5
    na_scale = NA_HEAD_DIM ** -0.5

    ident_c = jnp.broadcast_to(rope_c[:1], (SEQ_TILE, LANES))
    ident_s = jnp.zeros((SEQ_TILE, LANES), F32)
    k_cache, v_cache = kv_prep_cache(
        cache_mla_ckv.transpose(1, 0, 2, 3).reshape(DEPTH, T_CACHE, MLA_KV_LORA),
        cache_kpe_p.transpose(1, 0, 2, 3).reshape(DEPTH, T_CACHE, LANES), ident_c, ident_s, wukv_p, khg_n, khg_p)

    for l in range(DEPTH):
        (z,) = mod_matmul(x, g_mix, mods, w_in_p, l, 0, 1, tn=1024, emit_ht=False)
        q_mla, ckv_n, q_na, k_na_f, k_na_b, v_na_b = proj_prep(
            z, rope_c, rope_s1, rope_s2, qng, wuq_p, qhg_p, kvng, naqg, nakg, l)
        k_mla, v_mla = kv_prep(ckv_n, z, rope_c, rope_s1, rope_s2, wukv_p, khg_n, khg_p, l)

        o_mla_ctx = attention(q_mla, k_mla, v_mla, n_seq=N_CTX_SEQ, n_heads=MLA_HEADS, dq=MLA_HEAD_PAD,
                              sq_total=CTX_SEQ, sk=CTX_SEQ, q_row_off=0, k_row_off=0, scale=mla_scale)
        o_mla_lat = latent_mla_attention(q_mla, k_mla, v_mla, k_cache, v_cache, l, scale=mla_scale)
        o_na_ctx = attention(q_na, k_na_b, v_na_b, n_seq=N_CTX_SEQ, n_heads=NA_HEADS, dq=NA_HEAD_DIM,
                             sq_total=CTX_SEQ, sk=CTX_SEQ, q_row_off=0, k_row_off=0, scale=na_scale)
        o_na_lat = na_latent_attention(q_na, k_na_b, v_na_b, cache_nak, cache_nav, na_bias_tab, l, scale=na_scale)
        conv_ctx = short_conv(z, conv_w8, l, n_seq=N_CTX_SEQ, seq=CTX_SEQ, row_off=0)
        conv_lat = short_conv(z, conv_w8, l, n_seq=N_LAT_SEQ, seq=LAT_SEQ, row_off=T_CTX)

        x = out_matmul(x, mods, (o_mla_ctx, o_mla_lat), (o_na_ctx, o_na_lat), (conv_ctx, conv_lat), w_out_b, l, 2)

        q_peer, ht = mod_matmul(x, g_ffn, mods, peer_wq_b, l, 3, 4, tn=D_MODEL, emit_ht=True, out_dtype=BF16)
        e1, lb1, r2, e2 = peer_topk(q_peer, sub_keys_b, l)
        x = peer_dense(ht, peer_u_b, peer_v_b, e1, lb1, r2, e2, x, mods, l, 5)

        ckv_out.append(ckv_n[:T_CTX].reshape(N_CTX_SEQ, CTX_SEQ, MLA_KV_LORA))
        kpe_out.append(z[:T_CTX, Z_KPE_COL:Z_KPE_COL + MLA_ROPE].reshape(N_CTX_SEQ, CTX_SEQ, MLA_ROPE))
        nak_out.append(k_na_f[:T_CTX].reshape(N_CTX_SEQ, CTX_SEQ, NA_HEADS, NA_HEAD_DIM).transpose(0, 2, 1, 3))
        nav_out.append(z[:T_CTX, 1536:2048].reshape(N_CTX_SEQ, CTX_SEQ, NA_HEADS, NA_HEAD_DIM
                                                    ).transpose(0, 2, 1, 3))

    y_prompt = x[:T_CTX].reshape(N_CTX_SEQ, CTX_SEQ, D_MODEL)
    y_sample = x[T_CTX:].reshape(N_LAT_SEQ, LAT_SEQ, D_MODEL)
    return (y_prompt, y_sample, jnp.stack(ckv_out, axis=1), jnp.stack(kpe_out, axis=1),
            jnp.stack(nak_out, axis=1), jnp.stack(nav_out, axis=1))
```

```python
import functools

import numpy as np
import jax
import jax.numpy as jnp
from jax import lax
from jax.experimental import pallas as pl
from jax.experimental.pallas import tpu as pltpu

F32 = jnp.float32
BF16 = jnp.bfloat16

D_MODEL = 2048
DEPTH = 4
N_CTX_SEQ = 32
CTX_SEQ = 256
N_LAT_SEQ = 2
LAT_SEQ = 1024
PAST_LEN = 512
T_CTX = N_CTX_SEQ * CTX_SEQ
T_LAT = N_LAT_SEQ * LAT_SEQ
T_ALL = T_CTX + T_LAT
T_CACHE = N_LAT_SEQ * PAST_LEN
GRID_W = 64
NORM_EPS = 1e-6
ROPE_THETA = 10000.0
ADA_CHUNKS = 6

MLA_HEADS = 8
MLA_NOPE = 128
MLA_ROPE = 64
MLA_QK_DIM = MLA_NOPE + MLA_ROPE
MLA_Q_LORA = 512
MLA_KV_LORA = 256
MLA_HEAD_PAD = 256
NA_HEADS = 4
NA_HEAD_DIM = 128
NA_WIDTH = NA_HEADS * NA_HEAD_DIM
NA_KR = 8
NA_KC = 16
CONV_CH = 512
Z_BLOCK = 512
Z_NAV_BLOCK = 3
Z_CONV_BLOCK = 4
Z_CKV_BLOCK = 7
D_IN_PAD = (Z_CKV_BLOCK + 1) * Z_BLOCK
Z_KPE_COL = Z_CKV_BLOCK * Z_BLOCK + MLA_KV_LORA
D_IN = MLA_Q_LORA + MLA_KV_LORA + MLA_ROPE + 3 * NA_WIDTH + 3 * CONV_CH
ATTN_Q_BLOCK = 256

PEER_HEADS = 8
PEER_N_KEYS = 128
PEER_N_EXPERTS = PEER_N_KEYS * PEER_N_KEYS
PEER_TOPK = 16

LANES = 128
VMEM_LIMIT = 56 * 1024 * 1024

ROW_TILE = 1024
SEQ_TILE = 256
PEER_TOK_TILE = 512
PEER_EXP_TILE = 1024


def _params(sem, flags=None):
    return pltpu.CompilerParams(dimension_semantics=sem, vmem_limit_bytes=VMEM_LIMIT, flags=flags)


def _group_of_row(row):
    return jnp.where(row < T_CTX, 0, 1 + (row - T_CTX) // LAT_SEQ)


def _rms(x, g, n):
    ms = jnp.sum(x * x, axis=-1, keepdims=True) / n
    return x * lax.rsqrt(ms + NORM_EPS) * g


def _ada_kernel(c_ref, w_ref, b_ref, o_ref):
    c = c_ref[...]
    s = c * jax.nn.sigmoid(c)
    o_ref[...] = jnp.dot(s.astype(BF16), w_ref[...].astype(BF16), preferred_element_type=F32) + b_ref[...]


def ada_all(cpad, ada_w, ada_b):
    tn = 1536
    n = ADA_CHUNKS * D_MODEL
    return pl.pallas_call(
        _ada_kernel,
        grid=(DEPTH, n // tn),
        in_specs=[pl.BlockSpec((8, D_MODEL), lambda l, j: (0, 0)),
                  pl.BlockSpec((None, D_MODEL, tn), lambda l, j: (l, 0, j)),
                  pl.BlockSpec((None, 1, tn), lambda l, j: (l, 0, j))],
        out_specs=pl.BlockSpec((None, 8, tn), lambda l, j: (l, 0, j)),
        out_shape=jax.ShapeDtypeStruct((DEPTH, 8, n), F32),
        compiler_params=_params(("arbitrary", "arbitrary")),
        name="ada",
    )(cpad, ada_w, ada_b.reshape(DEPTH, 1, n))


def _mod_matmul_kernel(x_ref, g_ref, sh_ref, sc_ref, w_ref, z_ref, *rest, tm, emit_ht):
    if emit_ht:
        ht_ref, h_scr = rest
    else:
        (h_scr,) = rest
    i = pl.program_id(0)

    @pl.when(pl.program_id(1) == 0)
    def _():
        grp = _group_of_row(i * tm)
        y = _rms(x_ref[...], g_ref[...], D_MODEL)
        h = y * (1.0 + sc_ref[pl.ds(grp, 1), :]) + sh_ref[pl.ds(grp, 1), :]
        h_scr[...] = h.astype(BF16)
        if emit_ht:
            ht_ref[...] = h.T.astype(BF16)

    z_ref[...] = jnp.dot(h_scr[...], w_ref[...], preferred_element_type=F32).astype(z_ref.dtype)


def mod_matmul(x, gain, mods, w, layer, shift_chunk, scale_chunk, *, tn, emit_ht, tm=ROW_TILE, out_dtype=F32):
    t = x.shape[0]
    n = w.shape[-1]
    out_shape = [jax.ShapeDtypeStruct((t, n), out_dtype)]
    out_specs = [pl.BlockSpec((tm, tn), lambda i, j: (i, j))]
    if emit_ht:
        out_shape.append(jax.ShapeDtypeStruct((D_MODEL, t), BF16))
        out_specs.append(pl.BlockSpec((D_MODEL, tm), lambda i, j: (0, i)))
    res = pl.pallas_call(
        functools.partial(_mod_matmul_kernel, tm=tm, emit_ht=emit_ht),
        grid=(t // tm, n // tn),
        in_specs=[pl.BlockSpec((tm, D_MODEL), lambda i, j: (i, 0)),
                  pl.BlockSpec((None, 1, D_MODEL), lambda i, j: (layer, 0, 0)),
                  pl.BlockSpec((None, 8, D_MODEL), lambda i, j: (layer, 0, shift_chunk)),
                  pl.BlockSpec((None, 8, D_MODEL), lambda i, j: (layer, 0, scale_chunk)),
                  pl.BlockSpec((None, D_MODEL, tn), lambda i, j: (layer, 0, j))],
        out_specs=out_specs,
        out_shape=out_shape,
        scratch_shapes=[pltpu.VMEM((tm, D_MODEL), BF16)],
        compiler_params=_params(("arbitrary", "arbitrary")),
        name="mod_matmul_ht" if emit_ht else "mod_matmul",
    )(x, gain, mods, mods, w)
    return res


def _rope128(x, c, s1, s2):
    return x * c + pltpu.roll(x, 96, 1) * s1 + pltpu.roll(x, 32, 1) * s2


def _mla_keys_values(ckv_n, kpe, rc, rs1, rs2, wukv_ref, gn_ref, gp_ref, k_ref, v_ref):
    kv = jnp.dot(ckv_n.astype(BF16), wukv_ref[...], preferred_element_type=F32)
    pe_ss = jnp.sum(kpe * kpe, axis=-1, keepdims=True)
    for h in range(MLA_HEADS):
        kn = kv[:, h * MLA_NOPE:(h + 1) * MLA_NOPE]
        ms = (jnp.sum(kn * kn, axis=-1, keepdims=True) + pe_ss) / MLA_QK_DIM
        r = lax.rsqrt(ms + NORM_EPS)
        k_ref[:, h * MLA_HEAD_PAD:h * MLA_HEAD_PAD + LANES] = (kn * r * gn_ref[...]).astype(BF16)
        k_ref[:, h * MLA_HEAD_PAD + LANES:(h + 1) * MLA_HEAD_PAD] = (
            _rope128(kpe * r * gp_ref[...], rc, rs1, rs2).astype(BF16))
    v_ref[...] = kv[:, MLA_HEADS * MLA_NOPE:].astype(BF16)


def _proj_prep_kernel(cq_ref, naq_ref, nak_ref, nav_ref, ckv_ref, rc_ref, rs1_ref, rs2_ref,
                      qng_ref, wuq_ref, qhg_ref, kvng_ref, naqg_ref, nakg_ref, wukv_ref, gn_ref, gp_ref,
                      qmla_ref, ckvn_ref, qna_ref, knaf_ref, knab_ref, vnab_ref, k_ref, v_ref):
    cqn = _rms(cq_ref[...], qng_ref[...], MLA_Q_LORA)
    q = jnp.dot(cqn.astype(BF16), wuq_ref[...], preferred_element_type=F32)
    rc, rs1, rs2 = rc_ref[...], rs1_ref[...], rs2_ref[...]
    for h in range(MLA_HEADS):
        qh = _rms(q[:, h * MLA_HEAD_PAD:(h + 1) * MLA_HEAD_PAD], qhg_ref[...], MLA_QK_DIM)
        qmla_ref[:, h * MLA_HEAD_PAD:h * MLA_HEAD_PAD + LANES] = qh[:, :LANES].astype(BF16)
        qmla_ref[:, h * MLA_HEAD_PAD + LANES:(h + 1) * MLA_HEAD_PAD] = (
            _rope128(qh[:, LANES:], rc, rs1, rs2).astype(BF16))
    ckv_n = _rms(ckv_ref[:, :MLA_KV_LORA], kvng_ref[...], MLA_KV_LORA)
    ckvn_ref[...] = ckv_n
    _mla_keys_values(ckv_n, ckv_ref[:, MLA_KV_LORA:MLA_KV_LORA + LANES], rc, rs1, rs2,
                     wukv_ref, gn_ref, gp_ref, k_ref, v_ref)
    for h in range(NA_HEADS):
        sl = slice(h * NA_HEAD_DIM, (h + 1) * NA_HEAD_DIM)
        qna_ref[:, sl] = _rms(naq_ref[:, sl], naqg_ref[...], NA_HEAD_DIM).astype(BF16)
        kn = _rms(nak_ref[:, sl], nakg_ref[...], NA_HEAD_DIM)
        knaf_ref[:, sl] = kn
        knab_ref[:, sl] = kn.astype(BF16)
    vnab_ref[...] = nav_ref[...].astype(BF16)


def proj_prep(z, rope_c, rope_s1, rope_s2, qng, wuq, qhg, kvng, naqg, nakg, wukv, gn, gp, layer):
    t = z.shape[0]
    tm = SEQ_TILE
    zb = lambda k: pl.BlockSpec((tm, Z_BLOCK), lambda i: (i, k))
    rb = pl.BlockSpec((tm, LANES), lambda i: (i, 0))
    wl = lambda *shape: pl.BlockSpec((None,) + shape, lambda i: (layer,) + (0,) * len(shape))
    ob = lambda w: pl.BlockSpec((tm, w), lambda i: (i, 0))
    return pl.pallas_call(
        _proj_prep_kernel,
        grid=(t // tm,),
        in_specs=[zb(0), zb(1), zb(2), zb(Z_NAV_BLOCK), zb(Z_CKV_BLOCK), rb, rb, rb,
                  wl(1, MLA_Q_LORA), wl(MLA_Q_LORA, MLA_HEADS * MLA_HEAD_PAD), wl(1, MLA_HEAD_PAD),
                  wl(1, MLA_KV_LORA), wl(1, NA_HEAD_DIM), wl(1, NA_HEAD_DIM),
                  wl(MLA_KV_LORA, 2 * MLA_HEADS * MLA_NOPE), wl(1, LANES), wl(1, LANES)],
        out_specs=[ob(MLA_HEADS * MLA_HEAD_PAD), ob(MLA_KV_LORA), ob(NA_WIDTH), ob(NA_WIDTH), ob(NA_WIDTH),
                   ob(NA_WIDTH), ob(MLA_HEADS * MLA_HEAD_PAD), ob(MLA_HEADS * MLA_NOPE)],
        out_shape=[jax.ShapeDtypeStruct((t, MLA_HEADS * MLA_HEAD_PAD), BF16),
                   jax.ShapeDtypeStruct((t, MLA_KV_LORA), F32),
                   jax.ShapeDtypeStruct((t, NA_WIDTH), BF16),
                   jax.ShapeDtypeStruct((t, NA_WIDTH), F32),
                   jax.ShapeDtypeStruct((t, NA_WIDTH), BF16),
                   jax.ShapeDtypeStruct((t, NA_WIDTH), BF16),
                   jax.ShapeDtypeStruct((t, MLA_HEADS * MLA_HEAD_PAD), BF16),
                   jax.ShapeDtypeStruct((t, MLA_HEADS * MLA_NOPE), BF16)],
        compiler_params=_params(("arbitrary",)),
        name="proj_prep",
    )(z, z, z, z, z, rope_c, rope_s1, rope_s2, qng, wuq, qhg, kvng, naqg, nakg, wukv, gn, gp)


def _kv_prep_kernel(ckv_ref, kpe_ref, rc_ref, rs1_ref, rs2_ref, wukv_ref, gn_ref, gp_ref, k_ref, v_ref):
    _mla_keys_values(ckv_ref[...], kpe_ref[...], rc_ref[...], rs1_ref[...], rs2_ref[...],
                     wukv_ref, gn_ref, gp_ref, k_ref, v_ref)


def kv_prep_cache(ckv, kpe, ident_c, ident_s, wukv, gn, gp):
    tm = SEQ_TILE
    rb = lambda w: pl.BlockSpec((None, tm, w), lambda l, i: (l, i, 0))
    tb = pl.BlockSpec((tm, LANES), lambda l, i: (0, 0))
    wl = lambda *shape: pl.BlockSpec((None,) + shape, lambda l, i: (l,) + (0,) * len(shape))
    return pl.pallas_call(
        _kv_prep_kernel,
        grid=(DEPTH, T_CACHE // tm),
        in_specs=[rb(MLA_KV_LORA), rb(LANES), tb, tb, tb,
                  wl(MLA_KV_LORA, 2 * MLA_HEADS * MLA_NOPE), wl(1, LANES), wl(1, LANES)],
        out_specs=[rb(MLA_HEADS * MLA_HEAD_PAD), rb(MLA_HEADS * MLA_NOPE)],
        out_shape=[jax.ShapeDtypeStruct((DEPTH, T_CACHE, MLA_HEADS * MLA_HEAD_PAD), BF16),
                   jax.ShapeDtypeStruct((DEPTH, T_CACHE, MLA_HEADS * MLA_NOPE), BF16)],
        compiler_params=_params(("arbitrary", "arbitrary")),
        name="kv_prep_cache",
    )(ckv, kpe, ident_c, ident_s, ident_s, wukv, gn, gp)


_NT = (((1,), (1,)), ((), ()))


def _attn_kernel(q_ref, k_ref, v_ref, o_ref, *, scale, n_heads, dq):
    dv = NA_HEAD_DIM
    for h in range(n_heads):
        q = q_ref[:, h * dq:(h + 1) * dq]
        k = k_ref[:, h * dq:(h + 1) * dq]
        s = lax.dot_general(q, k, _NT, preferred_element_type=F32) * scale
        m = jnp.max(s, axis=-1, keepdims=True)
        p = jnp.exp(s - m)
        p = p / jnp.sum(p, axis=-1, keepdims=True)
        o = jnp.dot(p.astype(BF16), v_ref[:, h * dv:(h + 1) * dv], preferred_element_type=F32)
        o_ref[:, h * dv:(h + 1) * dv] = o.astype(BF16)


def attention(q, k, v, *, n_seq, n_heads, dq, sq_total, sk, q_row_off, k_row_off, scale):
    sq = ATTN_Q_BLOCK
    nq = sq_total // sq
    qoff = q_row_off // sq
    koff = k_row_off // sk
    dv = NA_HEAD_DIM
    return pl.pallas_call(
        functools.partial(_attn_kernel, scale=scale, n_heads=n_heads, dq=dq),
        grid=(n_seq, nq),
        in_specs=[pl.BlockSpec((sq, n_heads * dq), lambda b, qi: (qoff + b * nq + qi, 0)),
                  pl.BlockSpec((sk, n_heads * dq), lambda b, qi: (koff + b, 0)),
                  pl.BlockSpec((sk, n_heads * dv), lambda b, qi: (koff + b, 0))],
        out_specs=pl.BlockSpec((sq, n_heads * dv), lambda b, qi: (b * nq + qi, 0)),
        out_shape=jax.ShapeDtypeStruct((n_seq * sq_total, n_heads * dv), BF16),
        compiler_params=_params(("arbitrary", "arbitrary")),
        name="attention",
    )(q, k, v)


def _attn_cached_kernel(q_ref, k_ref, v_ref, kc_ref, vc_ref, o_ref, *, scale, n_heads, dq):
    dv = NA_HEAD_DIM
    for h in range(n_heads):
        qk = slice(h * dq, (h + 1) * dq)
        vv = slice(h * dv, (h + 1) * dv)
        q = q_ref[:, qk]
        s_own = lax.dot_general(q, k_ref[:, qk], _NT, preferred_element_type=F32) * scale
        s_ctx = lax.dot_general(q, kc_ref[:, qk], _NT, preferred_element_type=F32) * scale
        m = jnp.maximum(jnp.max(s_own, axis=-1, keepdims=True), jnp.max(s_ctx, axis=-1, keepdims=True))
        p_own = jnp.exp(s_own - m)
        p_ctx = jnp.exp(s_ctx - m)
        denom = jnp.sum(p_own, axis=-1, keepdims=True) + jnp.sum(p_ctx, axis=-1, keepdims=True)
        o = jnp.dot((p_ctx / denom).astype(BF16), vc_ref[:, vv], preferred_element_type=F32)
        o = o + jnp.dot((p_own / denom).astype(BF16), v_ref[:, vv], preferred_element_type=F32)
        o_ref[:, vv] = o.astype(BF16)


def latent_mla_attention(q, k, v, k_cache, v_cache, layer, *, scale):
    sq = ATTN_Q_BLOCK
    nq = LAT_SEQ // sq
    qoff = T_CTX // sq
    koff = T_CTX // LAT_SEQ
    wq = MLA_HEADS * MLA_HEAD_PAD
    wv = MLA_HEADS * NA_HEAD_DIM
    return pl.pallas_call(
        functools.partial(_attn_cached_kernel, scale=scale, n_heads=MLA_HEADS, dq=MLA_HEAD_PAD),
        grid=(N_LAT_SEQ, nq),
        in_specs=[pl.BlockSpec((sq, wq), lambda b, qi: (qoff + b * nq + qi, 0)),
                  pl.BlockSpec((LAT_SEQ, wq), lambda b, qi: (koff + b, 0)),
                  pl.BlockSpec((LAT_SEQ, wv), lambda b, qi: (koff + b, 0)),
                  pl.BlockSpec((None, PAST_LEN, wq), lambda b, qi: (layer, b, 0)),
                  pl.BlockSpec((None, PAST_LEN, wv), lambda b, qi: (layer, b, 0))],
        out_specs=pl.BlockSpec((sq, wv), lambda b, qi: (b * nq + qi, 0)),
        out_shape=jax.ShapeDtypeStruct((T_LAT, wv), BF16),
        compiler_params=_params(("arbitrary", "arbitrary")),
        name="latent_mla",
    )(q, k, v, k_cache, v_cache)


NA_ROWS = LAT_SEQ // GRID_W
NA_LOCAL = NA_KR * GRID_W


def _na_lat_kernel(q_ref, k_ref, v_ref, kc_ref, vc_ref, b_ref, o_ref, *, scale):
    rq = pl.program_id(1)
    row_start = jnp.clip(rq - NA_KR // 2, 0, NA_ROWS - NA_KR)
    start = pl.multiple_of(row_start * GRID_W, GRID_W)
    for h in range(NA_HEADS):
        cols = slice(h * NA_HEAD_DIM, (h + 1) * NA_HEAD_DIM)
        q = q_ref[:, cols]
        k_loc = k_ref[pl.ds(start, NA_LOCAL), cols]
        v_loc = v_ref[pl.ds(start, NA_LOCAL), cols]
        s_loc = lax.dot_general(q, k_loc, _NT, preferred_element_type=F32) * scale + b_ref[h, rq - row_start]
        s_ctx = lax.dot_general(q, kc_ref[:, cols], _NT, preferred_element_type=F32) * scale
        m = jnp.maximum(jnp.max(s_loc, axis=-1, keepdims=True), jnp.max(s_ctx, axis=-1, keepdims=True))
        p_loc = jnp.exp(s_loc - m)
        p_ctx = jnp.exp(s_ctx - m)
        denom = jnp.sum(p_loc, axis=-1, keepdims=True) + jnp.sum(p_ctx, axis=-1, keepdims=True)
        o = jnp.dot((p_loc / denom).astype(BF16), v_loc, preferred_element_type=F32)
        o = o + jnp.dot((p_ctx / denom).astype(BF16), vc_ref[:, cols], preferred_element_type=F32)
        o_ref[:, cols] = o.astype(BF16)


def na_latent_attention(q, k, v, k_cache, v_cache, bias_tab, layer, *, scale):
    qoff = T_CTX // GRID_W
    koff = T_CTX // LAT_SEQ
    w = NA_WIDTH
    return pl.pallas_call(
        functools.partial(_na_lat_kernel, scale=scale),
        grid=(N_LAT_SEQ, NA_ROWS),
        in_specs=[pl.BlockSpec((GRID_W, w), lambda b, r: (qoff + b * NA_ROWS + r, 0)),
                  pl.BlockSpec((LAT_SEQ, w), lambda b, r: (koff + b, 0)),
                  pl.BlockSpec((LAT_SEQ, w), lambda b, r: (koff + b, 0)),
                  pl.BlockSpec((None, None, PAST_LEN, w), lambda b, r: (layer, b, 0, 0)),
                  pl.BlockSpec((None, None, PAST_LEN, w), lambda b, r: (layer, b, 0, 0)),
                  pl.BlockSpec((None, NA_HEADS, NA_KR, GRID_W, NA_LOCAL), lambda b, r: (layer, 0, 0, 0, 0))],
        out_specs=pl.BlockSpec((GRID_W, w), lambda b, r: (b * NA_ROWS + r, 0)),
        out_shape=jax.ShapeDtypeStruct((T_LAT, NA_WIDTH), BF16),
        compiler_params=_params(("arbitrary", "arbitrary")),
        name="na_latent",
    )(q, k, v, k_cache, v_cache, bias_tab)


def _conv_kernel(gb_ref, gc_ref, u_ref, w_ref, o_ref):
    gu = gc_ref[...] * u_ref[...]
    s = gu.shape[0]
    row = lax.broadcasted_iota(jnp.int32, gu.shape, 0)
    prev = jnp.where(row == 0, 0.0, pltpu.roll(gu, 1, 0))
    nxt = jnp.where(row == s - 1, 0.0, pltpu.roll(gu, s - 1, 0))
    y = prev * w_ref[0:1, :] + gu * w_ref[1:2, :] + nxt * w_ref[2:3, :]
    o_ref[...] = (gb_ref[...] * y).astype(BF16)


def short_conv(z, conv_w8, layer, *, n_seq, seq, row_off):
    off = row_off // seq
    zb = lambda k: pl.BlockSpec((seq, CONV_CH), lambda i: (off + i, k))
    return pl.pallas_call(
        _conv_kernel,
        grid=(n_seq,),
        in_specs=[zb(Z_CONV_BLOCK), zb(Z_CONV_BLOCK + 1), zb(Z_CONV_BLOCK + 2),
                  pl.BlockSpec((None, 8, CONV_CH), lambda i: (layer, 0, 0))],
        out_specs=pl.BlockSpec((seq, CONV_CH), lambda i: (i, 0)),
        out_shape=jax.ShapeDtypeStruct((n_seq * seq, CONV_CH), BF16),
        compiler_params=_params(("arbitrary",)),
        name="short_conv",
    )(z, z, z, conv_w8)


def _out_matmul_kernel(x_ref, g_ref, a0c_ref, a0l_ref, a1c_ref, a1l_ref, a2c_ref, a2l_ref,
                       w0_ref, w1_ref, w2_ref, o_ref, *, tm):
    row = pl.program_id(0) * tm
    grp = _group_of_row(row)
    is_ctx = row < T_CTX
    acc = jnp.dot(jnp.where(is_ctx, a0c_ref[...], a0l_ref[...]), w0_ref[...], preferred_element_type=F32)
    acc = acc + jnp.dot(jnp.where(is_ctx, a1c_ref[...], a1l_ref[...]), w1_ref[...], preferred_element_type=F32)
    acc = acc + jnp.dot(jnp.where(is_ctx, a2c_ref[...], a2l_ref[...]), w2_ref[...], preferred_element_type=F32)
    o_ref[...] = x_ref[...] + g_ref[pl.ds(grp, 1), :] * acc


def out_matmul(x, mods, o_mla, o_na, conv, w_out, layer, gate_chunk):
    t = x.shape[0]
    tm, tn = ROW_TILE, 1024
    nj = D_MODEL // tn
    w_mla = MLA_HEADS * MLA_NOPE
    n_ctx = T_CTX // tm
    ctx = lambda w: pl.BlockSpec((tm, w), lambda i, j: (jnp.minimum(i, n_ctx - 1), 0))
    lat = lambda w: pl.BlockSpec((tm, w), lambda i, j: (jnp.maximum(i - n_ctx, 0), 0))
    return pl.pallas_call(
        functools.partial(_out_matmul_kernel, tm=tm),
        grid=(t // tm, nj),
        in_specs=[pl.BlockSpec((tm, tn), lambda i, j: (i, j)),
                  pl.BlockSpec((None, 8, tn), lambda i, j: (layer, 0, gate_chunk * nj + j)),
                  ctx(w_mla), lat(w_mla), ctx(NA_WIDTH), lat(NA_WIDTH), ctx(CONV_CH), lat(CONV_CH),
                  pl.BlockSpec((None, w_mla, tn), lambda i, j: (layer, 0, j)),
                  pl.BlockSpec((None, NA_WIDTH, tn), lambda i, j: (layer, w_mla // NA_WIDTH, j)),
                  pl.BlockSpec((None, CONV_CH, tn), lambda i, j: (layer, (w_mla + NA_WIDTH) // CONV_CH, j))],
        out_specs=pl.BlockSpec((tm, tn), lambda i, j: (i, j)),
        out_shape=jax.ShapeDtypeStruct((t, D_MODEL), F32),
        compiler_params=_params(("arbitrary", "arbitrary")),
        name="out_matmul",
    )(x, mods, *o_mla, *o_na, *conv, w_out, w_out, w_out)


def _argmax_step(s, pos, big):
    m = jnp.max(s, axis=0, keepdims=True)
    first = jnp.min(jnp.where(s == m, pos, big), axis=0, keepdims=True)
    return m, first, pos == first


def _half_ranks(s1, s2):
    n, width = s1.shape
    pos = lax.broadcasted_iota(jnp.int32, (n, width), 0).astype(F32)
    iota16 = lax.broadcasted_iota(jnp.int32, (PEER_TOPK, width), 0)

    def body(j, carry):
        s1c, v1c, i1c, s2c, v2c, r2c = carry
        m1, first1, hit1 = _argmax_step(s1c, pos, float(n))
        m2, _, hit2 = _argmax_step(s2c, pos, float(n))
        return (jnp.where(hit1, -jnp.inf, s1c), jnp.where(iota16 == j, m1, v1c), jnp.where(iota16 == j, first1, i1c),
                jnp.where(hit2, -jnp.inf, s2c), jnp.where(iota16 == j, m2, v2c), jnp.where(hit2, j, r2c))

    zero16 = jnp.zeros((PEER_TOPK, width), F32)
    rank0 = jnp.full((n, width), float(PEER_TOPK), F32)
    _, v1, i1, _, v2, r2 = lax.fori_loop(0, PEER_TOPK, body, (s1, zero16, zero16, s2, zero16, rank0), unroll=True)
    return pos, v1, i1, v2, r2


_CAND_ROWS = 16 + 7 * 8 + 8


def _cand_positions():
    p = [j2 for j2 in range(16)]
    p += [j1 * 16 + j2 for j1 in range(1, 8) for j2 in range(8)]
    p += [j1 * 16 for j1 in range(8, 16)]
    return np.tile(np.asarray(p, np.float32)[:, None], (1, LANES))


def _staircase(v1, v2, cpos):
    cand = jnp.concatenate([v1[0:1] + v2] + [v1[j:j + 1] + v2[0:8] for j in range(1, 8)] + [v1[8:16] + v2[0:1]],
                           axis=0)
    cmax = cand[0:1]

    def body(j, carry):
        s, sel = carry
        _, _, hit = _argmax_step(s, cpos, float(PEER_TOPK * PEER_TOPK))
        return jnp.where(hit, -jnp.inf, s), jnp.where(hit, 1.0, sel)

    _, sel = lax.fori_loop(0, PEER_TOPK, body, (cand, jnp.zeros_like(cand)), unroll=True)
    z = jnp.sum(sel * jnp.exp(cand - cmax), axis=0, keepdims=True)
    iota8 = lax.broadcasted_iota(jnp.int32, (8, cand.shape[1]), 0)
    low = jnp.zeros((8, cand.shape[1]), F32)
    low = jnp.where(iota8 == 0, jnp.sum(sel[0:16], axis=0, keepdims=True), low)
    for j in range(1, 8):
        low = jnp.where(iota8 == j, jnp.sum(sel[8 + 8 * j:16 + 8 * j], axis=0, keepdims=True), low)
    return jnp.concatenate([low, sel[72:80]], axis=0), z


def _peer_topk_kernel(q_ref, keys_ref, cpos_ref, e1_ref, lb1_ref, r2_ref, e2_ref, *, heads):
    nt = (((1,), (1,)), ((), ()))
    for h in range(heads):
        q = q_ref[:, h * 2 * LANES:(h + 1) * 2 * LANES]
        s1 = lax.dot_general(keys_ref[h, 0], q[:, :LANES], nt, preferred_element_type=F32)
        s2 = lax.dot_general(keys_ref[h, 1], q[:, LANES:], nt, preferred_element_type=F32)
        pos, v1, i1, v2, r2 = _half_ranks(s1, s2)
        counts, z = _staircase(v1, v2, cpos_ref[...])
        lb1 = jnp.zeros_like(s1)
        for j in range(PEER_TOPK):
            lb1 = jnp.where(pos == i1[j:j + 1], counts[j:j + 1], lb1)
        e1_ref[h] = jnp.exp(s1 - v1[0:1])
        lb1_ref[h] = lb1
        r2_ref[h] = r2
        e2_ref[h] = jnp.exp(s2 - v2[0:1]) / z


def peer_topk(q, sub_keys, layer):
    t = q.shape[0]
    heads = PEER_HEADS
    ob = pl.BlockSpec((heads, PEER_N_KEYS, LANES), lambda i, h: (h, 0, i))
    shp = jax.ShapeDtypeStruct((PEER_HEADS, PEER_N_KEYS, t), F32)
    return pl.pallas_call(
        functools.partial(_peer_topk_kernel, heads=heads),
        grid=(t // LANES, PEER_HEADS // heads),
        in_specs=[pl.BlockSpec((LANES, heads * 2 * LANES), lambda i, h: (i, h)),
                  pl.BlockSpec((None, heads, 2, PEER_N_KEYS, LANES), lambda i, h: (layer, h, 0, 0, 0)),
                  pl.BlockSpec((_CAND_ROWS, LANES), lambda i, h: (0, 0))],
        out_specs=[ob, ob, ob, ob],
        out_shape=[shp, shp, shp, shp],
        compiler_params=_params(("arbitrary", "arbitrary")),
        name="peer_topk",
    )(q, sub_keys, jnp.asarray(_cand_positions()))


_SQRT_HALF = float(np.sqrt(0.5))


def _peer_dense_kernel(ht_ref, u_ref, v_ref, e1_ref, lb1_ref, r2_ref, e2_ref, x_ref, g_ref, o_ref,
                       acc_ref, at_ref, wg_ref, *, tt, ec):
    i = pl.program_id(0)
    c = pl.program_id(1)

    @pl.when(c == 0)
    def _():
        acc_ref[...] = jnp.zeros_like(acc_ref)

    at_ref[...] = jnp.dot(u_ref[...], ht_ref[...], preferred_element_type=F32)
    n_a = ec // PEER_N_KEYS
    grp0 = pl.multiple_of((c * n_a) // 8 * 8, 8)
    off = (c * n_a) % 8

    def key_row(ref, h, al, lanes):
        blk = ref[h, pl.ds(grp0, 8), lanes]
        row = blk[al:al + 1]
        for o in range(n_a, 8, n_a):
            row = jnp.where(off == o, blk[o + al:o + al + 1], row)
        return row

    for al in range(n_a):
        rows = slice(al * PEER_N_KEYS, (al + 1) * PEER_N_KEYS)
        for lt in range(tt // LANES):
            lanes = slice(lt * LANES, (lt + 1) * LANES)
            gate = jnp.zeros((PEER_N_KEYS, LANES), F32)
            for h in range(PEER_HEADS):
                lb = key_row(lb1_ref, h, al, lanes)
                e1 = key_row(e1_ref, h, al, lanes)
                gate = gate + jnp.where(r2_ref[h, :, lanes] < lb, e2_ref[h, :, lanes], 0.0) * e1
            act = at_ref[rows, lanes]
            wg_ref[rows, lanes] = 0.5 * act * (1.0 + lax.erf(act * _SQRT_HALF)) * gate
    acc_ref[...] += jnp.dot(wg_ref[...].T.astype(BF16), v_ref[...], preferred_element_type=F32)

    @pl.when(c == pl.num_programs(1) - 1)
    def _():
        grp = _group_of_row(i * tt)
        o_ref[...] = x_ref[...] + g_ref[pl.ds(grp, 1), :] * acc_ref[...]


def peer_dense(ht, peer_u, peer_v, e1, lb1, r2, e2, x, mods, layer, gate_chunk):
    t = x.shape[0]
    tt, ec = PEER_TOK_TILE, PEER_EXP_TILE
    once = pl.Buffered(1)
    kb = pl.BlockSpec((PEER_HEADS, PEER_N_KEYS, tt), lambda i, c: (0, 0, i), pipeline_mode=once)
    return pl.pallas_call(
        functools.partial(_peer_dense_kernel, tt=tt, ec=ec),
        grid=(t // tt, PEER_N_EXPERTS // ec),
        in_specs=[pl.BlockSpec((D_MODEL, tt), lambda i, c: (0, i)),
                  pl.BlockSpec((None, ec, D_MODEL), lambda i, c: (layer, c, 0)),
                  pl.BlockSpec((None, ec, D_MODEL), lambda i, c: (layer, c, 0)),
                  kb, kb, kb, kb,
                  pl.BlockSpec((tt, D_MODEL), lambda i, c: (i, 0), pipeline_mode=once),
                  pl.BlockSpec((None, 8, D_MODEL), lambda i, c: (layer, 0, gate_chunk))],
        out_specs=pl.BlockSpec((tt, D_MODEL), lambda i, c: (i, 0)),
        out_shape=jax.ShapeDtypeStruct((t, D_MODEL), F32),
        scratch_shapes=[pltpu.VMEM((tt, D_MODEL), F32),
                        pltpu.VMEM((ec, tt), F32),
                        pltpu.VMEM((ec, tt), F32)],
        compiler_params=_params(("arbitrary", "arbitrary")),
        name="peer_dense",
    )(ht, peer_u, peer_v, e1, lb1, r2, e2, x, mods)


def _rope_tables():
    t = jnp.arange(LAT_SEQ)
    row = (t // GRID_W).astype(F32)
    col = (t % GRID_W).astype(F32)
    n_freq = MLA_ROPE // 4
    inv = ROPE_THETA ** (-jnp.arange(n_freq, dtype=F32) / n_freq)
    ang = jnp.concatenate([row[:, None] * inv, col[:, None] * inv], axis=-1)
    cos, sin = jnp.cos(ang), jnp.sin(ang)
    zero = jnp.zeros_like(cos)
    pad = jnp.zeros((LAT_SEQ, LANES - MLA_ROPE), F32)
    c_lat = jnp.concatenate([cos, cos, pad], axis=-1)
    s1_lat = jnp.concatenate([-sin, zero, pad], axis=-1)
    s2_lat = jnp.concatenate([zero, sin, pad], axis=-1)
    ones = jnp.concatenate([jnp.ones((1, MLA_ROPE), F32), jnp.zeros((1, LANES - MLA_ROPE), F32)], axis=-1)

    def full(lat, ident):
        return jnp.concatenate([jnp.broadcast_to(ident, (T_CTX, LANES)), jnp.tile(lat, (N_LAT_SEQ, 1)),
                                jnp.broadcast_to(ident, (T_CACHE, LANES))], axis=0)

    zeros = jnp.zeros((1, LANES), F32)
    return full(c_lat, ones), full(s1_lat, zeros), full(s2_lat, zeros)


def _na_bias_table(rel_bias):
    cq = np.arange(GRID_W)
    col_start = np.clip(cq - NA_KC // 2, 0, GRID_W - NA_KC)
    valid = (cq[None, :] >= col_start[:, None]) & (cq[None, :] < col_start[:, None] + NA_KC)
    coff = np.clip(cq[None, :] - cq[:, None], -(NA_KC - 1), NA_KC - 1) + (NA_KC - 1)
    onehot = (coff[:, :, None] == np.arange(2 * NA_KC - 1)[None, None, :]).astype(np.float32)
    toep = jnp.einsum('lhrc,qkc->lhrqk', rel_bias, jnp.asarray(onehot), precision=lax.Precision.HIGHEST)
    toep = jnp.where(jnp.asarray(valid)[None, None, None], toep, -jnp.inf)
    tabs = []
    for d in range(NA_KR):
        rows = toep[:, :, NA_KR - 1 - d:2 * NA_KR - 1 - d]
        tabs.append(rows.transpose(0, 1, 3, 2, 4).reshape(DEPTH, NA_HEADS, GRID_W, NA_LOCAL))
    return jnp.stack(tabs, axis=2)


def kernel(x_prompt, x_sample, cache_mla_ckv, cache_mla_kpe, cache_na_k, cache_na_v, c, c_ctx, ada_w, ada_b, norm_mix_g, norm_ffn_g, w_in, mla_q_norm_g, mla_w_uq, mla_kv_norm_g, mla_w_ukv, mla_q_head_g, mla_k_head_g, na_q_head_g, na_k_head_g, na_rel_bias, conv_w, w_out, peer_w_q, peer_sub_keys, peer_u, peer_v):
    lat_end = MLA_Q_LORA + MLA_KV_LORA + MLA_ROPE
    w_in_p = jnp.concatenate([w_in[..., :MLA_Q_LORA], w_in[..., lat_end:], w_in[..., MLA_Q_LORA:lat_end],
                              jnp.zeros((DEPTH, D_MODEL, D_IN_PAD - D_IN), F32)], axis=-1).astype(BF16)
    wuq_p = jnp.pad(mla_w_uq.reshape(DEPTH, MLA_Q_LORA, MLA_HEADS, MLA_QK_DIM),
                    ((0, 0), (0, 0), (0, 0), (0, MLA_HEAD_PAD - MLA_QK_DIM))
                    ).reshape(DEPTH, MLA_Q_LORA, MLA_HEADS * MLA_HEAD_PAD).astype(BF16)
    qhg_p = jnp.pad(mla_q_head_g, ((0, 0), (0, MLA_HEAD_PAD - MLA_QK_DIM))).reshape(DEPTH, 1, MLA_HEAD_PAD)
    wukv4 = mla_w_ukv.reshape(DEPTH, MLA_KV_LORA, MLA_HEADS, 2 * MLA_NOPE)
    wukv_p = jnp.concatenate([wukv4[..., :MLA_NOPE].reshape(DEPTH, MLA_KV_LORA, -1),
                              wukv4[..., MLA_NOPE:].reshape(DEPTH, MLA_KV_LORA, -1)], axis=-1).astype(BF16)
    khg_n = mla_k_head_g[:, :MLA_NOPE].reshape(DEPTH, 1, LANES)
    khg_p = jnp.pad(mla_k_head_g[:, MLA_NOPE:], ((0, 0), (0, LANES - MLA_ROPE))).reshape(DEPTH, 1, LANES)
    conv_w8 = jnp.pad(conv_w.transpose(0, 2, 1), ((0, 0), (0, 5), (0, 0)))
    w_out_b = w_out.astype(BF16)
    peer_wq_b = peer_w_q.astype(BF16)
    sub_keys_b = peer_sub_keys.astype(BF16)
    peer_u_b = peer_u.astype(BF16)
    peer_v_b = peer_v.astype(BF16)
    g_mix = norm_mix_g.reshape(DEPTH, 1, D_MODEL)
    g_ffn = norm_ffn_g.reshape(DEPTH, 1, D_MODEL)
    qng = mla_q_norm_g.reshape(DEPTH, 1, MLA_Q_LORA)
    kvng = mla_kv_norm_g.reshape(DEPTH, 1, MLA_KV_LORA)
    naqg = na_q_head_g.reshape(DEPTH, 1, NA_HEAD_DIM)
    nakg = na_k_head_g.reshape(DEPTH, 1, NA_HEAD_DIM)
    rope_c, rope_s1, rope_s2 = _rope_tables()
    na_bias_tab = _na_bias_table(na_rel_bias)
    cache_kpe_p = jnp.pad(cache_mla_kpe, ((0, 0), (0, 0), (0, 0), (0, LANES - MLA_ROPE)))
    cache_nak = cache_na_k.transpose(1, 0, 3, 2, 4).reshape(DEPTH, N_LAT_SEQ, PAST_LEN, NA_WIDTH).astype(BF16)
    cache_nav = cache_na_v.transpose(1, 0, 3, 2, 4).reshape(DEPTH, N_LAT_SEQ, PAST_LEN, NA_WIDTH).astype(BF16)

    cpad = jnp.concatenate([c_ctx[None, :], c, jnp.zeros((8 - 1 - N_LAT_SEQ, D_MODEL), F32)], axis=0)
    mods = ada_all(cpad, ada_w, ada_b)

    x = jnp.concatenate([x_prompt.reshape(T_CTX, D_MODEL), x_sample.reshape(T_LAT, D_MODEL)], axis=0)
    ckv_out, kpe_out, nak_out, nav_out = [], [], [], []
    mla_scale = MLA_QK_DIM ** -0.5
    na_scale = NA_HEAD_DIM ** -0.5

    ident_c = jnp.broadcast_to(rope_c[:1], (SEQ_TILE, LANES))
    ident_s = jnp.zeros((SEQ_TILE, LANES), F32)
    k_cache, v_cache = kv_prep_cache(
        cache_mla_ckv.transpose(1, 0, 2, 3).reshape(DEPTH, T_CACHE, MLA_KV_LORA),
        cache_kpe_p.transpose(1, 0, 2, 3).reshape(DEPTH, T_CACHE, LANES), ident_c, ident_s, wukv_p, khg_n, khg_p)

    for l in range(DEPTH):
        (z,) = mod_matmul(x, g_mix, mods, w_in_p, l, 0, 1, tn=1024, emit_ht=False)
        q_mla, ckv_n, q_na, k_na_f, k_na_b, v_na_b, k_mla, v_mla = proj_prep(
            z, rope_c, rope_s1, rope_s2, qng, wuq_p, qhg_p, kvng, naqg, nakg, wukv_p, khg_n, khg_p, l)

        o_mla_ctx = attention(q_mla, k_mla, v_mla, n_seq=N_CTX_SEQ, n_heads=MLA_HEADS, dq=MLA_HEAD_PAD,
                              sq_total=CTX_SEQ, sk=CTX_SEQ, q_row_off=0, k_row_off=0, scale=mla_scale)
        o_mla_lat = latent_mla_attention(q_mla, k_mla, v_mla, k_cache, v_cache, l, scale=mla_scale)
        o_na_ctx = attention(q_na, k_na_b, v_na_b, n_seq=N_CTX_SEQ, n_heads=NA_HEADS, dq=NA_HEAD_DIM,
                             sq_total=CTX_SEQ, sk=CTX_SEQ, q_row_off=0, k_row_off=0, scale=na_scale)
        o_na_lat = na_latent_attention(q_na, k_na_b, v_na_b, cache_nak, cache_nav, na_bias_tab, l, scale=na_scale)
        conv_ctx = short_conv(z, conv_w8, l, n_seq=N_CTX_SEQ, seq=CTX_SEQ, row_off=0)
        conv_lat = short_conv(z, conv_w8, l, n_seq=N_LAT_SEQ, seq=LAT_SEQ, row_off=T_CTX)

        x = out_matmul(x, mods, (o_mla_ctx, o_mla_lat), (o_na_ctx, o_na_lat), (conv_ctx, conv_lat), w_out_b, l, 2)

        q_peer, ht = mod_matmul(x, g_ffn, mods, peer_wq_b, l, 3, 4, tn=D_MODEL, emit_ht=True, out_dtype=BF16)
        e1, lb1, r2, e2 = peer_topk(q_peer, sub_keys_b, l)
        x = peer_dense(ht, peer_u_b, peer_v_b, e1, lb1, r2, e2, x, mods, l, 5)

        ckv_out.append(ckv_n[:T_CTX].reshape(N_CTX_SEQ, CTX_SEQ, MLA_KV_LORA))
        kpe_out.append(z[:T_CTX, Z_KPE_COL:Z_KPE_COL + MLA_ROPE].reshape(N_CTX_SEQ, CTX_SEQ, MLA_ROPE))
        nak_out.append(k_na_f[:T_CTX].reshape(N_CTX_SEQ, CTX_SEQ, NA_HEADS, NA_HEAD_DIM).transpose(0, 2, 1, 3))
        nav_out.append(z[:T_CTX, Z_NAV_BLOCK * Z_BLOCK:(Z_NAV_BLOCK + 1) * Z_BLOCK].reshape(
            N_CTX_SEQ, CTX_SEQ, NA_HEADS, NA_HEAD_DIM).transpose(0, 2, 1, 3))

    y_prompt = x[:T_CTX].reshape(N_CTX_SEQ, CTX_SEQ, D_MODEL)
    y_sample = x[T_CTX:].reshape(N_LAT_SEQ, LAT_SEQ, D_MODEL)
    return (y_prompt, y_sample, jnp.stack(ckv_out, axis=1), jnp.stack(kpe_out, axis=1),
            jnp.stack(nak_out, axis=1), jnp.stack(nav_out, axis=1))
```

```python
import functools

import numpy as np
import jax
import jax.numpy as jnp
from jax import lax
from jax.experimental import pallas as pl
from jax.experimental.pallas import tpu as pltpu

F32 = jnp.float32
BF16 = jnp.bfloat16

D_MODEL = 2048
DEPTH = 4
N_CTX_SEQ = 32
CTX_SEQ = 256
N_LAT_SEQ = 2
LAT_SEQ = 1024
PAST_LEN = 512
T_CTX = N_CTX_SEQ * CTX_SEQ
T_LAT = N_LAT_SEQ * LAT_SEQ
T_ALL = T_CTX + T_LAT
T_CACHE = N_LAT_SEQ * PAST_LEN
GRID_W = 64
NORM_EPS = 1e-6
ROPE_THETA = 10000.0
ADA_CHUNKS = 6

MLA_HEADS = 8
MLA_NOPE = 128
MLA_ROPE = 64
MLA_QK_DIM = MLA_NOPE + MLA_ROPE
MLA_Q_LORA = 512
MLA_KV_LORA = 256
MLA_HEAD_PAD = 256
NA_HEADS = 4
NA_HEAD_DIM = 128
NA_WIDTH = NA_HEADS * NA_HEAD_DIM
NA_KR = 8
NA_KC = 16
CONV_CH = 512
Z_BLOCK = 512
Z_NAV_BLOCK = 3
Z_CONV_BLOCK = 4
Z_CKV_BLOCK = 7
D_IN_PAD = (Z_CKV_BLOCK + 1) * Z_BLOCK
Z_KPE_COL = Z_CKV_BLOCK * Z_BLOCK + MLA_KV_LORA
D_IN = MLA_Q_LORA + MLA_KV_LORA + MLA_ROPE + 3 * NA_WIDTH + 3 * CONV_CH
ATTN_Q_BLOCK = 256

PEER_HEADS = 8
PEER_N_KEYS = 128
PEER_N_EXPERTS = PEER_N_KEYS * PEER_N_KEYS
PEER_TOPK = 16

LANES = 128
VMEM_LIMIT = 60 * 1024 * 1024

ROW_TILE = 1024
SEQ_TILE = 256
PEER_TOK_TILE = 512
PEER_EXP_TILE = 1024


def _params(sem, flags=None):
    return pltpu.CompilerParams(dimension_semantics=sem, vmem_limit_bytes=VMEM_LIMIT, flags=flags)


def _group_of_row(row):
    return jnp.where(row < T_CTX, 0, 1 + (row - T_CTX) // LAT_SEQ)


def _rms(x, g, n):
    ms = jnp.sum(x * x, axis=-1, keepdims=True) / n
    return x * lax.rsqrt(ms + NORM_EPS) * g


def _ada_kernel(c_ref, w_ref, b_ref, o_ref):
    c = c_ref[...]
    s = c * jax.nn.sigmoid(c)
    o_ref[...] = jnp.dot(s.astype(BF16), w_ref[...].astype(BF16), preferred_element_type=F32) + b_ref[...]


def ada_all(cpad, ada_w, ada_b):
    tn = 1536
    n = ADA_CHUNKS * D_MODEL
    return pl.pallas_call(
        _ada_kernel,
        grid=(DEPTH, n // tn),
        in_specs=[pl.BlockSpec((8, D_MODEL), lambda l, j: (0, 0)),
                  pl.BlockSpec((None, D_MODEL, tn), lambda l, j: (l, 0, j)),
                  pl.BlockSpec((None, 1, tn), lambda l, j: (l, 0, j))],
        out_specs=pl.BlockSpec((None, 8, tn), lambda l, j: (l, 0, j)),
        out_shape=jax.ShapeDtypeStruct((DEPTH, 8, n), F32),
        compiler_params=_params(("arbitrary", "arbitrary")),
        name="ada",
    )(cpad, ada_w, ada_b.reshape(DEPTH, 1, n))


def _mod_matmul_kernel(x_ref, g_ref, sh_ref, sc_ref, w_ref, z_ref, *rest, tm, emit_ht):
    if emit_ht:
        ht_ref, h_scr = rest
    else:
        (h_scr,) = rest
    i = pl.program_id(0)

    @pl.when(pl.program_id(1) == 0)
    def _():
        grp = _group_of_row(i * tm)
        y = _rms(x_ref[...], g_ref[...], D_MODEL)
        h = y * (1.0 + sc_ref[pl.ds(grp, 1), :]) + sh_ref[pl.ds(grp, 1), :]
        h_scr[...] = h.astype(BF16)
        if emit_ht:
            ht_ref[...] = h.T.astype(BF16)

    z_ref[...] = jnp.dot(h_scr[...], w_ref[...], preferred_element_type=F32).astype(z_ref.dtype)


def mod_matmul(x, gain, mods, w, layer, shift_chunk, scale_chunk, *, tn, emit_ht, tm=ROW_TILE, out_dtype=F32):
    t = x.shape[0]
    n = w.shape[-1]
    out_shape = [jax.ShapeDtypeStruct((t, n), out_dtype)]
    out_specs = [pl.BlockSpec((tm, tn), lambda i, j: (i, j))]
    if emit_ht:
        out_shape.append(jax.ShapeDtypeStruct((D_MODEL, t), BF16))
        out_specs.append(pl.BlockSpec((D_MODEL, tm), lambda i, j: (0, i)))
    res = pl.pallas_call(
        functools.partial(_mod_matmul_kernel, tm=tm, emit_ht=emit_ht),
        grid=(t // tm, n // tn),
        in_specs=[pl.BlockSpec((tm, D_MODEL), lambda i, j: (i, 0)),
                  pl.BlockSpec((None, 1, D_MODEL), lambda i, j: (layer, 0, 0)),
                  pl.BlockSpec((None, 8, D_MODEL), lambda i, j: (layer, 0, shift_chunk)),
                  pl.BlockSpec((None, 8, D_MODEL), lambda i, j: (layer, 0, scale_chunk)),
                  pl.BlockSpec((None, D_MODEL, tn), lambda i, j: (layer, 0, j))],
        out_specs=out_specs,
        out_shape=out_shape,
        scratch_shapes=[pltpu.VMEM((tm, D_MODEL), BF16)],
        compiler_params=_params(("arbitrary", "arbitrary")),
        name="mod_matmul_ht" if emit_ht else "mod_matmul",
    )(x, gain, mods, mods, w)
    return res


def _rope128(x, c, s1, s2):
    return x * c + pltpu.roll(x, 96, 1) * s1 + pltpu.roll(x, 32, 1) * s2


def _mla_keys_values(ckv_n, kpe, rc, rs1, rs2, wukv_ref, gn_ref, gp_ref, k_ref, v_ref):
    kv = jnp.dot(ckv_n.astype(BF16), wukv_ref[...], preferred_element_type=F32)
    pe_ss = jnp.sum(kpe * kpe, axis=-1, keepdims=True)
    for h in range(MLA_HEADS):
        kn = kv[:, h * MLA_NOPE:(h + 1) * MLA_NOPE]
        ms = (jnp.sum(kn * kn, axis=-1, keepdims=True) + pe_ss) / MLA_QK_DIM
        r = lax.rsqrt(ms + NORM_EPS)
        k_ref[:, h * MLA_HEAD_PAD:h * MLA_HEAD_PAD + LANES] = (kn * r * gn_ref[...]).astype(BF16)
        k_ref[:, h * MLA_HEAD_PAD + LANES:(h + 1) * MLA_HEAD_PAD] = (
            _rope128(kpe * r * gp_ref[...], rc, rs1, rs2).astype(BF16))
    v_ref[...] = kv[:, MLA_HEADS * MLA_NOPE:].astype(BF16)


def _proj_prep_kernel(cq_ref, naq_ref, nak_ref, nav_ref, ckv_ref, rc_ref, rs1_ref, rs2_ref,
                      qng_ref, wuq_ref, qhg_ref, kvng_ref, naqg_ref, nakg_ref, wukv_ref, gn_ref, gp_ref,
                      qmla_ref, ckvn_ref, qna_ref, knaf_ref, knab_ref, vnab_ref, k_ref, v_ref):
    cqn = _rms(cq_ref[...], qng_ref[...], MLA_Q_LORA)
    q = jnp.dot(cqn.astype(BF16), wuq_ref[...], preferred_element_type=F32)
    rc, rs1, rs2 = rc_ref[...], rs1_ref[...], rs2_ref[...]
    for h in range(MLA_HEADS):
        qh = _rms(q[:, h * MLA_HEAD_PAD:(h + 1) * MLA_HEAD_PAD], qhg_ref[...], MLA_QK_DIM)
        qmla_ref[:, h * MLA_HEAD_PAD:h * MLA_HEAD_PAD + LANES] = qh[:, :LANES].astype(BF16)
        qmla_ref[:, h * MLA_HEAD_PAD + LANES:(h + 1) * MLA_HEAD_PAD] = (
            _rope128(qh[:, LANES:], rc, rs1, rs2).astype(BF16))
    ckv_n = _rms(ckv_ref[:, :MLA_KV_LORA], kvng_ref[...], MLA_KV_LORA)
    ckvn_ref[...] = ckv_n
    _mla_keys_values(ckv_n, ckv_ref[:, MLA_KV_LORA:MLA_KV_LORA + LANES], rc, rs1, rs2,
                     wukv_ref, gn_ref, gp_ref, k_ref, v_ref)
    for h in range(NA_HEADS):
        sl = slice(h * NA_HEAD_DIM, (h + 1) * NA_HEAD_DIM)
        qna_ref[:, sl] = _rms(naq_ref[:, sl], naqg_ref[...], NA_HEAD_DIM).astype(BF16)
        kn = _rms(nak_ref[:, sl], nakg_ref[...], NA_HEAD_DIM)
        knaf_ref[:, sl] = kn
        knab_ref[:, sl] = kn.astype(BF16)
    vnab_ref[...] = nav_ref[...].astype(BF16)


def proj_prep(z, rope_c, rope_s1, rope_s2, qng, wuq, qhg, kvng, naqg, nakg, wukv, gn, gp, layer):
    t = z.shape[0]
    tm = SEQ_TILE
    zb = lambda k: pl.BlockSpec((tm, Z_BLOCK), lambda i: (i, k))
    rb = pl.BlockSpec((tm, LANES), lambda i: (i, 0))
    wl = lambda *shape: pl.BlockSpec((None,) + shape, lambda i: (layer,) + (0,) * len(shape))
    ob = lambda w: pl.BlockSpec((tm, w), lambda i: (i, 0))
    return pl.pallas_call(
        _proj_prep_kernel,
        grid=(t // tm,),
        in_specs=[zb(0), zb(1), zb(2), zb(Z_NAV_BLOCK), zb(Z_CKV_BLOCK), rb, rb, rb,
                  wl(1, MLA_Q_LORA), wl(MLA_Q_LORA, MLA_HEADS * MLA_HEAD_PAD), wl(1, MLA_HEAD_PAD),
                  wl(1, MLA_KV_LORA), wl(1, NA_HEAD_DIM), wl(1, NA_HEAD_DIM),
                  wl(MLA_KV_LORA, 2 * MLA_HEADS * MLA_NOPE), wl(1, LANES), wl(1, LANES)],
        out_specs=[ob(MLA_HEADS * MLA_HEAD_PAD), ob(MLA_KV_LORA), ob(NA_WIDTH), ob(NA_WIDTH), ob(NA_WIDTH),
                   ob(NA_WIDTH), ob(MLA_HEADS * MLA_HEAD_PAD), ob(MLA_HEADS * MLA_NOPE)],
        out_shape=[jax.ShapeDtypeStruct((t, MLA_HEADS * MLA_HEAD_PAD), BF16),
                   jax.ShapeDtypeStruct((t, MLA_KV_LORA), F32),
                   jax.ShapeDtypeStruct((t, NA_WIDTH), BF16),
                   jax.ShapeDtypeStruct((t, NA_WIDTH), F32),
                   jax.ShapeDtypeStruct((t, NA_WIDTH), BF16),
                   jax.ShapeDtypeStruct((t, NA_WIDTH), BF16),
                   jax.ShapeDtypeStruct((t, MLA_HEADS * MLA_HEAD_PAD), BF16),
                   jax.ShapeDtypeStruct((t, MLA_HEADS * MLA_NOPE), BF16)],
        compiler_params=_params(("arbitrary",)),
        name="proj_prep",
    )(z, z, z, z, z, rope_c, rope_s1, rope_s2, qng, wuq, qhg, kvng, naqg, nakg, wukv, gn, gp)


def _kv_prep_kernel(ckv_ref, kpe_ref, rc_ref, rs1_ref, rs2_ref, wukv_ref, gn_ref, gp_ref, k_ref, v_ref):
    _mla_keys_values(ckv_ref[...], kpe_ref[...], rc_ref[...], rs1_ref[...], rs2_ref[...],
                     wukv_ref, gn_ref, gp_ref, k_ref, v_ref)


def kv_prep_cache(ckv, kpe, ident_c, ident_s, wukv, gn, gp):
    tm = SEQ_TILE
    rb = lambda w: pl.BlockSpec((None, tm, w), lambda l, i: (l, i, 0))
    tb = pl.BlockSpec((tm, LANES), lambda l, i: (0, 0))
    wl = lambda *shape: pl.BlockSpec((None,) + shape, lambda l, i: (l,) + (0,) * len(shape))
    return pl.pallas_call(
        _kv_prep_kernel,
        grid=(DEPTH, T_CACHE // tm),
        in_specs=[rb(MLA_KV_LORA), rb(LANES), tb, tb, tb,
                  wl(MLA_KV_LORA, 2 * MLA_HEADS * MLA_NOPE), wl(1, LANES), wl(1, LANES)],
        out_specs=[rb(MLA_HEADS * MLA_HEAD_PAD), rb(MLA_HEADS * MLA_NOPE)],
        out_shape=[jax.ShapeDtypeStruct((DEPTH, T_CACHE, MLA_HEADS * MLA_HEAD_PAD), BF16),
                   jax.ShapeDtypeStruct((DEPTH, T_CACHE, MLA_HEADS * MLA_NOPE), BF16)],
        compiler_params=_params(("arbitrary", "arbitrary")),
        name="kv_prep_cache",
    )(ckv, kpe, ident_c, ident_s, ident_s, wukv, gn, gp)


_NT = (((1,), (1,)), ((), ()))


def _attn_kernel(q_ref, k_ref, v_ref, o_ref, *, scale, n_heads, dq):
    dv = NA_HEAD_DIM
    for h in range(n_heads):
        q = q_ref[:, h * dq:(h + 1) * dq]
        k = k_ref[:, h * dq:(h + 1) * dq]
        s = lax.dot_general(q, k, _NT, preferred_element_type=F32) * scale
        m = jnp.max(s, axis=-1, keepdims=True)
        p = jnp.exp(s - m)
        p = p / jnp.sum(p, axis=-1, keepdims=True)
        o = jnp.dot(p.astype(BF16), v_ref[:, h * dv:(h + 1) * dv], preferred_element_type=F32)
        o_ref[:, h * dv:(h + 1) * dv] = o.astype(BF16)


def attention(q, k, v, *, n_seq, n_heads, dq, sq_total, sk, q_row_off, k_row_off, scale):
    sq = ATTN_Q_BLOCK
    nq = sq_total // sq
    qoff = q_row_off // sq
    koff = k_row_off // sk
    dv = NA_HEAD_DIM
    return pl.pallas_call(
        functools.partial(_attn_kernel, scale=scale, n_heads=n_heads, dq=dq),
        grid=(n_seq, nq),
        in_specs=[pl.BlockSpec((sq, n_heads * dq), lambda b, qi: (qoff + b * nq + qi, 0)),
                  pl.BlockSpec((sk, n_heads * dq), lambda b, qi: (koff + b, 0)),
                  pl.BlockSpec((sk, n_heads * dv), lambda b, qi: (koff + b, 0))],
        out_specs=pl.BlockSpec((sq, n_heads * dv), lambda b, qi: (b * nq + qi, 0)),
        out_shape=jax.ShapeDtypeStruct((n_seq * sq_total, n_heads * dv), BF16),
        compiler_params=_params(("arbitrary", "arbitrary")),
        name="attention",
    )(q, k, v)


def _attn_cached_kernel(q_ref, k_ref, v_ref, kc_ref, vc_ref, o_ref, *, scale, n_heads, dq):
    dv = NA_HEAD_DIM
    for h in range(n_heads):
        qk = slice(h * dq, (h + 1) * dq)
        vv = slice(h * dv, (h + 1) * dv)
        q = q_ref[:, qk]
        s_own = lax.dot_general(q, k_ref[:, qk], _NT, preferred_element_type=F32) * scale
        s_ctx = lax.dot_general(q, kc_ref[:, qk], _NT, preferred_element_type=F32) * scale
        m = jnp.maximum(jnp.max(s_own, axis=-1, keepdims=True), jnp.max(s_ctx, axis=-1, keepdims=True))
        p_own = jnp.exp(s_own - m)
        p_ctx = jnp.exp(s_ctx - m)
        denom = jnp.sum(p_own, axis=-1, keepdims=True) + jnp.sum(p_ctx, axis=-1, keepdims=True)
        o = jnp.dot((p_ctx / denom).astype(BF16), vc_ref[:, vv], preferred_element_type=F32)
        o = o + jnp.dot((p_own / denom).astype(BF16), v_ref[:, vv], preferred_element_type=F32)
        o_ref[:, vv] = o.astype(BF16)


def latent_mla_attention(q, k, v, k_cache, v_cache, layer, *, scale):
    sq = ATTN_Q_BLOCK
    nq = LAT_SEQ // sq
    qoff = T_CTX // sq
    koff = T_CTX // LAT_SEQ
    wq = MLA_HEADS * MLA_HEAD_PAD
    wv = MLA_HEADS * NA_HEAD_DIM
    return pl.pallas_call(
        functools.partial(_attn_cached_kernel, scale=scale, n_heads=MLA_HEADS, dq=MLA_HEAD_PAD),
        grid=(N_LAT_SEQ, nq),
        in_specs=[pl.BlockSpec((sq, wq), lambda b, qi: (qoff + b * nq + qi, 0)),
                  pl.BlockSpec((LAT_SEQ, wq), lambda b, qi: (koff + b, 0)),
                  pl.BlockSpec((LAT_SEQ, wv), lambda b, qi: (koff + b, 0)),
                  pl.BlockSpec((None, PAST_LEN, wq), lambda b, qi: (layer, b, 0)),
                  pl.BlockSpec((None, PAST_LEN, wv), lambda b, qi: (layer, b, 0))],
        out_specs=pl.BlockSpec((sq, wv), lambda b, qi: (b * nq + qi, 0)),
        out_shape=jax.ShapeDtypeStruct((T_LAT, wv), BF16),
        compiler_params=_params(("arbitrary", "arbitrary")),
        name="latent_mla",
    )(q, k, v, k_cache, v_cache)


NA_ROWS = LAT_SEQ // GRID_W
NA_LOCAL = NA_KR * GRID_W


def _na_lat_kernel(q_ref, k_ref, v_ref, kc_ref, vc_ref, b_ref, o_ref, *, scale):
    rq = pl.program_id(1)
    row_start = jnp.clip(rq - NA_KR // 2, 0, NA_ROWS - NA_KR)
    start = pl.multiple_of(row_start * GRID_W, GRID_W)
    for h in range(NA_HEADS):
        cols = slice(h * NA_HEAD_DIM, (h + 1) * NA_HEAD_DIM)
        q = q_ref[:, cols]
        k_loc = k_ref[pl.ds(start, NA_LOCAL), cols]
        v_loc = v_ref[pl.ds(start, NA_LOCAL), cols]
        s_loc = lax.dot_general(q, k_loc, _NT, preferred_element_type=F32) * scale + b_ref[h, rq - row_start]
        s_ctx = lax.dot_general(q, kc_ref[:, cols], _NT, preferred_element_type=F32) * scale
        m = jnp.maximum(jnp.max(s_loc, axis=-1, keepdims=True), jnp.max(s_ctx, axis=-1, keepdims=True))
        p_loc = jnp.exp(s_loc - m)
        p_ctx = jnp.exp(s_ctx - m)
        denom = jnp.sum(p_loc, axis=-1, keepdims=True) + jnp.sum(p_ctx, axis=-1, keepdims=True)
        o = jnp.dot((p_loc / denom).astype(BF16), v_loc, preferred_element_type=F32)
        o = o + jnp.dot((p_ctx / denom).astype(BF16), vc_ref[:, cols], preferred_element_type=F32)
        o_ref[:, cols] = o.astype(BF16)


def na_latent_attention(q, k, v, k_cache, v_cache, bias_tab, layer, *, scale):
    qoff = T_CTX // GRID_W
    koff = T_CTX // LAT_SEQ
    w = NA_WIDTH
    return pl.pallas_call(
        functools.partial(_na_lat_kernel, scale=scale),
        grid=(N_LAT_SEQ, NA_ROWS),
        in_specs=[pl.BlockSpec((GRID_W, w), lambda b, r: (qoff + b * NA_ROWS + r, 0)),
                  pl.BlockSpec((LAT_SEQ, w), lambda b, r: (koff + b, 0)),
                  pl.BlockSpec((LAT_SEQ, w), lambda b, r: (koff + b, 0)),
                  pl.BlockSpec((None, None, PAST_LEN, w), lambda b, r: (layer, b, 0, 0)),
                  pl.BlockSpec((None, None, PAST_LEN, w), lambda b, r: (layer, b, 0, 0)),
                  pl.BlockSpec((None, NA_HEADS, NA_KR, GRID_W, NA_LOCAL), lambda b, r: (layer, 0, 0, 0, 0))],
        out_specs=pl.BlockSpec((GRID_W, w), lambda b, r: (b * NA_ROWS + r, 0)),
        out_shape=jax.ShapeDtypeStruct((T_LAT, NA_WIDTH), BF16),
        compiler_params=_params(("arbitrary", "arbitrary")),
        name="na_latent",
    )(q, k, v, k_cache, v_cache, bias_tab)


def _conv_kernel(gb_ref, gc_ref, u_ref, w_ref, o_ref):
    gu = gc_ref[...] * u_ref[...]
    s = gu.shape[0]
    row = lax.broadcasted_iota(jnp.int32, gu.shape, 0)
    prev = jnp.where(row == 0, 0.0, pltpu.roll(gu, 1, 0))
    nxt = jnp.where(row == s - 1, 0.0, pltpu.roll(gu, s - 1, 0))
    y = prev * w_ref[0:1, :] + gu * w_ref[1:2, :] + nxt * w_ref[2:3, :]
    o_ref[...] = (gb_ref[...] * y).astype(BF16)


def short_conv(z, conv_w8, layer, *, n_seq, seq, row_off):
    off = row_off // seq
    zb = lambda k: pl.BlockSpec((seq, CONV_CH), lambda i: (off + i, k))
    return pl.pallas_call(
        _conv_kernel,
        grid=(n_seq,),
        in_specs=[zb(Z_CONV_BLOCK), zb(Z_CONV_BLOCK + 1), zb(Z_CONV_BLOCK + 2),
                  pl.BlockSpec((None, 8, CONV_CH), lambda i: (layer, 0, 0))],
        out_specs=pl.BlockSpec((seq, CONV_CH), lambda i: (i, 0)),
        out_shape=jax.ShapeDtypeStruct((n_seq * seq, CONV_CH), BF16),
        compiler_params=_params(("arbitrary",)),
        name="short_conv",
    )(z, z, z, conv_w8)


def _out_matmul_kernel(x_ref, g_ref, a0c_ref, a0l_ref, a1c_ref, a1l_ref, a2c_ref, a2l_ref,
                       w0_ref, w1_ref, w2_ref, o_ref, *, tm):
    row = pl.program_id(0) * tm
    grp = _group_of_row(row)
    is_ctx = row < T_CTX
    acc = jnp.dot(jnp.where(is_ctx, a0c_ref[...], a0l_ref[...]), w0_ref[...], preferred_element_type=F32)
    acc = acc + jnp.dot(jnp.where(is_ctx, a1c_ref[...], a1l_ref[...]), w1_ref[...], preferred_element_type=F32)
    acc = acc + jnp.dot(jnp.where(is_ctx, a2c_ref[...], a2l_ref[...]), w2_ref[...], preferred_element_type=F32)
    o_ref[...] = x_ref[...] + g_ref[pl.ds(grp, 1), :] * acc


def out_matmul(x, mods, o_mla, o_na, conv, w_out, layer, gate_chunk):
    t = x.shape[0]
    tm, tn = ROW_TILE, 1024
    nj = D_MODEL // tn
    w_mla = MLA_HEADS * MLA_NOPE
    n_ctx = T_CTX // tm
    ctx = lambda w: pl.BlockSpec((tm, w), lambda i, j: (jnp.minimum(i, n_ctx - 1), 0))
    lat = lambda w: pl.BlockSpec((tm, w), lambda i, j: (jnp.maximum(i - n_ctx, 0), 0))
    return pl.pallas_call(
        functools.partial(_out_matmul_kernel, tm=tm),
        grid=(t // tm, nj),
        in_specs=[pl.BlockSpec((tm, tn), lambda i, j: (i, j)),
                  pl.BlockSpec((None, 8, tn), lambda i, j: (layer, 0, gate_chunk * nj + j)),
                  ctx(w_mla), lat(w_mla), ctx(NA_WIDTH), lat(NA_WIDTH), ctx(CONV_CH), lat(CONV_CH),
                  pl.BlockSpec((None, w_mla, tn), lambda i, j: (layer, 0, j)),
                  pl.BlockSpec((None, NA_WIDTH, tn), lambda i, j: (layer, w_mla // NA_WIDTH, j)),
                  pl.BlockSpec((None, CONV_CH, tn), lambda i, j: (layer, (w_mla + NA_WIDTH) // CONV_CH, j))],
        out_specs=pl.BlockSpec((tm, tn), lambda i, j: (i, j)),
        out_shape=jax.ShapeDtypeStruct((t, D_MODEL), F32),
        compiler_params=_params(("arbitrary", "arbitrary")),
        name="out_matmul",
    )(x, mods, *o_mla, *o_na, *conv, w_out, w_out, w_out)


def _argmax_step(s, pos, big):
    m = jnp.max(s, axis=0, keepdims=True)
    first = jnp.min(jnp.where(s == m, pos, big), axis=0, keepdims=True)
    return m, first, pos == first


def _half_ranks(s1, s2):
    n, width = s1.shape
    pos = lax.broadcasted_iota(jnp.int32, (n, width), 0).astype(F32)
    iota16 = lax.broadcasted_iota(jnp.int32, (PEER_TOPK, width), 0)

    def body(j, carry):
        s1c, v1c, i1c, s2c, v2c, r2c = carry
        m1, first1, hit1 = _argmax_step(s1c, pos, float(n))
        m2, _, hit2 = _argmax_step(s2c, pos, float(n))
        return (jnp.where(hit1, -jnp.inf, s1c), jnp.where(iota16 == j, m1, v1c), jnp.where(iota16 == j, first1, i1c),
                jnp.where(hit2, -jnp.inf, s2c), jnp.where(iota16 == j, m2, v2c), jnp.where(hit2, j, r2c))

    zero16 = jnp.zeros((PEER_TOPK, width), F32)
    rank0 = jnp.full((n, width), float(PEER_TOPK), F32)
    _, v1, i1, _, v2, r2 = lax.fori_loop(0, PEER_TOPK, body, (s1, zero16, zero16, s2, zero16, rank0), unroll=True)
    return pos, v1, i1, v2, r2


_CAND_ROWS = 16 + 7 * 8 + 8


def _cand_positions():
    p = [j2 for j2 in range(16)]
    p += [j1 * 16 + j2 for j1 in range(1, 8) for j2 in range(8)]
    p += [j1 * 16 for j1 in range(8, 16)]
    return np.tile(np.asarray(p, np.float32)[:, None], (1, LANES))


def _staircase(v1, v2, cpos):
    cand = jnp.concatenate([v1[0:1] + v2] + [v1[j:j + 1] + v2[0:8] for j in range(1, 8)] + [v1[8:16] + v2[0:1]],
                           axis=0)
    cmax = cand[0:1]

    def body(j, carry):
        s, sel = carry
        _, _, hit = _argmax_step(s, cpos, float(PEER_TOPK * PEER_TOPK))
        return jnp.where(hit, -jnp.inf, s), jnp.where(hit, 1.0, sel)

    _, sel = lax.fori_loop(0, PEER_TOPK, body, (cand, jnp.zeros_like(cand)), unroll=True)
    z = jnp.sum(sel * jnp.exp(cand - cmax), axis=0, keepdims=True)
    iota8 = lax.broadcasted_iota(jnp.int32, (8, cand.shape[1]), 0)
    low = jnp.zeros((8, cand.shape[1]), F32)
    low = jnp.where(iota8 == 0, jnp.sum(sel[0:16], axis=0, keepdims=True), low)
    for j in range(1, 8):
        low = jnp.where(iota8 == j, jnp.sum(sel[8 + 8 * j:16 + 8 * j], axis=0, keepdims=True), low)
    return jnp.concatenate([low, sel[72:80]], axis=0), z


def _peer_topk_kernel(q_ref, keys_ref, cpos_ref, e1_ref, lb1_ref, r2_ref, e2_ref, *, heads):
    nt = (((1,), (1,)), ((), ()))
    for h in range(heads):
        q = q_ref[:, h * 2 * LANES:(h + 1) * 2 * LANES]
        s1 = lax.dot_general(keys_ref[h, 0], q[:, :LANES], nt, preferred_element_type=F32)
        s2 = lax.dot_general(keys_ref[h, 1], q[:, LANES:], nt, preferred_element_type=F32)
        pos, v1, i1, v2, r2 = _half_ranks(s1, s2)
        counts, z = _staircase(v1, v2, cpos_ref[...])
        lb1 = jnp.zeros_like(s1)
        for j in range(PEER_TOPK):
            lb1 = jnp.where(pos == i1[j:j + 1], counts[j:j + 1], lb1)
        e1_ref[h] = jnp.exp(s1 - v1[0:1])
        lb1_ref[h] = lb1
        r2_ref[h] = r2
        e2_ref[h] = jnp.exp(s2 - v2[0:1]) / z


def peer_topk(q, sub_keys, layer):
    t = q.shape[0]
    heads = PEER_HEADS
    ob = pl.BlockSpec((heads, PEER_N_KEYS, LANES), lambda i, h: (h, 0, i))
    shp = jax.ShapeDtypeStruct((PEER_HEADS, PEER_N_KEYS, t), F32)
    return pl.pallas_call(
        functools.partial(_peer_topk_kernel, heads=heads),
        grid=(t // LANES, PEER_HEADS // heads),
        in_specs=[pl.BlockSpec((LANES, heads * 2 * LANES), lambda i, h: (i, h)),
                  pl.BlockSpec((None, heads, 2, PEER_N_KEYS, LANES), lambda i, h: (layer, h, 0, 0, 0)),
                  pl.BlockSpec((_CAND_ROWS, LANES), lambda i, h: (0, 0))],
        out_specs=[ob, ob, ob, ob],
        out_shape=[shp, shp, shp, shp],
        compiler_params=_params(("arbitrary", "arbitrary")),
        name="peer_topk",
    )(q, sub_keys, jnp.asarray(_cand_positions()))


_SQRT_HALF = float(np.sqrt(0.5))


def _peer_dense_kernel(ht_ref, u_ref, v_ref, e1_ref, lb1_ref, r2_ref, e2_ref, x_ref, g_ref, o_ref,
                       acc_ref, at_ref, wg_ref, *, tt, ec):
    i = pl.program_id(0)
    c = pl.program_id(1)

    @pl.when(c == 0)
    def _():
        acc_ref[...] = jnp.zeros_like(acc_ref)

    at_ref[...] = jnp.dot(u_ref[...], ht_ref[...], preferred_element_type=F32)
    n_a = ec // PEER_N_KEYS
    grp0 = pl.multiple_of((c * n_a) // 8 * 8, 8)
    off = (c * n_a) % 8

    def key_row(ref, h, al, lanes):
        blk = ref[h, pl.ds(grp0, 8), lanes]
        row = blk[al:al + 1]
        for o in range(n_a, 8, n_a):
            row = jnp.where(off == o, blk[o + al:o + al + 1], row)
        return row

    for al in range(n_a):
        rows = slice(al * PEER_N_KEYS, (al + 1) * PEER_N_KEYS)
        for lt in range(tt // LANES):
            lanes = slice(lt * LANES, (lt + 1) * LANES)
            gate = jnp.zeros((PEER_N_KEYS, LANES), F32)
            for h in range(PEER_HEADS):
                lb = key_row(lb1_ref, h, al, lanes)
                e1 = key_row(e1_ref, h, al, lanes)
                gate = gate + jnp.where(r2_ref[h, :, lanes] < lb, e2_ref[h, :, lanes], 0.0) * e1
            act = at_ref[rows, lanes]
            wg_ref[rows, lanes] = 0.5 * act * (1.0 + lax.erf(act * _SQRT_HALF)) * gate
    acc_ref[...] += jnp.dot(wg_ref[...].T.astype(BF16), v_ref[...], preferred_element_type=F32)

    @pl.when(c == pl.num_programs(1) - 1)
    def _():
        grp = _group_of_row(i * tt)
        o_ref[...] = x_ref[...] + g_ref[pl.ds(grp, 1), :] * acc_ref[...]


def peer_dense(ht, peer_u, peer_v, e1, lb1, r2, e2, x, mods, layer, gate_chunk):
    t = x.shape[0]
    tt, ec = PEER_TOK_TILE, PEER_EXP_TILE
    once = pl.Buffered(1)
    kb = pl.BlockSpec((PEER_HEADS, PEER_N_KEYS, tt), lambda i, c: (0, 0, i))
    return pl.pallas_call(
        functools.partial(_peer_dense_kernel, tt=tt, ec=ec),
        grid=(t // tt, PEER_N_EXPERTS // ec),
        in_specs=[pl.BlockSpec((D_MODEL, tt), lambda i, c: (0, i)),
                  pl.BlockSpec((None, ec, D_MODEL), lambda i, c: (layer, c, 0)),
                  pl.BlockSpec((None, ec, D_MODEL), lambda i, c: (layer, c, 0)),
                  kb, kb, kb, kb,
                  pl.BlockSpec((tt, D_MODEL), lambda i, c: (i, 0), pipeline_mode=once),
                  pl.BlockSpec((None, 8, D_MODEL), lambda i, c: (layer, 0, gate_chunk))],
        out_specs=pl.BlockSpec((tt, D_MODEL), lambda i, c: (i, 0)),
        out_shape=jax.ShapeDtypeStruct((t, D_MODEL), F32),
        scratch_shapes=[pltpu.VMEM((tt, D_MODEL), F32),
                        pltpu.VMEM((ec, tt), F32),
                        pltpu.VMEM((ec, tt), F32)],
        compiler_params=_params(("arbitrary", "arbitrary")),
        name="peer_dense",
    )(ht, peer_u, peer_v, e1, lb1, r2, e2, x, mods)


def _rope_tables():
    t = jnp.arange(LAT_SEQ)
    row = (t // GRID_W).astype(F32)
    col = (t % GRID_W).astype(F32)
    n_freq = MLA_ROPE // 4
    inv = ROPE_THETA ** (-jnp.arange(n_freq, dtype=F32) / n_freq)
    ang = jnp.concatenate([row[:, None] * inv, col[:, None] * inv], axis=-1)
    cos, sin = jnp.cos(ang), jnp.sin(ang)
    zero = jnp.zeros_like(cos)
    pad = jnp.zeros((LAT_SEQ, LANES - MLA_ROPE), F32)
    c_lat = jnp.concatenate([cos, cos, pad], axis=-1)
    s1_lat = jnp.concatenate([-sin, zero, pad], axis=-1)
    s2_lat = jnp.concatenate([zero, sin, pad], axis=-1)
    ones = jnp.concatenate([jnp.ones((1, MLA_ROPE), F32), jnp.zeros((1, LANES - MLA_ROPE), F32)], axis=-1)

    def full(lat, ident):
        return jnp.concatenate([jnp.broadcast_to(ident, (T_CTX, LANES)), jnp.tile(lat, (N_LAT_SEQ, 1)),
                                jnp.broadcast_to(ident, (T_CACHE, LANES))], axis=0)

    zeros = jnp.zeros((1, LANES), F32)
    return full(c_lat, ones), full(s1_lat, zeros), full(s2_lat, zeros)


def _na_bias_table(rel_bias):
    cq = np.arange(GRID_W)
    col_start = np.clip(cq - NA_KC // 2, 0, GRID_W - NA_KC)
    valid = (cq[None, :] >= col_start[:, None]) & (cq[None, :] < col_start[:, None] + NA_KC)
    coff = np.clip(cq[None, :] - cq[:, None], -(NA_KC - 1), NA_KC - 1) + (NA_KC - 1)
    onehot = (coff[:, :, None] == np.arange(2 * NA_KC - 1)[None, None, :]).astype(np.float32)
    toep = jnp.einsum('lhrc,qkc->lhrqk', rel_bias, jnp.asarray(onehot), precision=lax.Precision.HIGHEST)
    toep = jnp.where(jnp.asarray(valid)[None, None, None], toep, -jnp.inf)
    tabs = []
    for d in range(NA_KR):
        rows = toep[:, :, NA_KR - 1 - d:2 * NA_KR - 1 - d]
        tabs.append(rows.transpose(0, 1, 3, 2, 4).reshape(DEPTH, NA_HEADS, GRID_W, NA_LOCAL))
    return jnp.stack(tabs, axis=2)


def kernel(x_prompt, x_sample, cache_mla_ckv, cache_mla_kpe, cache_na_k, cache_na_v, c, c_ctx, ada_w, ada_b, norm_mix_g, norm_ffn_g, w_in, mla_q_norm_g, mla_w_uq, mla_kv_norm_g, mla_w_ukv, mla_q_head_g, mla_k_head_g, na_q_head_g, na_k_head_g, na_rel_bias, conv_w, w_out, peer_w_q, peer_sub_keys, peer_u, peer_v):
    lat_end = MLA_Q_LORA + MLA_KV_LORA + MLA_ROPE
    w_in_p = jnp.concatenate([w_in[..., :MLA_Q_LORA], w_in[..., lat_end:], w_in[..., MLA_Q_LORA:lat_end],
                              jnp.zeros((DEPTH, D_MODEL, D_IN_PAD - D_IN), F32)], axis=-1).astype(BF16)
    wuq_p = jnp.pad(mla_w_uq.reshape(DEPTH, MLA_Q_LORA, MLA_HEADS, MLA_QK_DIM),
                    ((0, 0), (0, 0), (0, 0), (0, MLA_HEAD_PAD - MLA_QK_DIM))
                    ).reshape(DEPTH, MLA_Q_LORA, MLA_HEADS * MLA_HEAD_PAD).astype(BF16)
    qhg_p = jnp.pad(mla_q_head_g, ((0, 0), (0, MLA_HEAD_PAD - MLA_QK_DIM))).reshape(DEPTH, 1, MLA_HEAD_PAD)
    wukv4 = mla_w_ukv.reshape(DEPTH, MLA_KV_LORA, MLA_HEADS, 2 * MLA_NOPE)
    wukv_p = jnp.concatenate([wukv4[..., :MLA_NOPE].reshape(DEPTH, MLA_KV_LORA, -1),
                              wukv4[..., MLA_NOPE:].reshape(DEPTH, MLA_KV_LORA, -1)], axis=-1).astype(BF16)
    khg_n = mla_k_head_g[:, :MLA_NOPE].reshape(DEPTH, 1, LANES)
    khg_p = jnp.pad(mla_k_head_g[:, MLA_NOPE:], ((0, 0), (0, LANES - MLA_ROPE))).reshape(DEPTH, 1, LANES)
    conv_w8 = jnp.pad(conv_w.transpose(0, 2, 1), ((0, 0), (0, 5), (0, 0)))
    w_out_b = w_out.astype(BF16)
    peer_wq_b = peer_w_q.astype(BF16)
    sub_keys_b = peer_sub_keys.astype(BF16)
    peer_u_b = peer_u.astype(BF16)
    peer_v_b = peer_v.astype(BF16)
    g_mix = norm_mix_g.reshape(DEPTH, 1, D_MODEL)
    g_ffn = norm_ffn_g.reshape(DEPTH, 1, D_MODEL)
    qng = mla_q_norm_g.reshape(DEPTH, 1, MLA_Q_LORA)
    kvng = mla_kv_norm_g.reshape(DEPTH, 1, MLA_KV_LORA)
    naqg = na_q_head_g.reshape(DEPTH, 1, NA_HEAD_DIM)
    nakg = na_k_head_g.reshape(DEPTH, 1, NA_HEAD_DIM)
    rope_c, rope_s1, rope_s2 = _rope_tables()
    na_bias_tab = _na_bias_table(na_rel_bias)
    cache_kpe_p = jnp.pad(cache_mla_kpe, ((0, 0), (0, 0), (0, 0), (0, LANES - MLA_ROPE)))
    cache_nak = cache_na_k.transpose(1, 0, 3, 2, 4).reshape(DEPTH, N_LAT_SEQ, PAST_LEN, NA_WIDTH).astype(BF16)
    cache_nav = cache_na_v.transpose(1, 0, 3, 2, 4).reshape(DEPTH, N_LAT_SEQ, PAST_LEN, NA_WIDTH).astype(BF16)

    cpad = jnp.concatenate([c_ctx[None, :], c, jnp.zeros((8 - 1 - N_LAT_SEQ, D_MODEL), F32)], axis=0)
    mods = ada_all(cpad, ada_w, ada_b)

    x = jnp.concatenate([x_prompt.reshape(T_CTX, D_MODEL), x_sample.reshape(T_LAT, D_MODEL)], axis=0)
    ckv_out, kpe_out, nak_out, nav_out = [], [], [], []
    mla_scale = MLA_QK_DIM ** -0.5
    na_scale = NA_HEAD_DIM ** -0.5

    ident_c = jnp.broadcast_to(rope_c[:1], (SEQ_TILE, LANES))
    ident_s = jnp.zeros((SEQ_TILE, LANES), F32)
    k_cache, v_cache = kv_prep_cache(
        cache_mla_ckv.transpose(1, 0, 2, 3).reshape(DEPTH, T_CACHE, MLA_KV_LORA),
        cache_kpe_p.transpose(1, 0, 2, 3).reshape(DEPTH, T_CACHE, LANES), ident_c, ident_s, wukv_p, khg_n, khg_p)

    for l in range(DEPTH):
        (z,) = mod_matmul(x, g_mix, mods, w_in_p, l, 0, 1, tn=1024, emit_ht=False)
        q_mla, ckv_n, q_na, k_na_f, k_na_b, v_na_b, k_mla, v_mla = proj_prep(
            z, rope_c, rope_s1, rope_s2, qng, wuq_p, qhg_p, kvng, naqg, nakg, wukv_p, khg_n, khg_p, l)

        o_mla_ctx = attention(q_mla, k_mla, v_mla, n_seq=N_CTX_SEQ, n_heads=MLA_HEADS, dq=MLA_HEAD_PAD,
                              sq_total=CTX_SEQ, sk=CTX_SEQ, q_row_off=0, k_row_off=0, scale=mla_scale)
        o_mla_lat = latent_mla_attention(q_mla, k_mla, v_mla, k_cache, v_cache, l, scale=mla_scale)
        o_na_ctx = attention(q_na, k_na_b, v_na_b, n_seq=N_CTX_SEQ, n_heads=NA_HEADS, dq=NA_HEAD_DIM,
                             sq_total=CTX_SEQ, sk=CTX_SEQ, q_row_off=0, k_row_off=0, scale=na_scale)
        o_na_lat = na_latent_attention(q_na, k_na_b, v_na_b, cache_nak, cache_nav, na_bias_tab, l, scale=na_scale)
        conv_ctx = short_conv(z, conv_w8, l, n_seq=N_CTX_SEQ, seq=CTX_SEQ, row_off=0)
        conv_lat = short_conv(z, conv_w8, l, n_seq=N_LAT_SEQ, seq=LAT_SEQ, row_off=T_CTX)

        x = out_matmul(x, mods, (o_mla_ctx, o_mla_lat), (o_na_ctx, o_na_lat), (conv_ctx, conv_lat), w_out_b, l, 2)

        q_peer, ht = mod_matmul(x, g_ffn, mods, peer_wq_b, l, 3, 4, tn=D_MODEL, emit_ht=True, out_dtype=BF16)
        e1, lb1, r2, e2 = peer_topk(q_peer, sub_keys_b, l)
        x = peer_dense(ht, peer_u_b, peer_v_b, e1, lb1, r2, e2, x, mods, l, 5)

        ckv_out.append(ckv_n[:T_CTX].reshape(N_CTX_SEQ, CTX_SEQ, MLA_KV_LORA))
        kpe_out.append(z[:T_CTX, Z_KPE_COL:Z_KPE_COL + MLA_ROPE].reshape(N_CTX_SEQ, CTX_SEQ, MLA_ROPE))
        nak_out.append(k_na_f[:T_CTX].reshape(N_CTX_SEQ, CTX_SEQ, NA_HEADS, NA_HEAD_DIM).transpose(0, 2, 1, 3))
        nav_out.append(z[:T_CTX, Z_NAV_BLOCK * Z_BLOCK:(Z_NAV_BLOCK + 1) * Z_BLOCK].reshape(
            N_CTX_SEQ, CTX_SEQ, NA_HEADS, NA_HEAD_DIM).transpose(0, 2, 1, 3))

    y_prompt = x[:T_CTX].reshape(N_CTX_SEQ, CTX_SEQ, D_MODEL)
    y_sample = x[T_CTX:].reshape(N_LAT_SEQ, LAT_SEQ, D_MODEL)
    return (y_prompt, y_sample, jnp.stack(ckv_out, axis=1), jnp.stack(kpe_out, axis=1),
            jnp.stack(nak_out, axis=1), jnp.stack(nav_out, axis=1))
```

```python
import functools

import numpy as np
import jax
import jax.numpy as jnp
from jax import lax
from jax.experimental import pallas as pl
from jax.experimental.pallas import tpu as pltpu

F32 = jnp.float32
BF16 = jnp.bfloat16

D_MODEL = 2048
DEPTH = 4
N_CTX_SEQ = 32
CTX_SEQ = 256
N_LAT_SEQ = 2
LAT_SEQ = 1024
PAST_LEN = 512
T_CTX = N_CTX_SEQ * CTX_SEQ
T_LAT = N_LAT_SEQ * LAT_SEQ
T_ALL = T_CTX + T_LAT
T_CACHE = N_LAT_SEQ * PAST_LEN
GRID_W = 64
NORM_EPS = 1e-6
ROPE_THETA = 10000.0
ADA_CHUNKS = 6

MLA_HEADS = 8
MLA_NOPE = 128
MLA_ROPE = 64
MLA_QK_DIM = MLA_NOPE + MLA_ROPE
MLA_Q_LORA = 512
MLA_KV_LORA = 256
MLA_HEAD_PAD = 256
NA_HEADS = 4
NA_HEAD_DIM = 128
NA_WIDTH = NA_HEADS * NA_HEAD_DIM
NA_KR = 8
NA_KC = 16
CONV_CH = 512
Z_BLOCK = 512
Z_NAV_BLOCK = 3
Z_CONV_BLOCK = 4
Z_CKV_BLOCK = 7
D_IN_PAD = (Z_CKV_BLOCK + 1) * Z_BLOCK
Z_KPE_COL = Z_CKV_BLOCK * Z_BLOCK + MLA_KV_LORA
D_IN = MLA_Q_LORA + MLA_KV_LORA + MLA_ROPE + 3 * NA_WIDTH + 3 * CONV_CH
ATTN_Q_BLOCK = 256

PEER_HEADS = 8
PEER_N_KEYS = 128
PEER_N_EXPERTS = PEER_N_KEYS * PEER_N_KEYS
PEER_TOPK = 16

LANES = 128
VMEM_LIMIT = 60 * 1024 * 1024

ROW_TILE = 1024
SEQ_TILE = CTX_SEQ
PEER_TOK_TILE = 1024
PEER_EXP_TILE = 512


def _params(sem, flags=None):
    return pltpu.CompilerParams(dimension_semantics=sem, vmem_limit_bytes=VMEM_LIMIT, flags=flags)


def _group_of_row(row):
    return jnp.where(row < T_CTX, 0, 1 + (row - T_CTX) // LAT_SEQ)


def _rms(x, g, n):
    ms = jnp.sum(x * x, axis=-1, keepdims=True) / n
    return x * lax.rsqrt(ms + NORM_EPS) * g


def _ada_kernel(c_ref, w_ref, b_ref, o_ref):
    c = c_ref[...]
    s = c * jax.nn.sigmoid(c)
    o_ref[...] = jnp.dot(s.astype(BF16), w_ref[...].astype(BF16), preferred_element_type=F32) + b_ref[...]


def ada_all(cpad, ada_w, ada_b):
    tn = 1536
    n = ADA_CHUNKS * D_MODEL
    return pl.pallas_call(
        _ada_kernel,
        grid=(DEPTH, n // tn),
        in_specs=[pl.BlockSpec((8, D_MODEL), lambda l, j: (0, 0)),
                  pl.BlockSpec((None, D_MODEL, tn), lambda l, j: (l, 0, j)),
                  pl.BlockSpec((None, 1, tn), lambda l, j: (l, 0, j))],
        out_specs=pl.BlockSpec((None, 8, tn), lambda l, j: (l, 0, j)),
        out_shape=jax.ShapeDtypeStruct((DEPTH, 8, n), F32),
        compiler_params=_params(("arbitrary", "arbitrary")),
        name="ada",
    )(cpad, ada_w, ada_b.reshape(DEPTH, 1, n))


def _mod_matmul_kernel(x_ref, g_ref, sh_ref, sc_ref, w_ref, z_ref, *rest, tm, emit_ht):
    if emit_ht:
        ht_ref, h_scr = rest
    else:
        (h_scr,) = rest
    i = pl.program_id(0)

    @pl.when(pl.program_id(1) == 0)
    def _():
        grp = _group_of_row(i * tm)
        y = _rms(x_ref[...], g_ref[...], D_MODEL)
        h = y * (1.0 + sc_ref[pl.ds(grp, 1), :]) + sh_ref[pl.ds(grp, 1), :]
        h_scr[...] = h.astype(BF16)
        if emit_ht:
            ht_ref[...] = h.T.astype(BF16)

    z_ref[...] = jnp.dot(h_scr[...], w_ref[...], preferred_element_type=F32).astype(z_ref.dtype)


def mod_matmul(x, gain, mods, w, layer, shift_chunk, scale_chunk, *, tn, emit_ht, tm=ROW_TILE, out_dtype=F32):
    t = x.shape[0]
    n = w.shape[-1]
    out_shape = [jax.ShapeDtypeStruct((t, n), out_dtype)]
    out_specs = [pl.BlockSpec((tm, tn), lambda i, j: (i, j))]
    if emit_ht:
        out_shape.append(jax.ShapeDtypeStruct((D_MODEL, t), BF16))
        out_specs.append(pl.BlockSpec((D_MODEL, tm), lambda i, j: (0, i)))
    res = pl.pallas_call(
        functools.partial(_mod_matmul_kernel, tm=tm, emit_ht=emit_ht),
        grid=(t // tm, n // tn),
        in_specs=[pl.BlockSpec((tm, D_MODEL), lambda i, j: (i, 0)),
                  pl.BlockSpec((None, 1, D_MODEL), lambda i, j: (layer, 0, 0)),
                  pl.BlockSpec((None, 8, D_MODEL), lambda i, j: (layer, 0, shift_chunk)),
                  pl.BlockSpec((None, 8, D_MODEL), lambda i, j: (layer, 0, scale_chunk)),
                  pl.BlockSpec((None, D_MODEL, tn), lambda i, j: (layer, 0, j))],
        out_specs=out_specs,
        out_shape=out_shape,
        scratch_shapes=[pltpu.VMEM((tm, D_MODEL), BF16)],
        compiler_params=_params(("arbitrary", "arbitrary")),
        name="mod_matmul_ht" if emit_ht else "mod_matmul",
    )(x, gain, mods, mods, w)
    return res


def _rope128(x, c, s1, s2):
    return x * c + pltpu.roll(x, 96, 1) * s1 + pltpu.roll(x, 32, 1) * s2


def _mla_keys_values(ckv_n, kpe, rc, rs1, rs2, wukv_ref, gn_ref, gp_ref, k_ref, v_ref):
    kv = jnp.dot(ckv_n.astype(BF16), wukv_ref[...], preferred_element_type=F32)
    pe_ss = jnp.sum(kpe * kpe, axis=-1, keepdims=True)
    for h in range(MLA_HEADS):
        kn = kv[:, h * MLA_NOPE:(h + 1) * MLA_NOPE]
        ms = (jnp.sum(kn * kn, axis=-1, keepdims=True) + pe_ss) / MLA_QK_DIM
        r = lax.rsqrt(ms + NORM_EPS)
        k_ref[:, h * MLA_HEAD_PAD:h * MLA_HEAD_PAD + LANES] = (kn * r * gn_ref[...]).astype(BF16)
        k_ref[:, h * MLA_HEAD_PAD + LANES:(h + 1) * MLA_HEAD_PAD] = (
            _rope128(kpe * r * gp_ref[...], rc, rs1, rs2).astype(BF16))
    v_ref[...] = kv[:, MLA_HEADS * MLA_NOPE:].astype(BF16)


def _proj_prep_kernel(cq_ref, naq_ref, nak_ref, nav_ref, ckv_ref, rc_ref, rs1_ref, rs2_ref,
                      qng_ref, wuq_ref, qhg_ref, kvng_ref, naqg_ref, nakg_ref, wukv_ref, gn_ref, gp_ref,
                      qmla_ref, ckvn_ref, qna_ref, knaf_ref, vnaf_ref, knab_ref, vnab_ref, k_ref, v_ref):
    cqn = _rms(cq_ref[...], qng_ref[...], MLA_Q_LORA)
    q = jnp.dot(cqn.astype(BF16), wuq_ref[...], preferred_element_type=F32)
    rc, rs1, rs2 = rc_ref[...], rs1_ref[...], rs2_ref[...]
    for h in range(MLA_HEADS):
        qh = _rms(q[:, h * MLA_HEAD_PAD:(h + 1) * MLA_HEAD_PAD], qhg_ref[...], MLA_QK_DIM)
        qmla_ref[:, h * MLA_HEAD_PAD:h * MLA_HEAD_PAD + LANES] = qh[:, :LANES].astype(BF16)
        qmla_ref[:, h * MLA_HEAD_PAD + LANES:(h + 1) * MLA_HEAD_PAD] = (
            _rope128(qh[:, LANES:], rc, rs1, rs2).astype(BF16))
    ckv_n = _rms(ckv_ref[:, :MLA_KV_LORA], kvng_ref[...], MLA_KV_LORA)
    ckvn_ref[...] = ckv_n
    _mla_keys_values(ckv_n, ckv_ref[:, MLA_KV_LORA:MLA_KV_LORA + LANES], rc, rs1, rs2,
                     wukv_ref, gn_ref, gp_ref, k_ref, v_ref)
    for h in range(NA_HEADS):
        sl = slice(h * NA_HEAD_DIM, (h + 1) * NA_HEAD_DIM)
        qna_ref[:, sl] = _rms(naq_ref[:, sl], naqg_ref[...], NA_HEAD_DIM).astype(BF16)
        kn = _rms(nak_ref[:, sl], nakg_ref[...], NA_HEAD_DIM)
        knaf_ref[h] = kn
        vnaf_ref[h] = nav_ref[:, sl]
        knab_ref[:, sl] = kn.astype(BF16)
    vnab_ref[...] = nav_ref[...].astype(BF16)


def proj_prep(z, rope_c, rope_s1, rope_s2, qng, wuq, qhg, kvng, naqg, nakg, wukv, gn, gp, layer):
    t = z.shape[0]
    tm = SEQ_TILE
    zb = lambda k: pl.BlockSpec((tm, Z_BLOCK), lambda i: (i, k))
    rb = pl.BlockSpec((tm, LANES), lambda i: (i, 0))
    wl = lambda *shape: pl.BlockSpec((None,) + shape, lambda i: (layer,) + (0,) * len(shape))
    ob = lambda w: pl.BlockSpec((tm, w), lambda i: (i, 0))
    cache_b = pl.BlockSpec((None, NA_HEADS, tm, NA_HEAD_DIM), lambda i: (i, 0, 0, 0))
    return pl.pallas_call(
        _proj_prep_kernel,
        grid=(t // tm,),
        in_specs=[zb(0), zb(1), zb(2), zb(Z_NAV_BLOCK), zb(Z_CKV_BLOCK), rb, rb, rb,
                  wl(1, MLA_Q_LORA), wl(MLA_Q_LORA, MLA_HEADS * MLA_HEAD_PAD), wl(1, MLA_HEAD_PAD),
                  wl(1, MLA_KV_LORA), wl(1, NA_HEAD_DIM), wl(1, NA_HEAD_DIM),
                  wl(MLA_KV_LORA, 2 * MLA_HEADS * MLA_NOPE), wl(1, LANES), wl(1, LANES)],
        out_specs=[ob(MLA_HEADS * MLA_HEAD_PAD), ob(MLA_KV_LORA), ob(NA_WIDTH), cache_b, cache_b, ob(NA_WIDTH),
                   ob(NA_WIDTH), ob(MLA_HEADS * MLA_HEAD_PAD), ob(MLA_HEADS * MLA_NOPE)],
        out_shape=[jax.ShapeDtypeStruct((t, MLA_HEADS * MLA_HEAD_PAD), BF16),
                   jax.ShapeDtypeStruct((t, MLA_KV_LORA), F32),
                   jax.ShapeDtypeStruct((t, NA_WIDTH), BF16),
                   jax.ShapeDtypeStruct((t // tm, NA_HEADS, tm, NA_HEAD_DIM), F32),
                   jax.ShapeDtypeStruct((t // tm, NA_HEADS, tm, NA_HEAD_DIM), F32),
                   jax.ShapeDtypeStruct((t, NA_WIDTH), BF16),
                   jax.ShapeDtypeStruct((t, NA_WIDTH), BF16),
                   jax.ShapeDtypeStruct((t, MLA_HEADS * MLA_HEAD_PAD), BF16),
                   jax.ShapeDtypeStruct((t, MLA_HEADS * MLA_NOPE), BF16)],
        compiler_params=_params(("arbitrary",)),
        name="proj_prep",
    )(z, z, z, z, z, rope_c, rope_s1, rope_s2, qng, wuq, qhg, kvng, naqg, nakg, wukv, gn, gp)


def _kv_prep_kernel(ckv_ref, kpe_ref, rc_ref, rs1_ref, rs2_ref, wukv_ref, gn_ref, gp_ref, k_ref, v_ref):
    _mla_keys_values(ckv_ref[...], kpe_ref[...], rc_ref[...], rs1_ref[...], rs2_ref[...],
                     wukv_ref, gn_ref, gp_ref, k_ref, v_ref)


def kv_prep_cache(ckv, kpe, ident_c, ident_s, wukv, gn, gp):
    tm = SEQ_TILE
    rb = lambda w: pl.BlockSpec((None, tm, w), lambda l, i: (l, i, 0))
    tb = pl.BlockSpec((tm, LANES), lambda l, i: (0, 0))
    wl = lambda *shape: pl.BlockSpec((None,) + shape, lambda l, i: (l,) + (0,) * len(shape))
    return pl.pallas_call(
        _kv_prep_kernel,
        grid=(DEPTH, T_CACHE // tm),
        in_specs=[rb(MLA_KV_LORA), rb(LANES), tb, tb, tb,
                  wl(MLA_KV_LORA, 2 * MLA_HEADS * MLA_NOPE), wl(1, LANES), wl(1, LANES)],
        out_specs=[rb(MLA_HEADS * MLA_HEAD_PAD), rb(MLA_HEADS * MLA_NOPE)],
        out_shape=[jax.ShapeDtypeStruct((DEPTH, T_CACHE, MLA_HEADS * MLA_HEAD_PAD), BF16),
                   jax.ShapeDtypeStruct((DEPTH, T_CACHE, MLA_HEADS * MLA_NOPE), BF16)],
        compiler_params=_params(("arbitrary", "arbitrary")),
        name="kv_prep_cache",
    )(ckv, kpe, ident_c, ident_s, ident_s, wukv, gn, gp)


_NT = (((1,), (1,)), ((), ()))


def _attn_kernel(q_ref, k_ref, v_ref, o_ref, *, scale, n_heads, dq):
    dv = NA_HEAD_DIM
    for h in range(n_heads):
        q = q_ref[:, h * dq:(h + 1) * dq]
        k = k_ref[:, h * dq:(h + 1) * dq]
        s = lax.dot_general(q, k, _NT, preferred_element_type=F32) * scale
        m = jnp.max(s, axis=-1, keepdims=True)
        p = jnp.exp(s - m)
        p = p / jnp.sum(p, axis=-1, keepdims=True)
        o = jnp.dot(p.astype(BF16), v_ref[:, h * dv:(h + 1) * dv], preferred_element_type=F32)
        o_ref[:, h * dv:(h + 1) * dv] = o.astype(BF16)


def attention(q, k, v, *, n_seq, n_heads, dq, sq_total, sk, q_row_off, k_row_off, scale):
    sq = ATTN_Q_BLOCK
    nq = sq_total // sq
    qoff = q_row_off // sq
    koff = k_row_off // sk
    dv = NA_HEAD_DIM
    return pl.pallas_call(
        functools.partial(_attn_kernel, scale=scale, n_heads=n_heads, dq=dq),
        grid=(n_seq, nq),
        in_specs=[pl.BlockSpec((sq, n_heads * dq), lambda b, qi: (qoff + b * nq + qi, 0)),
                  pl.BlockSpec((sk, n_heads * dq), lambda b, qi: (koff + b, 0)),
                  pl.BlockSpec((sk, n_heads * dv), lambda b, qi: (koff + b, 0))],
        out_specs=pl.BlockSpec((sq, n_heads * dv), lambda b, qi: (b * nq + qi, 0)),
        out_shape=jax.ShapeDtypeStruct((n_seq * sq_total, n_heads * dv), BF16),
        compiler_params=_params(("arbitrary", "arbitrary")),
        name="attention",
    )(q, k, v)


def _attn_cached_kernel(q_ref, k_ref, v_ref, kc_ref, vc_ref, o_ref, *, scale, n_heads, dq):
    dv = NA_HEAD_DIM
    for h in range(n_heads):
        qk = slice(h * dq, (h + 1) * dq)
        vv = slice(h * dv, (h + 1) * dv)
        q = q_ref[:, qk]
        s_own = lax.dot_general(q, k_ref[:, qk], _NT, preferred_element_type=F32) * scale
        s_ctx = lax.dot_general(q, kc_ref[:, qk], _NT, preferred_element_type=F32) * scale
        m = jnp.maximum(jnp.max(s_own, axis=-1, keepdims=True), jnp.max(s_ctx, axis=-1, keepdims=True))
        p_own = jnp.exp(s_own - m)
        p_ctx = jnp.exp(s_ctx - m)
        denom = jnp.sum(p_own, axis=-1, keepdims=True) + jnp.sum(p_ctx, axis=-1, keepdims=True)
        o = jnp.dot((p_ctx / denom).astype(BF16), vc_ref[:, vv], preferred_element_type=F32)
        o = o + jnp.dot((p_own / denom).astype(BF16), v_ref[:, vv], preferred_element_type=F32)
        o_ref[:, vv] = o.astype(BF16)


def latent_mla_attention(q, k, v, k_cache, v_cache, layer, *, scale):
    sq = ATTN_Q_BLOCK
    nq = LAT_SEQ // sq
    qoff = T_CTX // sq
    koff = T_CTX // LAT_SEQ
    wq = MLA_HEADS * MLA_HEAD_PAD
    wv = MLA_HEADS * NA_HEAD_DIM
    return pl.pallas_call(
        functools.partial(_attn_cached_kernel, scale=scale, n_heads=MLA_HEADS, dq=MLA_HEAD_PAD),
        grid=(N_LAT_SEQ, nq),
        in_specs=[pl.BlockSpec((sq, wq), lambda b, qi: (qoff + b * nq + qi, 0)),
                  pl.BlockSpec((LAT_SEQ, wq), lambda b, qi: (koff + b, 0)),
                  pl.BlockSpec((LAT_SEQ, wv), lambda b, qi: (koff + b, 0)),
                  pl.BlockSpec((None, PAST_LEN, wq), lambda b, qi: (layer, b, 0)),
                  pl.BlockSpec((None, PAST_LEN, wv), lambda b, qi: (layer, b, 0))],
        out_specs=pl.BlockSpec((sq, wv), lambda b, qi: (b * nq + qi, 0)),
        out_shape=jax.ShapeDtypeStruct((T_LAT, wv), BF16),
        compiler_params=_params(("arbitrary", "arbitrary")),
        name="latent_mla",
    )(q, k, v, k_cache, v_cache)


NA_ROWS = LAT_SEQ // GRID_W
NA_LOCAL = NA_KR * GRID_W


def _na_lat_kernel(q_ref, k_ref, v_ref, kc_ref, vc_ref, b_ref, o_ref, *, scale):
    rq = pl.program_id(1)
    row_start = jnp.clip(rq - NA_KR // 2, 0, NA_ROWS - NA_KR)
    start = pl.multiple_of(row_start * GRID_W, GRID_W)
    for h in range(NA_HEADS):
        cols = slice(h * NA_HEAD_DIM, (h + 1) * NA_HEAD_DIM)
        q = q_ref[:, cols]
        k_loc = k_ref[pl.ds(start, NA_LOCAL), cols]
        v_loc = v_ref[pl.ds(start, NA_LOCAL), cols]
        s_loc = lax.dot_general(q, k_loc, _NT, preferred_element_type=F32) * scale + b_ref[h, rq - row_start]
        s_ctx = lax.dot_general(q, kc_ref[:, cols], _NT, preferred_element_type=F32) * scale
        m = jnp.maximum(jnp.max(s_loc, axis=-1, keepdims=True), jnp.max(s_ctx, axis=-1, keepdims=True))
        p_loc = jnp.exp(s_loc - m)
        p_ctx = jnp.exp(s_ctx - m)
        denom = jnp.sum(p_loc, axis=-1, keepdims=True) + jnp.sum(p_ctx, axis=-1, keepdims=True)
        o = jnp.dot((p_loc / denom).astype(BF16), v_loc, preferred_element_type=F32)
        o = o + jnp.dot((p_ctx / denom).astype(BF16), vc_ref[:, cols], preferred_element_type=F32)
        o_ref[:, cols] = o.astype(BF16)


def na_latent_attention(q, k, v, k_cache, v_cache, bias_tab, layer, *, scale):
    qoff = T_CTX // GRID_W
    koff = T_CTX // LAT_SEQ
    w = NA_WIDTH
    return pl.pallas_call(
        functools.partial(_na_lat_kernel, scale=scale),
        grid=(N_LAT_SEQ, NA_ROWS),
        in_specs=[pl.BlockSpec((GRID_W, w), lambda b, r: (qoff + b * NA_ROWS + r, 0)),
                  pl.BlockSpec((LAT_SEQ, w), lambda b, r: (koff + b, 0)),
                  pl.BlockSpec((LAT_SEQ, w), lambda b, r: (koff + b, 0)),
                  pl.BlockSpec((None, None, PAST_LEN, w), lambda b, r: (layer, b, 0, 0)),
                  pl.BlockSpec((None, None, PAST_LEN, w), lambda b, r: (layer, b, 0, 0)),
                  pl.BlockSpec((None, NA_HEADS, NA_KR, GRID_W, NA_LOCAL), lambda b, r: (layer, 0, 0, 0, 0))],
        out_specs=pl.BlockSpec((GRID_W, w), lambda b, r: (b * NA_ROWS + r, 0)),
        out_shape=jax.ShapeDtypeStruct((T_LAT, NA_WIDTH), BF16),
        compiler_params=_params(("arbitrary", "arbitrary")),
        name="na_latent",
    )(q, k, v, k_cache, v_cache, bias_tab)


def _conv_kernel(gb_ref, gc_ref, u_ref, w_ref, o_ref):
    gu = gc_ref[...] * u_ref[...]
    s = gu.shape[0]
    row = lax.broadcasted_iota(jnp.int32, gu.shape, 0)
    prev = jnp.where(row == 0, 0.0, pltpu.roll(gu, 1, 0))
    nxt = jnp.where(row == s - 1, 0.0, pltpu.roll(gu, s - 1, 0))
    y = prev * w_ref[0:1, :] + gu * w_ref[1:2, :] + nxt * w_ref[2:3, :]
    o_ref[...] = (gb_ref[...] * y).astype(BF16)


def short_conv(z, conv_w8, layer, *, n_seq, seq, row_off):
    off = row_off // seq
    zb = lambda k: pl.BlockSpec((seq, CONV_CH), lambda i: (off + i, k))
    return pl.pallas_call(
        _conv_kernel,
        grid=(n_seq,),
        in_specs=[zb(Z_CONV_BLOCK), zb(Z_CONV_BLOCK + 1), zb(Z_CONV_BLOCK + 2),
                  pl.BlockSpec((None, 8, CONV_CH), lambda i: (layer, 0, 0))],
        out_specs=pl.BlockSpec((seq, CONV_CH), lambda i: (i, 0)),
        out_shape=jax.ShapeDtypeStruct((n_seq * seq, CONV_CH), BF16),
        compiler_params=_params(("arbitrary",)),
        name="short_conv",
    )(z, z, z, conv_w8)


def _out_matmul_kernel(x_ref, g_ref, a0c_ref, a0l_ref, a1c_ref, a1l_ref, a2c_ref, a2l_ref,
                       w0_ref, w1_ref, w2_ref, o_ref, *, tm):
    row = pl.program_id(0) * tm
    grp = _group_of_row(row)
    is_ctx = row < T_CTX
    acc = jnp.dot(jnp.where(is_ctx, a0c_ref[...], a0l_ref[...]), w0_ref[...], preferred_element_type=F32)
    acc = acc + jnp.dot(jnp.where(is_ctx, a1c_ref[...], a1l_ref[...]), w1_ref[...], preferred_element_type=F32)
    acc = acc + jnp.dot(jnp.where(is_ctx, a2c_ref[...], a2l_ref[...]), w2_ref[...], preferred_element_type=F32)
    o_ref[...] = x_ref[...] + g_ref[pl.ds(grp, 1), :] * acc


def out_matmul(x, mods, o_mla, o_na, conv, w_out, layer, gate_chunk):
    t = x.shape[0]
    tm, tn = ROW_TILE, 1024
    nj = D_MODEL // tn
    w_mla = MLA_HEADS * MLA_NOPE
    n_ctx = T_CTX // tm
    ctx = lambda w: pl.BlockSpec((tm, w), lambda i, j: (jnp.minimum(i, n_ctx - 1), 0))
    lat = lambda w: pl.BlockSpec((tm, w), lambda i, j: (jnp.maximum(i - n_ctx, 0), 0))
    return pl.pallas_call(
        functools.partial(_out_matmul_kernel, tm=tm),
        grid=(t // tm, nj),
        in_specs=[pl.BlockSpec((tm, tn), lambda i, j: (i, j)),
                  pl.BlockSpec((None, 8, tn), lambda i, j: (layer, 0, gate_chunk * nj + j)),
                  ctx(w_mla), lat(w_mla), ctx(NA_WIDTH), lat(NA_WIDTH), ctx(CONV_CH), lat(CONV_CH),
                  pl.BlockSpec((None, w_mla, tn), lambda i, j: (layer, 0, j)),
                  pl.BlockSpec((None, NA_WIDTH, tn), lambda i, j: (layer, w_mla // NA_WIDTH, j)),
                  pl.BlockSpec((None, CONV_CH, tn), lambda i, j: (layer, (w_mla + NA_WIDTH) // CONV_CH, j))],
        out_specs=pl.BlockSpec((tm, tn), lambda i, j: (i, j)),
        out_shape=jax.ShapeDtypeStruct((t, D_MODEL), F32),
        compiler_params=_params(("arbitrary", "arbitrary")),
        name="out_matmul",
    )(x, mods, *o_mla, *o_na, *conv, w_out, w_out, w_out)


def _argmax_step(s, pos, big):
    m = jnp.max(s, axis=0, keepdims=True)
    first = jnp.min(jnp.where(s == m, pos, big), axis=0, keepdims=True)
    return m, first, pos == first


def _half_ranks(s1, s2):
    n, width = s1.shape
    pos = lax.broadcasted_iota(jnp.int32, (n, width), 0).astype(F32)
    iota16 = lax.broadcasted_iota(jnp.int32, (PEER_TOPK, width), 0)

    def body(j, carry):
        s1c, v1c, i1c, s2c, v2c, r2c = carry
        m1, first1, hit1 = _argmax_step(s1c, pos, float(n))
        m2, _, hit2 = _argmax_step(s2c, pos, float(n))
        return (jnp.where(hit1, -jnp.inf, s1c), jnp.where(iota16 == j, m1, v1c), jnp.where(iota16 == j, first1, i1c),
                jnp.where(hit2, -jnp.inf, s2c), jnp.where(iota16 == j, m2, v2c), jnp.where(hit2, j, r2c))

    zero16 = jnp.zeros((PEER_TOPK, width), F32)
    rank0 = jnp.full((n, width), float(PEER_TOPK), F32)
    _, v1, i1, _, v2, r2 = lax.fori_loop(0, PEER_TOPK, body, (s1, zero16, zero16, s2, zero16, rank0), unroll=True)
    return pos, v1, i1, v2, r2


_CAND_ROWS = 16 + 7 * 8 + 8


def _cand_positions():
    p = [j2 for j2 in range(16)]
    p += [j1 * 16 + j2 for j1 in range(1, 8) for j2 in range(8)]
    p += [j1 * 16 for j1 in range(8, 16)]
    return np.tile(np.asarray(p, np.float32)[:, None], (1, LANES))


def _staircase(v1, v2, cpos):
    cand = jnp.concatenate([v1[0:1] + v2] + [v1[j:j + 1] + v2[0:8] for j in range(1, 8)] + [v1[8:16] + v2[0:1]],
                           axis=0)
    cmax = cand[0:1]

    def body(j, carry):
        s, sel = carry
        _, _, hit = _argmax_step(s, cpos, float(PEER_TOPK * PEER_TOPK))
        return jnp.where(hit, -jnp.inf, s), jnp.where(hit, 1.0, sel)

    _, sel = lax.fori_loop(0, PEER_TOPK, body, (cand, jnp.zeros_like(cand)), unroll=True)
    z = jnp.sum(sel * jnp.exp(cand - cmax), axis=0, keepdims=True)
    iota8 = lax.broadcasted_iota(jnp.int32, (8, cand.shape[1]), 0)
    low = jnp.zeros((8, cand.shape[1]), F32)
    low = jnp.where(iota8 == 0, jnp.sum(sel[0:16], axis=0, keepdims=True), low)
    for j in range(1, 8):
        low = jnp.where(iota8 == j, jnp.sum(sel[8 + 8 * j:16 + 8 * j], axis=0, keepdims=True), low)
    return jnp.concatenate([low, sel[72:80]], axis=0), z


def _peer_topk_kernel(q_ref, keys_ref, cpos_ref, e1_ref, lb1_ref, r2_ref, e2_ref, *, heads):
    nt = (((1,), (1,)), ((), ()))
    for h in range(heads):
        q = q_ref[:, h * 2 * LANES:(h + 1) * 2 * LANES]
        s1 = lax.dot_general(keys_ref[h, 0], q[:, :LANES], nt, preferred_element_type=F32)
        s2 = lax.dot_general(keys_ref[h, 1], q[:, LANES:], nt, preferred_element_type=F32)
        pos, v1, i1, v2, r2 = _half_ranks(s1, s2)
        counts, z = _staircase(v1, v2, cpos_ref[...])
        lb1 = jnp.zeros_like(s1)
        for j in range(PEER_TOPK):
            lb1 = jnp.where(pos == i1[j:j + 1], counts[j:j + 1], lb1)
        e1_ref[h] = jnp.exp(s1 - v1[0:1])
        lb1_ref[h] = lb1
        r2_ref[h] = r2
        e2_ref[h] = jnp.exp(s2 - v2[0:1]) / z


def peer_topk(q, sub_keys, layer):
    t = q.shape[0]
    heads = PEER_HEADS
    ob = pl.BlockSpec((heads, PEER_N_KEYS, LANES), lambda i, h: (h, 0, i))
    shp = jax.ShapeDtypeStruct((PEER_HEADS, PEER_N_KEYS, t), F32)
    return pl.pallas_call(
        functools.partial(_peer_topk_kernel, heads=heads),
        grid=(t // LANES, PEER_HEADS // heads),
        in_specs=[pl.BlockSpec((LANES, heads * 2 * LANES), lambda i, h: (i, h)),
                  pl.BlockSpec((None, heads, 2, PEER_N_KEYS, LANES), lambda i, h: (layer, h, 0, 0, 0)),
                  pl.BlockSpec((_CAND_ROWS, LANES), lambda i, h: (0, 0))],
        out_specs=[ob, ob, ob, ob],
        out_shape=[shp, shp, shp, shp],
        compiler_params=_params(("arbitrary", "arbitrary")),
        name="peer_topk",
    )(q, sub_keys, jnp.asarray(_cand_positions()))


_SQRT_HALF = float(np.sqrt(0.5))


def _peer_dense_kernel(ht_ref, u_ref, v_ref, e1_ref, lb1_ref, r2_ref, e2_ref, x_ref, g_ref, o_ref,
                       acc_ref, at_ref, wg_ref, *, tt, ec):
    i = pl.program_id(0)
    c = pl.program_id(1)

    @pl.when(c == 0)
    def _():
        acc_ref[...] = jnp.zeros_like(acc_ref)

    at_ref[...] = jnp.dot(u_ref[...], ht_ref[...], preferred_element_type=F32)
    n_a = ec // PEER_N_KEYS
    grp0 = pl.multiple_of((c * n_a) // 8 * 8, 8)
    off = (c * n_a) % 8

    def key_row(ref, h, al, lanes):
        blk = ref[h, pl.ds(grp0, 8), lanes]
        row = blk[al:al + 1]
        for o in range(n_a, 8, n_a):
            row = jnp.where(off == o, blk[o + al:o + al + 1], row)
        return row

    for al in range(n_a):
        rows = slice(al * PEER_N_KEYS, (al + 1) * PEER_N_KEYS)
        for lt in range(tt // LANES):
            lanes = slice(lt * LANES, (lt + 1) * LANES)
            gate = jnp.zeros((PEER_N_KEYS, LANES), F32)
            for h in range(PEER_HEADS):
                lb = key_row(lb1_ref, h, al, lanes)
                e1 = key_row(e1_ref, h, al, lanes)
                gate = gate + jnp.where(r2_ref[h, :, lanes] < lb, e2_ref[h, :, lanes], 0.0) * e1
            act = at_ref[rows, lanes]
            wg_ref[rows, lanes] = 0.5 * act * (1.0 + lax.erf(act * _SQRT_HALF)) * gate
    acc_ref[...] += jnp.dot(wg_ref[...].T.astype(BF16), v_ref[...], preferred_element_type=F32)

    @pl.when(c == pl.num_programs(1) - 1)
    def _():
        grp = _group_of_row(i * tt)
        o_ref[...] = x_ref[...] + g_ref[pl.ds(grp, 1), :] * acc_ref[...]


def peer_dense(ht, peer_u, peer_v, e1, lb1, r2, e2, x, mods, layer, gate_chunk):
    t = x.shape[0]
    tt, ec = PEER_TOK_TILE, PEER_EXP_TILE
    once = pl.Buffered(1)
    kb = pl.BlockSpec((PEER_HEADS, PEER_N_KEYS, tt), lambda i, c: (0, 0, i), pipeline_mode=once)
    return pl.pallas_call(
        functools.partial(_peer_dense_kernel, tt=tt, ec=ec),
        grid=(t // tt, PEER_N_EXPERTS // ec),
        in_specs=[pl.BlockSpec((D_MODEL, tt), lambda i, c: (0, i), pipeline_mode=once),
                  pl.BlockSpec((None, ec, D_MODEL), lambda i, c: (layer, c, 0)),
                  pl.BlockSpec((None, ec, D_MODEL), lambda i, c: (layer, c, 0)),
                  kb, kb, kb, kb,
                  pl.BlockSpec((tt, D_MODEL), lambda i, c: (i, 0), pipeline_mode=once),
                  pl.BlockSpec((None, 8, D_MODEL), lambda i, c: (layer, 0, gate_chunk))],
        out_specs=pl.BlockSpec((tt, D_MODEL), lambda i, c: (i, 0), pipeline_mode=once),
        out_shape=jax.ShapeDtypeStruct((t, D_MODEL), F32),
        scratch_shapes=[pltpu.VMEM((tt, D_MODEL), F32),
                        pltpu.VMEM((ec, tt), F32),
                        pltpu.VMEM((ec, tt), F32)],
        compiler_params=_params(("arbitrary", "arbitrary")),
        name="peer_dense",
    )(ht, peer_u, peer_v, e1, lb1, r2, e2, x, mods)


def _rope_tables():
    t = jnp.arange(LAT_SEQ)
    row = (t // GRID_W).astype(F32)
    col = (t % GRID_W).astype(F32)
    n_freq = MLA_ROPE // 4
    inv = ROPE_THETA ** (-jnp.arange(n_freq, dtype=F32) / n_freq)
    ang = jnp.concatenate([row[:, None] * inv, col[:, None] * inv], axis=-1)
    cos, sin = jnp.cos(ang), jnp.sin(ang)
    zero = jnp.zeros_like(cos)
    pad = jnp.zeros((LAT_SEQ, LANES - MLA_ROPE), F32)
    c_lat = jnp.concatenate([cos, cos, pad], axis=-1)
    s1_lat = jnp.concatenate([-sin, zero, pad], axis=-1)
    s2_lat = jnp.concatenate([zero, sin, pad], axis=-1)
    ones = jnp.concatenate([jnp.ones((1, MLA_ROPE), F32), jnp.zeros((1, LANES - MLA_ROPE), F32)], axis=-1)

    def full(lat, ident):
        return jnp.concatenate([jnp.broadcast_to(ident, (T_CTX, LANES)), jnp.tile(lat, (N_LAT_SEQ, 1)),
                                jnp.broadcast_to(ident, (T_CACHE, LANES))], axis=0)

    zeros = jnp.zeros((1, LANES), F32)
    return full(c_lat, ones), full(s1_lat, zeros), full(s2_lat, zeros)


def _na_bias_table(rel_bias):
    cq = np.arange(GRID_W)
    col_start = np.clip(cq - NA_KC // 2, 0, GRID_W - NA_KC)
    valid = (cq[None, :] >= col_start[:, None]) & (cq[None, :] < col_start[:, None] + NA_KC)
    coff = np.clip(cq[None, :] - cq[:, None], -(NA_KC - 1), NA_KC - 1) + (NA_KC - 1)
    onehot = (coff[:, :, None] == np.arange(2 * NA_KC - 1)[None, None, :]).astype(np.float32)
    toep = jnp.einsum('lhrc,qkc->lhrqk', rel_bias, jnp.asarray(onehot), precision=lax.Precision.HIGHEST)
    toep = jnp.where(jnp.asarray(valid)[None, None, None], toep, -jnp.inf)
    tabs = []
    for d in range(NA_KR):
        rows = toep[:, :, NA_KR - 1 - d:2 * NA_KR - 1 - d]
        tabs.append(rows.transpose(0, 1, 3, 2, 4).reshape(DEPTH, NA_HEADS, GRID_W, NA_LOCAL))
    return jnp.stack(tabs, axis=2)


def kernel(x_prompt, x_sample, cache_mla_ckv, cache_mla_kpe, cache_na_k, cache_na_v, c, c_ctx, ada_w, ada_b, norm_mix_g, norm_ffn_g, w_in, mla_q_norm_g, mla_w_uq, mla_kv_norm_g, mla_w_ukv, mla_q_head_g, mla_k_head_g, na_q_head_g, na_k_head_g, na_rel_bias, conv_w, w_out, peer_w_q, peer_sub_keys, peer_u, peer_v):
    lat_end = MLA_Q_LORA + MLA_KV_LORA + MLA_ROPE
    w_in_p = jnp.concatenate([w_in[..., :MLA_Q_LORA], w_in[..., lat_end:], w_in[..., MLA_Q_LORA:lat_end],
                              jnp.zeros((DEPTH, D_MODEL, D_IN_PAD - D_IN), F32)], axis=-1).astype(BF16)
    wuq_p = jnp.pad(mla_w_uq.reshape(DEPTH, MLA_Q_LORA, MLA_HEADS, MLA_QK_DIM),
                    ((0, 0), (0, 0), (0, 0), (0, MLA_HEAD_PAD - MLA_QK_DIM))
                    ).reshape(DEPTH, MLA_Q_LORA, MLA_HEADS * MLA_HEAD_PAD).astype(BF16)
    qhg_p = jnp.pad(mla_q_head_g, ((0, 0), (0, MLA_HEAD_PAD - MLA_QK_DIM))).reshape(DEPTH, 1, MLA_HEAD_PAD)
    wukv4 = mla_w_ukv.reshape(DEPTH, MLA_KV_LORA, MLA_HEADS, 2 * MLA_NOPE)
    wukv_p = jnp.concatenate([wukv4[..., :MLA_NOPE].reshape(DEPTH, MLA_KV_LORA, -1),
                              wukv4[..., MLA_NOPE:].reshape(DEPTH, MLA_KV_LORA, -1)], axis=-1).astype(BF16)
    khg_n = mla_k_head_g[:, :MLA_NOPE].reshape(DEPTH, 1, LANES)
    khg_p = jnp.pad(mla_k_head_g[:, MLA_NOPE:], ((0, 0), (0, LANES - MLA_ROPE))).reshape(DEPTH, 1, LANES)
    conv_w8 = jnp.pad(conv_w.transpose(0, 2, 1), ((0, 0), (0, 5), (0, 0)))
    w_out_b = w_out.astype(BF16)
    peer_wq_b = peer_w_q.astype(BF16)
    sub_keys_b = peer_sub_keys.astype(BF16)
    peer_u_b = peer_u.astype(BF16)
    peer_v_b = peer_v.astype(BF16)
    g_mix = norm_mix_g.reshape(DEPTH, 1, D_MODEL)
    g_ffn = norm_ffn_g.reshape(DEPTH, 1, D_MODEL)
    qng = mla_q_norm_g.reshape(DEPTH, 1, MLA_Q_LORA)
    kvng = mla_kv_norm_g.reshape(DEPTH, 1, MLA_KV_LORA)
    naqg = na_q_head_g.reshape(DEPTH, 1, NA_HEAD_DIM)
    nakg = na_k_head_g.reshape(DEPTH, 1, NA_HEAD_DIM)
    rope_c, rope_s1, rope_s2 = _rope_tables()
    na_bias_tab = _na_bias_table(na_rel_bias)
    cache_kpe_p = jnp.pad(cache_mla_kpe, ((0, 0), (0, 0), (0, 0), (0, LANES - MLA_ROPE)))
    cache_nak = cache_na_k.transpose(1, 0, 3, 2, 4).reshape(DEPTH, N_LAT_SEQ, PAST_LEN, NA_WIDTH).astype(BF16)
    cache_nav = cache_na_v.transpose(1, 0, 3, 2, 4).reshape(DEPTH, N_LAT_SEQ, PAST_LEN, NA_WIDTH).astype(BF16)

    cpad = jnp.concatenate([c_ctx[None, :], c, jnp.zeros((8 - 1 - N_LAT_SEQ, D_MODEL), F32)], axis=0)
    mods = ada_all(cpad, ada_w, ada_b)

    x = jnp.concatenate([x_prompt.reshape(T_CTX, D_MODEL), x_sample.reshape(T_LAT, D_MODEL)], axis=0)
    ckv_out, kpe_out, nak_out, nav_out = [], [], [], []
    mla_scale = MLA_QK_DIM ** -0.5
    na_scale = NA_HEAD_DIM ** -0.5

    ident_c = jnp.broadcast_to(rope_c[:1], (SEQ_TILE, LANES))
    ident_s = jnp.zeros((SEQ_TILE, LANES), F32)
    k_cache, v_cache = kv_prep_cache(
        cache_mla_ckv.transpose(1, 0, 2, 3).reshape(DEPTH, T_CACHE, MLA_KV_LORA),
        cache_kpe_p.transpose(1, 0, 2, 3).reshape(DEPTH, T_CACHE, LANES), ident_c, ident_s, wukv_p, khg_n, khg_p)

    for l in range(DEPTH):
        (z,) = mod_matmul(x, g_mix, mods, w_in_p, l, 0, 1, tn=1024, emit_ht=False)
        q_mla, ckv_n, q_na, k_na_f, v_na_f, k_na_b, v_na_b, k_mla, v_mla = proj_prep(
            z, rope_c, rope_s1, rope_s2, qng, wuq_p, qhg_p, kvng, naqg, nakg, wukv_p, khg_n, khg_p, l)

        o_mla_ctx = attention(q_mla, k_mla, v_mla, n_seq=N_CTX_SEQ, n_heads=MLA_HEADS, dq=MLA_HEAD_PAD,
                              sq_total=CTX_SEQ, sk=CTX_SEQ, q_row_off=0, k_row_off=0, scale=mla_scale)
        o_mla_lat = latent_mla_attention(q_mla, k_mla, v_mla, k_cache, v_cache, l, scale=mla_scale)
        o_na_ctx = attention(q_na, k_na_b, v_na_b, n_seq=N_CTX_SEQ, n_heads=NA_HEADS, dq=NA_HEAD_DIM,
                             sq_total=CTX_SEQ, sk=CTX_SEQ, q_row_off=0, k_row_off=0, scale=na_scale)
        o_na_lat = na_latent_attention(q_na, k_na_b, v_na_b, cache_nak, cache_nav, na_bias_tab, l, scale=na_scale)
        conv_ctx = short_conv(z, conv_w8, l, n_seq=N_CTX_SEQ, seq=CTX_SEQ, row_off=0)
        conv_lat = short_conv(z, conv_w8, l, n_seq=N_LAT_SEQ, seq=LAT_SEQ, row_off=T_CTX)

        x = out_matmul(x, mods, (o_mla_ctx, o_mla_lat), (o_na_ctx, o_na_lat), (conv_ctx, conv_lat), w_out_b, l, 2)

        q_peer, ht = mod_matmul(x, g_ffn, mods, peer_wq_b, l, 3, 4, tn=D_MODEL, emit_ht=True, out_dtype=BF16)
        e1, lb1, r2, e2 = peer_topk(q_peer, sub_keys_b, l)
        x = peer_dense(ht, peer_u_b, peer_v_b, e1, lb1, r2, e2, x, mods, l, 5)

        ckv_out.append(ckv_n[:T_CTX].reshape(N_CTX_SEQ, CTX_SEQ, MLA_KV_LORA))
        kpe_out.append(z[:T_CTX, Z_KPE_COL:Z_KPE_COL + MLA_ROPE].reshape(N_CTX_SEQ, CTX_SEQ, MLA_ROPE))
        nak_out.append(k_na_f[:N_CTX_SEQ])
        nav_out.append(v_na_f[:N_CTX_SEQ])

    y_prompt = x[:T_CTX].reshape(N_CTX_SEQ, CTX_SEQ, D_MODEL)
    y_sample = x[T_CTX:].reshape(N_LAT_SEQ, LAT_SEQ, D_MODEL)
    return (y_prompt, y_sample, jnp.stack(ckv_out, axis=1), jnp.stack(kpe_out, axis=1),
            jnp.stack(nak_out, axis=1), jnp.stack(nav_out, axis=1))
```

```python
import functools

import numpy as np
import jax
import jax.numpy as jnp
from jax import lax
from jax.experimental import pallas as pl
from jax.experimental.pallas import tpu as pltpu

F32 = jnp.float32
BF16 = jnp.bfloat16

D_MODEL = 2048
DEPTH = 4
N_CTX_SEQ = 32
CTX_SEQ = 256
N_LAT_SEQ = 2
LAT_SEQ = 1024
PAST_LEN = 512
T_CTX = N_CTX_SEQ * CTX_SEQ
T_LAT = N_LAT_SEQ * LAT_SEQ
T_ALL = T_CTX + T_LAT
T_CACHE = N_LAT_SEQ * PAST_LEN
GRID_W = 64
NORM_EPS = 1e-6
ROPE_THETA = 10000.0
ADA_CHUNKS = 6

MLA_HEADS = 8
MLA_NOPE = 128
MLA_ROPE = 64
MLA_QK_DIM = MLA_NOPE + MLA_ROPE
MLA_Q_LORA = 512
MLA_KV_LORA = 256
MLA_HEAD_PAD = 256
NA_HEADS = 4
NA_HEAD_DIM = 128
NA_WIDTH = NA_HEADS * NA_HEAD_DIM
NA_KR = 8
NA_KC = 16
CONV_CH = 512
Z_BLOCK = 512
Z_NAV_BLOCK = 3
Z_CONV_BLOCK = 4
Z_CKV_BLOCK = 7
D_IN_PAD = (Z_CKV_BLOCK + 1) * Z_BLOCK
Z_KPE_COL = Z_CKV_BLOCK * Z_BLOCK + MLA_KV_LORA
D_IN = MLA_Q_LORA + MLA_KV_LORA + MLA_ROPE + 3 * NA_WIDTH + 3 * CONV_CH
ATTN_Q_BLOCK = 256

PEER_HEADS = 8
PEER_N_KEYS = 128
PEER_N_EXPERTS = PEER_N_KEYS * PEER_N_KEYS
PEER_TOPK = 16

LANES = 128
SUBLANES = 8
VMEM_LIMIT = 60 * 1024 * 1024

ROW_TILE = 1024
SEQ_TILE = CTX_SEQ
PEER_TOK_TILE = 1024
PEER_EXP_TILE = 512


def _params(sem):
    return pltpu.CompilerParams(dimension_semantics=sem, vmem_limit_bytes=VMEM_LIMIT)


def _group_of_row(row):
    return jnp.where(row < T_CTX, 0, 1 + (row - T_CTX) // LAT_SEQ)


def _rms(x, g, n):
    ms = jnp.sum(x * x, axis=-1, keepdims=True) / n
    return x * lax.rsqrt(ms + NORM_EPS) * g


def _ada_kernel(c_ref, w_ref, b_ref, o_ref):
    c = c_ref[...]
    s = c * jax.nn.sigmoid(c)
    o_ref[...] = jnp.dot(s.astype(BF16), w_ref[...].astype(BF16), preferred_element_type=F32) + b_ref[...]


def ada_all(cpad, ada_w, ada_b):
    tn = 1536
    n = ADA_CHUNKS * D_MODEL
    return pl.pallas_call(
        _ada_kernel,
        grid=(DEPTH, n // tn),
        in_specs=[pl.BlockSpec((8, D_MODEL), lambda l, j: (0, 0)),
                  pl.BlockSpec((None, D_MODEL, tn), lambda l, j: (l, 0, j)),
                  pl.BlockSpec((None, 1, tn), lambda l, j: (l, 0, j))],
        out_specs=pl.BlockSpec((None, 8, tn), lambda l, j: (l, 0, j)),
        out_shape=jax.ShapeDtypeStruct((DEPTH, 8, n), F32),
        compiler_params=_params(("arbitrary", "arbitrary")),
        name="ada",
    )(cpad, ada_w, ada_b.reshape(DEPTH, 1, n))


def _mod_matmul_kernel(x_ref, g_ref, sh_ref, sc_ref, w_ref, z_ref, *rest, tm, emit_ht):
    if emit_ht:
        ht_ref, h_scr = rest
    else:
        (h_scr,) = rest
    i = pl.program_id(0)

    @pl.when(pl.program_id(1) == 0)
    def _():
        grp = _group_of_row(i * tm)
        y = _rms(x_ref[...], g_ref[...], D_MODEL)
        h = y * (1.0 + sc_ref[pl.ds(grp, 1), :]) + sh_ref[pl.ds(grp, 1), :]
        h_scr[...] = h.astype(BF16)
        if emit_ht:
            ht_ref[...] = h.T.astype(BF16)

    z_ref[...] = jnp.dot(h_scr[...], w_ref[...], preferred_element_type=F32).astype(z_ref.dtype)


def mod_matmul(x, gain, mods, w, layer, shift_chunk, scale_chunk, *, tn, emit_ht, tm=ROW_TILE, out_dtype=F32):
    t = x.shape[0]
    n = w.shape[-1]
    out_shape = [jax.ShapeDtypeStruct((t, n), out_dtype)]
    out_specs = [pl.BlockSpec((tm, tn), lambda i, j: (i, j))]
    if emit_ht:
        out_shape.append(jax.ShapeDtypeStruct((D_MODEL, t), BF16))
        out_specs.append(pl.BlockSpec((D_MODEL, tm), lambda i, j: (0, i)))
    res = pl.pallas_call(
        functools.partial(_mod_matmul_kernel, tm=tm, emit_ht=emit_ht),
        grid=(t // tm, n // tn),
        in_specs=[pl.BlockSpec((tm, D_MODEL), lambda i, j: (i, 0)),
                  pl.BlockSpec((None, 1, D_MODEL), lambda i, j: (layer, 0, 0)),
                  pl.BlockSpec((None, 8, D_MODEL), lambda i, j: (layer, 0, shift_chunk)),
                  pl.BlockSpec((None, 8, D_MODEL), lambda i, j: (layer, 0, scale_chunk)),
                  pl.BlockSpec((None, D_MODEL, tn), lambda i, j: (layer, 0, j))],
        out_specs=out_specs,
        out_shape=out_shape,
        scratch_shapes=[pltpu.VMEM((tm, D_MODEL), BF16)],
        compiler_params=_params(("arbitrary", "arbitrary")),
        name="mod_matmul_ht" if emit_ht else "mod_matmul",
    )(x, gain, mods, mods, w)
    return res


def _rope128(x, c, s1, s2):
    return x * c + pltpu.roll(x, 96, 1) * s1 + pltpu.roll(x, 32, 1) * s2


def _mla_keys_values(ckv_n, kpe, rc, rs1, rs2, wukv_ref, gn_ref, gp_ref, k_ref, v_ref):
    kv = jnp.dot(ckv_n.astype(BF16), wukv_ref[...], preferred_element_type=F32)
    pe_ss = jnp.sum(kpe * kpe, axis=-1, keepdims=True)
    for h in range(MLA_HEADS):
        kn = kv[:, h * MLA_NOPE:(h + 1) * MLA_NOPE]
        ms = (jnp.sum(kn * kn, axis=-1, keepdims=True) + pe_ss) / MLA_QK_DIM
        r = lax.rsqrt(ms + NORM_EPS)
        k_ref[:, h * MLA_HEAD_PAD:h * MLA_HEAD_PAD + LANES] = (kn * r * gn_ref[...]).astype(BF16)
        k_ref[:, h * MLA_HEAD_PAD + LANES:(h + 1) * MLA_HEAD_PAD] = (
            _rope128(kpe * r * gp_ref[...], rc, rs1, rs2).astype(BF16))
    v_ref[...] = kv[:, MLA_HEADS * MLA_NOPE:].astype(BF16)


def _proj_prep_kernel(cq_ref, naq_ref, nak_ref, nav_ref, ckv_ref, rc_ref, rs1_ref, rs2_ref,
                      qng_ref, wuq_ref, qhg_ref, kvng_ref, naqg_ref, nakg_ref, wukv_ref, gn_ref, gp_ref,
                      qmla_ref, ckvn_ref, qna_ref, knaf_ref, vnaf_ref, knab_ref, vnab_ref, k_ref, v_ref):
    cqn = _rms(cq_ref[...], qng_ref[...], MLA_Q_LORA)
    q = jnp.dot(cqn.astype(BF16), wuq_ref[...], preferred_element_type=F32)
    rc, rs1, rs2 = rc_ref[...], rs1_ref[...], rs2_ref[...]
    for h in range(MLA_HEADS):
        qh = _rms(q[:, h * MLA_HEAD_PAD:(h + 1) * MLA_HEAD_PAD], qhg_ref[...], MLA_QK_DIM)
        qmla_ref[:, h * MLA_HEAD_PAD:h * MLA_HEAD_PAD + LANES] = qh[:, :LANES].astype(BF16)
        qmla_ref[:, h * MLA_HEAD_PAD + LANES:(h + 1) * MLA_HEAD_PAD] = (
            _rope128(qh[:, LANES:], rc, rs1, rs2).astype(BF16))
    ckv_n = _rms(ckv_ref[:, :MLA_KV_LORA], kvng_ref[...], MLA_KV_LORA)
    ckvn_ref[...] = ckv_n
    _mla_keys_values(ckv_n, ckv_ref[:, MLA_KV_LORA:MLA_KV_LORA + LANES], rc, rs1, rs2,
                     wukv_ref, gn_ref, gp_ref, k_ref, v_ref)
    for h in range(NA_HEADS):
        sl = slice(h * NA_HEAD_DIM, (h + 1) * NA_HEAD_DIM)
        qna_ref[:, sl] = _rms(naq_ref[:, sl], naqg_ref[...], NA_HEAD_DIM).astype(BF16)
        kn = _rms(nak_ref[:, sl], nakg_ref[...], NA_HEAD_DIM)
        knaf_ref[h] = kn
        vnaf_ref[h] = nav_ref[:, sl]
        knab_ref[:, sl] = kn.astype(BF16)
    vnab_ref[...] = nav_ref[...].astype(BF16)


def proj_prep(z, rope_c, rope_s1, rope_s2, qng, wuq, qhg, kvng, naqg, nakg, wukv, gn, gp, layer):
    t = z.shape[0]
    tm = SEQ_TILE
    zb = lambda k: pl.BlockSpec((tm, Z_BLOCK), lambda i: (i, k))
    rb = pl.BlockSpec((tm, LANES), lambda i: (i, 0))
    wl = lambda *shape: pl.BlockSpec((None,) + shape, lambda i: (layer,) + (0,) * len(shape))
    ob = lambda w: pl.BlockSpec((tm, w), lambda i: (i, 0))
    cache_b = pl.BlockSpec((None, NA_HEADS, tm, NA_HEAD_DIM), lambda i: (i, 0, 0, 0))
    return pl.pallas_call(
        _proj_prep_kernel,
        grid=(t // tm,),
        in_specs=[zb(0), zb(1), zb(2), zb(Z_NAV_BLOCK), zb(Z_CKV_BLOCK), rb, rb, rb,
                  wl(1, MLA_Q_LORA), wl(MLA_Q_LORA, MLA_HEADS * MLA_HEAD_PAD), wl(1, MLA_HEAD_PAD),
                  wl(1, MLA_KV_LORA), wl(1, NA_HEAD_DIM), wl(1, NA_HEAD_DIM),
                  wl(MLA_KV_LORA, 2 * MLA_HEADS * MLA_NOPE), wl(1, LANES), wl(1, LANES)],
        out_specs=[ob(MLA_HEADS * MLA_HEAD_PAD), ob(MLA_KV_LORA), ob(NA_WIDTH), cache_b, cache_b, ob(NA_WIDTH),
                   ob(NA_WIDTH), ob(MLA_HEADS * MLA_HEAD_PAD), ob(MLA_HEADS * MLA_NOPE)],
        out_shape=[jax.ShapeDtypeStruct((t, MLA_HEADS * MLA_HEAD_PAD), BF16),
                   jax.ShapeDtypeStruct((t, MLA_KV_LORA), F32),
                   jax.ShapeDtypeStruct((t, NA_WIDTH), BF16),
                   jax.ShapeDtypeStruct((t // tm, NA_HEADS, tm, NA_HEAD_DIM), F32),
                   jax.ShapeDtypeStruct((t // tm, NA_HEADS, tm, NA_HEAD_DIM), F32),
                   jax.ShapeDtypeStruct((t, NA_WIDTH), BF16),
                   jax.ShapeDtypeStruct((t, NA_WIDTH), BF16),
                   jax.ShapeDtypeStruct((t, MLA_HEADS * MLA_HEAD_PAD), BF16),
                   jax.ShapeDtypeStruct((t, MLA_HEADS * MLA_NOPE), BF16)],
        compiler_params=_params(("arbitrary",)),
        name="proj_prep",
    )(z, z, z, z, z, rope_c, rope_s1, rope_s2, qng, wuq, qhg, kvng, naqg, nakg, wukv, gn, gp)


def _kv_prep_kernel(ckv_ref, kpe_ref, rc_ref, rs1_ref, rs2_ref, wukv_ref, gn_ref, gp_ref, k_ref, v_ref):
    _mla_keys_values(ckv_ref[...], kpe_ref[...], rc_ref[...], rs1_ref[...], rs2_ref[...],
                     wukv_ref, gn_ref, gp_ref, k_ref, v_ref)


def kv_prep_cache(ckv, kpe, ident_c, ident_s, wukv, gn, gp):
    tm = SEQ_TILE
    rb = lambda w: pl.BlockSpec((None, tm, w), lambda l, i: (l, i, 0))
    tb = pl.BlockSpec((tm, LANES), lambda l, i: (0, 0))
    wl = lambda *shape: pl.BlockSpec((None,) + shape, lambda l, i: (l,) + (0,) * len(shape))
    return pl.pallas_call(
        _kv_prep_kernel,
        grid=(DEPTH, T_CACHE // tm),
        in_specs=[rb(MLA_KV_LORA), rb(LANES), tb, tb, tb,
                  wl(MLA_KV_LORA, 2 * MLA_HEADS * MLA_NOPE), wl(1, LANES), wl(1, LANES)],
        out_specs=[rb(MLA_HEADS * MLA_HEAD_PAD), rb(MLA_HEADS * MLA_NOPE)],
        out_shape=[jax.ShapeDtypeStruct((DEPTH, T_CACHE, MLA_HEADS * MLA_HEAD_PAD), BF16),
                   jax.ShapeDtypeStruct((DEPTH, T_CACHE, MLA_HEADS * MLA_NOPE), BF16)],
        compiler_params=_params(("arbitrary", "arbitrary")),
        name="kv_prep_cache",
    )(ckv, kpe, ident_c, ident_s, ident_s, wukv, gn, gp)


_NT = (((1,), (1,)), ((), ()))


def _attn_kernel(q_ref, k_ref, v_ref, o_ref, *, scale, n_heads, dq):
    dv = NA_HEAD_DIM
    for h in range(n_heads):
        q = q_ref[:, h * dq:(h + 1) * dq]
        k = k_ref[:, h * dq:(h + 1) * dq]
        s = lax.dot_general(q, k, _NT, preferred_element_type=F32) * scale
        m = jnp.max(s, axis=-1, keepdims=True)
        p = jnp.exp(s - m)
        p = p / jnp.sum(p, axis=-1, keepdims=True)
        o = jnp.dot(p.astype(BF16), v_ref[:, h * dv:(h + 1) * dv], preferred_element_type=F32)
        o_ref[:, h * dv:(h + 1) * dv] = o.astype(BF16)


def attention(q, k, v, *, n_seq, n_heads, dq, sq_total, sk, q_row_off, k_row_off, scale):
    sq = ATTN_Q_BLOCK
    nq = sq_total // sq
    qoff = q_row_off // sq
    koff = k_row_off // sk
    dv = NA_HEAD_DIM
    return pl.pallas_call(
        functools.partial(_attn_kernel, scale=scale, n_heads=n_heads, dq=dq),
        grid=(n_seq, nq),
        in_specs=[pl.BlockSpec((sq, n_heads * dq), lambda b, qi: (qoff + b * nq + qi, 0)),
                  pl.BlockSpec((sk, n_heads * dq), lambda b, qi: (koff + b, 0)),
                  pl.BlockSpec((sk, n_heads * dv), lambda b, qi: (koff + b, 0))],
        out_specs=pl.BlockSpec((sq, n_heads * dv), lambda b, qi: (b * nq + qi, 0)),
        out_shape=jax.ShapeDtypeStruct((n_seq * sq_total, n_heads * dv), BF16),
        compiler_params=_params(("arbitrary", "arbitrary")),
        name="attention",
    )(q, k, v)


def _attn_cached_kernel(q_ref, k_ref, v_ref, kc_ref, vc_ref, o_ref, *, scale, n_heads, dq):
    dv = NA_HEAD_DIM
    for h in range(n_heads):
        qk = slice(h * dq, (h + 1) * dq)
        vv = slice(h * dv, (h + 1) * dv)
        q = q_ref[:, qk]
        s_own = lax.dot_general(q, k_ref[:, qk], _NT, preferred_element_type=F32) * scale
        s_ctx = lax.dot_general(q, kc_ref[:, qk], _NT, preferred_element_type=F32) * scale
        m = jnp.maximum(jnp.max(s_own, axis=-1, keepdims=True), jnp.max(s_ctx, axis=-1, keepdims=True))
        p_own = jnp.exp(s_own - m)
        p_ctx = jnp.exp(s_ctx - m)
        denom = jnp.sum(p_own, axis=-1, keepdims=True) + jnp.sum(p_ctx, axis=-1, keepdims=True)
        o = jnp.dot((p_ctx / denom).astype(BF16), vc_ref[:, vv], preferred_element_type=F32)
        o = o + jnp.dot((p_own / denom).astype(BF16), v_ref[:, vv], preferred_element_type=F32)
        o_ref[:, vv] = o.astype(BF16)


def latent_mla_attention(q, k, v, k_cache, v_cache, layer, *, scale):
    sq = ATTN_Q_BLOCK
    nq = LAT_SEQ // sq
    qoff = T_CTX // sq
    koff = T_CTX // LAT_SEQ
    wq = MLA_HEADS * MLA_HEAD_PAD
    wv = MLA_HEADS * NA_HEAD_DIM
    return pl.pallas_call(
        functools.partial(_attn_cached_kernel, scale=scale, n_heads=MLA_HEADS, dq=MLA_HEAD_PAD),
        grid=(N_LAT_SEQ, nq),
        in_specs=[pl.BlockSpec((sq, wq), lambda b, qi: (qoff + b * nq + qi, 0)),
                  pl.BlockSpec((LAT_SEQ, wq), lambda b, qi: (koff + b, 0)),
                  pl.BlockSpec((LAT_SEQ, wv), lambda b, qi: (koff + b, 0)),
                  pl.BlockSpec((None, PAST_LEN, wq), lambda b, qi: (layer, b, 0)),
                  pl.BlockSpec((None, PAST_LEN, wv), lambda b, qi: (layer, b, 0))],
        out_specs=pl.BlockSpec((sq, wv), lambda b, qi: (b * nq + qi, 0)),
        out_shape=jax.ShapeDtypeStruct((T_LAT, wv), BF16),
        compiler_params=_params(("arbitrary", "arbitrary")),
        name="latent_mla",
    )(q, k, v, k_cache, v_cache)


NA_ROWS = LAT_SEQ // GRID_W
NA_LOCAL = NA_KR * GRID_W


def _na_lat_kernel(q_ref, k_ref, v_ref, kc_ref, vc_ref, b_ref, o_ref, *, scale):
    rq = pl.program_id(1)
    row_start = jnp.clip(rq - NA_KR // 2, 0, NA_ROWS - NA_KR)
    start = pl.multiple_of(row_start * GRID_W, GRID_W)
    for h in range(NA_HEADS):
        cols = slice(h * NA_HEAD_DIM, (h + 1) * NA_HEAD_DIM)
        q = q_ref[:, cols]
        k_loc = k_ref[pl.ds(start, NA_LOCAL), cols]
        v_loc = v_ref[pl.ds(start, NA_LOCAL), cols]
        s_loc = lax.dot_general(q, k_loc, _NT, preferred_element_type=F32) * scale + b_ref[h, rq - row_start]
        s_ctx = lax.dot_general(q, kc_ref[:, cols], _NT, preferred_element_type=F32) * scale
        m = jnp.maximum(jnp.max(s_loc, axis=-1, keepdims=True), jnp.max(s_ctx, axis=-1, keepdims=True))
        p_loc = jnp.exp(s_loc - m)
        p_ctx = jnp.exp(s_ctx - m)
        denom = jnp.sum(p_loc, axis=-1, keepdims=True) + jnp.sum(p_ctx, axis=-1, keepdims=True)
        o = jnp.dot((p_loc / denom).astype(BF16), v_loc, preferred_element_type=F32)
        o = o + jnp.dot((p_ctx / denom).astype(BF16), vc_ref[:, cols], preferred_element_type=F32)
        o_ref[:, cols] = o.astype(BF16)


def na_latent_attention(q, k, v, k_cache, v_cache, bias_tab, layer, *, scale):
    qoff = T_CTX // GRID_W
    koff = T_CTX // LAT_SEQ
    w = NA_WIDTH
    return pl.pallas_call(
        functools.partial(_na_lat_kernel, scale=scale),
        grid=(N_LAT_SEQ, NA_ROWS),
        in_specs=[pl.BlockSpec((GRID_W, w), lambda b, r: (qoff + b * NA_ROWS + r, 0)),
                  pl.BlockSpec((LAT_SEQ, w), lambda b, r: (koff + b, 0)),
                  pl.BlockSpec((LAT_SEQ, w), lambda b, r: (koff + b, 0)),
                  pl.BlockSpec((None, None, PAST_LEN, w), lambda b, r: (layer, b, 0, 0)),
                  pl.BlockSpec((None, None, PAST_LEN, w), lambda b, r: (layer, b, 0, 0)),
                  pl.BlockSpec((None, NA_HEADS, NA_KR, GRID_W, NA_LOCAL), lambda b, r: (layer, 0, 0, 0, 0))],
        out_specs=pl.BlockSpec((GRID_W, w), lambda b, r: (b * NA_ROWS + r, 0)),
        out_shape=jax.ShapeDtypeStruct((T_LAT, NA_WIDTH), BF16),
        compiler_params=_params(("arbitrary", "arbitrary")),
        name="na_latent",
    )(q, k, v, k_cache, v_cache, bias_tab)


def _conv_kernel(gb_ref, gc_ref, u_ref, w_ref, o_ref):
    gu = gc_ref[...] * u_ref[...]
    s = gu.shape[0]
    row = lax.broadcasted_iota(jnp.int32, gu.shape, 0)
    prev = jnp.where(row == 0, 0.0, pltpu.roll(gu, 1, 0))
    nxt = jnp.where(row == s - 1, 0.0, pltpu.roll(gu, s - 1, 0))
    y = prev * w_ref[0:1, :] + gu * w_ref[1:2, :] + nxt * w_ref[2:3, :]
    o_ref[...] = (gb_ref[...] * y).astype(BF16)


def short_conv(z, conv_w8, layer, *, n_seq, seq, row_off):
    off = row_off // seq
    zb = lambda k: pl.BlockSpec((seq, CONV_CH), lambda i: (off + i, k))
    return pl.pallas_call(
        _conv_kernel,
        grid=(n_seq,),
        in_specs=[zb(Z_CONV_BLOCK), zb(Z_CONV_BLOCK + 1), zb(Z_CONV_BLOCK + 2),
                  pl.BlockSpec((None, 8, CONV_CH), lambda i: (layer, 0, 0))],
        out_specs=pl.BlockSpec((seq, CONV_CH), lambda i: (i, 0)),
        out_shape=jax.ShapeDtypeStruct((n_seq * seq, CONV_CH), BF16),
        compiler_params=_params(("arbitrary",)),
        name="short_conv",
    )(z, z, z, conv_w8)


def _out_matmul_kernel(x_ref, g_ref, a0c_ref, a0l_ref, a1c_ref, a1l_ref, a2c_ref, a2l_ref,
                       w0_ref, w1_ref, w2_ref, o_ref, *, tm):
    row = pl.program_id(0) * tm
    grp = _group_of_row(row)
    is_ctx = row < T_CTX
    acc = jnp.dot(jnp.where(is_ctx, a0c_ref[...], a0l_ref[...]), w0_ref[...], preferred_element_type=F32)
    acc = acc + jnp.dot(jnp.where(is_ctx, a1c_ref[...], a1l_ref[...]), w1_ref[...], preferred_element_type=F32)
    acc = acc + jnp.dot(jnp.where(is_ctx, a2c_ref[...], a2l_ref[...]), w2_ref[...], preferred_element_type=F32)
    o_ref[...] = x_ref[...] + g_ref[pl.ds(grp, 1), :] * acc


def out_matmul(x, mods, o_mla, o_na, conv, w_out, layer, gate_chunk):
    t = x.shape[0]
    tm, tn = ROW_TILE, 1024
    nj = D_MODEL // tn
    w_mla = MLA_HEADS * MLA_NOPE
    n_ctx = T_CTX // tm
    ctx = lambda w: pl.BlockSpec((tm, w), lambda i, j: (jnp.minimum(i, n_ctx - 1), 0))
    lat = lambda w: pl.BlockSpec((tm, w), lambda i, j: (jnp.maximum(i - n_ctx, 0), 0))
    return pl.pallas_call(
        functools.partial(_out_matmul_kernel, tm=tm),
        grid=(t // tm, nj),
        in_specs=[pl.BlockSpec((tm, tn), lambda i, j: (i, j)),
                  pl.BlockSpec((None, 8, tn), lambda i, j: (layer, 0, gate_chunk * nj + j)),
                  ctx(w_mla), lat(w_mla), ctx(NA_WIDTH), lat(NA_WIDTH), ctx(CONV_CH), lat(CONV_CH),
                  pl.BlockSpec((None, w_mla, tn), lambda i, j: (layer, 0, j)),
                  pl.BlockSpec((None, NA_WIDTH, tn), lambda i, j: (layer, w_mla // NA_WIDTH, j)),
                  pl.BlockSpec((None, CONV_CH, tn), lambda i, j: (layer, (w_mla + NA_WIDTH) // CONV_CH, j))],
        out_specs=pl.BlockSpec((tm, tn), lambda i, j: (i, j)),
        out_shape=jax.ShapeDtypeStruct((t, D_MODEL), F32),
        compiler_params=_params(("arbitrary", "arbitrary")),
        name="out_matmul",
    )(x, mods, *o_mla, *o_na, *conv, w_out, w_out, w_out)


def _argmax_step(s, pos, big):
    vals = [s[k:k + SUBLANES] for k in range(0, s.shape[0], SUBLANES)]
    idxs = [pos[k:k + SUBLANES] for k in range(0, s.shape[0], SUBLANES)]
    while len(vals) > 1:
        nxt_v, nxt_i = [], []
        for a in range(0, len(vals) - 1, 2):
            take_hi = vals[a + 1] > vals[a]
            nxt_v.append(jnp.maximum(vals[a], vals[a + 1]))
            nxt_i.append(jnp.where(take_hi, idxs[a + 1], idxs[a]))
        if len(vals) % 2:
            nxt_v.append(vals[-1])
            nxt_i.append(idxs[-1])
        vals, idxs = nxt_v, nxt_i
    m = jnp.max(vals[0], axis=0, keepdims=True)
    first = jnp.min(jnp.where(vals[0] == m, idxs[0], big), axis=0, keepdims=True)
    return m, first, pos == first


def _half_ranks(s1, s2):
    n, width = s1.shape
    pos = lax.broadcasted_iota(jnp.int32, (n, width), 0).astype(F32)
    iota16 = lax.broadcasted_iota(jnp.int32, (PEER_TOPK, width), 0)

    def body(j, carry):
        s1c, v1c, i1c, s2c, v2c, r2c = carry
        m1, first1, hit1 = _argmax_step(s1c, pos, float(n))
        m2, _, hit2 = _argmax_step(s2c, pos, float(n))
        return (jnp.where(hit1, -jnp.inf, s1c), jnp.where(iota16 == j, m1, v1c), jnp.where(iota16 == j, first1, i1c),
                jnp.where(hit2, -jnp.inf, s2c), jnp.where(iota16 == j, m2, v2c), jnp.where(hit2, j, r2c))

    zero16 = jnp.zeros((PEER_TOPK, width), F32)
    rank0 = jnp.full((n, width), float(PEER_TOPK), F32)
    _, v1, i1, _, v2, r2 = lax.fori_loop(0, PEER_TOPK, body, (s1, zero16, zero16, s2, zero16, rank0), unroll=True)
    return pos, v1, i1, v2, r2


_CAND_ROWS = 16 + 7 * 8 + 8


def _cand_positions():
    p = [j2 for j2 in range(16)]
    p += [j1 * 16 + j2 for j1 in range(1, 8) for j2 in range(8)]
    p += [j1 * 16 for j1 in range(8, 16)]
    return np.tile(np.asarray(p, np.float32)[:, None], (1, LANES))


def _staircase(v1, v2, cpos):
    cand = jnp.concatenate([v1[0:1] + v2] + [v1[j:j + 1] + v2[0:8] for j in range(1, 8)] + [v1[8:16] + v2[0:1]],
                           axis=0)
    cmax = cand[0:1]

    def body(j, carry):
        s, sel = carry
        _, _, hit = _argmax_step(s, cpos, float(PEER_TOPK * PEER_TOPK))
        return jnp.where(hit, -jnp.inf, s), jnp.where(hit, 1.0, sel)

    _, sel = lax.fori_loop(0, PEER_TOPK, body, (cand, jnp.zeros_like(cand)), unroll=True)
    z = jnp.sum(sel * jnp.exp(cand - cmax), axis=0, keepdims=True)
    iota8 = lax.broadcasted_iota(jnp.int32, (8, cand.shape[1]), 0)
    low = jnp.zeros((8, cand.shape[1]), F32)
    low = jnp.where(iota8 == 0, jnp.sum(sel[0:16], axis=0, keepdims=True), low)
    for j in range(1, 8):
        low = jnp.where(iota8 == j, jnp.sum(sel[8 + 8 * j:16 + 8 * j], axis=0, keepdims=True), low)
    return jnp.concatenate([low, sel[72:80]], axis=0), z


def _peer_topk_kernel(q_ref, keys_ref, cpos_ref, e1_ref, lb1_ref, r2_ref, e2_ref, *, heads):
    nt = (((1,), (1,)), ((), ()))
    for h in range(heads):
        q = q_ref[:, h * 2 * LANES:(h + 1) * 2 * LANES]
        s1 = lax.dot_general(keys_ref[h, 0], q[:, :LANES], nt, preferred_element_type=F32)
        s2 = lax.dot_general(keys_ref[h, 1], q[:, LANES:], nt, preferred_element_type=F32)
        pos, v1, i1, v2, r2 = _half_ranks(s1, s2)
        counts, z = _staircase(v1, v2, cpos_ref[...])
        lb1 = jnp.zeros_like(s1)
        for j in range(PEER_TOPK):
            lb1 = jnp.where(pos == i1[j:j + 1], counts[j:j + 1], lb1)
        e1_ref[h] = jnp.exp(s1 - v1[0:1])
        lb1_ref[h] = lb1
        r2_ref[h] = r2
        e2_ref[h] = jnp.exp(s2 - v2[0:1]) / z


def peer_topk(q, sub_keys, layer):
    t = q.shape[0]
    heads = PEER_HEADS
    ob = pl.BlockSpec((heads, PEER_N_KEYS, LANES), lambda i, h: (h, 0, i))
    shp = jax.ShapeDtypeStruct((PEER_HEADS, PEER_N_KEYS, t), F32)
    return pl.pallas_call(
        functools.partial(_peer_topk_kernel, heads=heads),
        grid=(t // LANES, PEER_HEADS // heads),
        in_specs=[pl.BlockSpec((LANES, heads * 2 * LANES), lambda i, h: (i, h)),
                  pl.BlockSpec((None, heads, 2, PEER_N_KEYS, LANES), lambda i, h: (layer, h, 0, 0, 0)),
                  pl.BlockSpec((_CAND_ROWS, LANES), lambda i, h: (0, 0))],
        out_specs=[ob, ob, ob, ob],
        out_shape=[shp, shp, shp, shp],
        compiler_params=_params(("arbitrary", "arbitrary")),
        name="peer_topk",
    )(q, sub_keys, jnp.asarray(_cand_positions()))


_SQRT_HALF = float(np.sqrt(0.5))


def _peer_dense_kernel(ht_ref, u_ref, v_ref, e1_ref, lb1_ref, r2_ref, e2_ref, x_ref, g_ref, o_ref,
                       acc_ref, at_ref, wg_ref, *, tt, ec):
    i = pl.program_id(0)
    c = pl.program_id(1)

    @pl.when(c == 0)
    def _():
        acc_ref[...] = jnp.zeros_like(acc_ref)

    at_ref[...] = jnp.dot(u_ref[...], ht_ref[...], preferred_element_type=F32)
    n_a = ec // PEER_N_KEYS
    grp0 = pl.multiple_of((c * n_a) // 8 * 8, 8)
    off = (c * n_a) % 8

    def key_row(ref, h, al, lanes):
        blk = ref[h, pl.ds(grp0, 8), lanes]
        row = blk[al:al + 1]
        for o in range(n_a, 8, n_a):
            row = jnp.where(off == o, blk[o + al:o + al + 1], row)
        return row

    for al in range(n_a):
        rows = slice(al * PEER_N_KEYS, (al + 1) * PEER_N_KEYS)
        for lt in range(tt // LANES):
            lanes = slice(lt * LANES, (lt + 1) * LANES)
            gate = jnp.zeros((PEER_N_KEYS, LANES), F32)
            for h in range(PEER_HEADS):
                lb = key_row(lb1_ref, h, al, lanes)
                e1 = key_row(e1_ref, h, al, lanes)
                gate = gate + jnp.where(r2_ref[h, :, lanes] < lb, e2_ref[h, :, lanes], 0.0) * e1
            act = at_ref[rows, lanes]
            wg_ref[rows, lanes] = 0.5 * act * (1.0 + lax.erf(act * _SQRT_HALF)) * gate
    acc_ref[...] += jnp.dot(wg_ref[...].T.astype(BF16), v_ref[...], preferred_element_type=F32)

    @pl.when(c == pl.num_programs(1) - 1)
    def _():
        grp = _group_of_row(i * tt)
        o_ref[...] = x_ref[...] + g_ref[pl.ds(grp, 1), :] * acc_ref[...]


def peer_dense(ht, peer_u, peer_v, e1, lb1, r2, e2, x, mods, layer, gate_chunk):
    t = x.shape[0]
    tt, ec = PEER_TOK_TILE, PEER_EXP_TILE
    once = pl.Buffered(1)
    kb = pl.BlockSpec((PEER_HEADS, PEER_N_KEYS, tt), lambda i, c: (0, 0, i), pipeline_mode=once)
    return pl.pallas_call(
        functools.partial(_peer_dense_kernel, tt=tt, ec=ec),
        grid=(t // tt, PEER_N_EXPERTS // ec),
        in_specs=[pl.BlockSpec((D_MODEL, tt), lambda i, c: (0, i), pipeline_mode=once),
                  pl.BlockSpec((None, ec, D_MODEL), lambda i, c: (layer, c, 0)),
                  pl.BlockSpec((None, ec, D_MODEL), lambda i, c: (layer, c, 0)),
                  kb, kb, kb, kb,
                  pl.BlockSpec((tt, D_MODEL), lambda i, c: (i, 0), pipeline_mode=once),
                  pl.BlockSpec((None, 8, D_MODEL), lambda i, c: (layer, 0, gate_chunk))],
        out_specs=pl.BlockSpec((tt, D_MODEL), lambda i, c: (i, 0), pipeline_mode=once),
        out_shape=jax.ShapeDtypeStruct((t, D_MODEL), F32),
        scratch_shapes=[pltpu.VMEM((tt, D_MODEL), F32),
                        pltpu.VMEM((ec, tt), F32),
                        pltpu.VMEM((ec, tt), F32)],
        compiler_params=_params(("arbitrary", "arbitrary")),
        name="peer_dense",
    )(ht, peer_u, peer_v, e1, lb1, r2, e2, x, mods)


def _rope_tables():
    t = jnp.arange(LAT_SEQ)
    row = (t // GRID_W).astype(F32)
    col = (t % GRID_W).astype(F32)
    n_freq = MLA_ROPE // 4
    inv = ROPE_THETA ** (-jnp.arange(n_freq, dtype=F32) / n_freq)
    ang = jnp.concatenate([row[:, None] * inv, col[:, None] * inv], axis=-1)
    cos, sin = jnp.cos(ang), jnp.sin(ang)
    zero = jnp.zeros_like(cos)
    pad = jnp.zeros((LAT_SEQ, LANES - MLA_ROPE), F32)
    c_lat = jnp.concatenate([cos, cos, pad], axis=-1)
    s1_lat = jnp.concatenate([-sin, zero, pad], axis=-1)
    s2_lat = jnp.concatenate([zero, sin, pad], axis=-1)
    ones = jnp.concatenate([jnp.ones((1, MLA_ROPE), F32), jnp.zeros((1, LANES - MLA_ROPE), F32)], axis=-1)

    def full(lat, ident):
        return jnp.concatenate([jnp.broadcast_to(ident, (T_CTX, LANES)), jnp.tile(lat, (N_LAT_SEQ, 1)),
                                jnp.broadcast_to(ident, (T_CACHE, LANES))], axis=0)

    zeros = jnp.zeros((1, LANES), F32)
    return full(c_lat, ones), full(s1_lat, zeros), full(s2_lat, zeros)


def _na_bias_table(rel_bias):
    cq = np.arange(GRID_W)
    col_start = np.clip(cq - NA_KC // 2, 0, GRID_W - NA_KC)
    valid = (cq[None, :] >= col_start[:, None]) & (cq[None, :] < col_start[:, None] + NA_KC)
    coff = np.clip(cq[None, :] - cq[:, None], -(NA_KC - 1), NA_KC - 1) + (NA_KC - 1)
    onehot = (coff[:, :, None] == np.arange(2 * NA_KC - 1)[None, None, :]).astype(np.float32)
    toep = jnp.einsum('lhrc,qkc->lhrqk', rel_bias, jnp.asarray(onehot), precision=lax.Precision.HIGHEST)
    toep = jnp.where(jnp.asarray(valid)[None, None, None], toep, -jnp.inf)
    tabs = []
    for d in range(NA_KR):
        rows = toep[:, :, NA_KR - 1 - d:2 * NA_KR - 1 - d]
        tabs.append(rows.transpose(0, 1, 3, 2, 4).reshape(DEPTH, NA_HEADS, GRID_W, NA_LOCAL))
    return jnp.stack(tabs, axis=2)


def kernel(x_prompt, x_sample, cache_mla_ckv, cache_mla_kpe, cache_na_k, cache_na_v, c, c_ctx, ada_w, ada_b, norm_mix_g, norm_ffn_g, w_in, mla_q_norm_g, mla_w_uq, mla_kv_norm_g, mla_w_ukv, mla_q_head_g, mla_k_head_g, na_q_head_g, na_k_head_g, na_rel_bias, conv_w, w_out, peer_w_q, peer_sub_keys, peer_u, peer_v):
    lat_end = MLA_Q_LORA + MLA_KV_LORA + MLA_ROPE
    w_in_p = jnp.concatenate([w_in[..., :MLA_Q_LORA], w_in[..., lat_end:], w_in[..., MLA_Q_LORA:lat_end],
                              jnp.zeros((DEPTH, D_MODEL, D_IN_PAD - D_IN), F32)], axis=-1).astype(BF16)
    wuq_p = jnp.pad(mla_w_uq.reshape(DEPTH, MLA_Q_LORA, MLA_HEADS, MLA_QK_DIM),
                    ((0, 0), (0, 0), (0, 0), (0, MLA_HEAD_PAD - MLA_QK_DIM))
                    ).reshape(DEPTH, MLA_Q_LORA, MLA_HEADS * MLA_HEAD_PAD).astype(BF16)
    qhg_p = jnp.pad(mla_q_head_g, ((0, 0), (0, MLA_HEAD_PAD - MLA_QK_DIM))).reshape(DEPTH, 1, MLA_HEAD_PAD)
    wukv4 = mla_w_ukv.reshape(DEPTH, MLA_KV_LORA, MLA_HEADS, 2 * MLA_NOPE)
    wukv_p = jnp.concatenate([wukv4[..., :MLA_NOPE].reshape(DEPTH, MLA_KV_LORA, -1),
                              wukv4[..., MLA_NOPE:].reshape(DEPTH, MLA_KV_LORA, -1)], axis=-1).astype(BF16)
    khg_n = mla_k_head_g[:, :MLA_NOPE].reshape(DEPTH, 1, LANES)
    khg_p = jnp.pad(mla_k_head_g[:, MLA_NOPE:], ((0, 0), (0, LANES - MLA_ROPE))).reshape(DEPTH, 1, LANES)
    conv_w8 = jnp.pad(conv_w.transpose(0, 2, 1), ((0, 0), (0, 5), (0, 0)))
    w_out_b = w_out.astype(BF16)
    peer_wq_b = peer_w_q.astype(BF16)
    sub_keys_b = peer_sub_keys.astype(BF16)
    peer_u_b = peer_u.astype(BF16)
    peer_v_b = peer_v.astype(BF16)
    g_mix = norm_mix_g.reshape(DEPTH, 1, D_MODEL)
    g_ffn = norm_ffn_g.reshape(DEPTH, 1, D_MODEL)
    qng = mla_q_norm_g.reshape(DEPTH, 1, MLA_Q_LORA)
    kvng = mla_kv_norm_g.reshape(DEPTH, 1, MLA_KV_LORA)
    naqg = na_q_head_g.reshape(DEPTH, 1, NA_HEAD_DIM)
    nakg = na_k_head_g.reshape(DEPTH, 1, NA_HEAD_DIM)
    rope_c, rope_s1, rope_s2 = _rope_tables()
    na_bias_tab = _na_bias_table(na_rel_bias)
    cache_kpe_p = jnp.pad(cache_mla_kpe, ((0, 0), (0, 0), (0, 0), (0, LANES - MLA_ROPE)))
    cache_nak = cache_na_k.transpose(1, 0, 3, 2, 4).reshape(DEPTH, N_LAT_SEQ, PAST_LEN, NA_WIDTH).astype(BF16)
    cache_nav = cache_na_v.transpose(1, 0, 3, 2, 4).reshape(DEPTH, N_LAT_SEQ, PAST_LEN, NA_WIDTH).astype(BF16)

    cpad = jnp.concatenate([c_ctx[None, :], c, jnp.zeros((8 - 1 - N_LAT_SEQ, D_MODEL), F32)], axis=0)
    mods = ada_all(cpad, ada_w, ada_b)

    x = jnp.concatenate([x_prompt.reshape(T_CTX, D_MODEL), x_sample.reshape(T_LAT, D_MODEL)], axis=0)
    ckv_out, kpe_out, nak_out, nav_out = [], [], [], []
    mla_scale = MLA_QK_DIM ** -0.5
    na_scale = NA_HEAD_DIM ** -0.5

    ident_c = jnp.broadcast_to(rope_c[:1], (SEQ_TILE, LANES))
    ident_s = jnp.zeros((SEQ_TILE, LANES), F32)
    k_cache, v_cache = kv_prep_cache(
        cache_mla_ckv.transpose(1, 0, 2, 3).reshape(DEPTH, T_CACHE, MLA_KV_LORA),
        cache_kpe_p.transpose(1, 0, 2, 3).reshape(DEPTH, T_CACHE, LANES), ident_c, ident_s, wukv_p, khg_n, khg_p)

    for l in range(DEPTH):
        (z,) = mod_matmul(x, g_mix, mods, w_in_p, l, 0, 1, tn=1024, emit_ht=False)
        q_mla, ckv_n, q_na, k_na_f, v_na_f, k_na_b, v_na_b, k_mla, v_mla = proj_prep(
            z, rope_c, rope_s1, rope_s2, qng, wuq_p, qhg_p, kvng, naqg, nakg, wukv_p, khg_n, khg_p, l)

        o_mla_ctx = attention(q_mla, k_mla, v_mla, n_seq=N_CTX_SEQ, n_heads=MLA_HEADS, dq=MLA_HEAD_PAD,
                              sq_total=CTX_SEQ, sk=CTX_SEQ, q_row_off=0, k_row_off=0, scale=mla_scale)
        o_mla_lat = latent_mla_attention(q_mla, k_mla, v_mla, k_cache, v_cache, l, scale=mla_scale)
        o_na_ctx = attention(q_na, k_na_b, v_na_b, n_seq=N_CTX_SEQ, n_heads=NA_HEADS, dq=NA_HEAD_DIM,
                             sq_total=CTX_SEQ, sk=CTX_SEQ, q_row_off=0, k_row_off=0, scale=na_scale)
        o_na_lat = na_latent_attention(q_na, k_na_b, v_na_b, cache_nak, cache_nav, na_bias_tab, l, scale=na_scale)
        conv_ctx = short_conv(z, conv_w8, l, n_seq=N_CTX_SEQ, seq=CTX_SEQ, row_off=0)
        conv_lat = short_conv(z, conv_w8, l, n_seq=N_LAT_SEQ, seq=LAT_SEQ, row_off=T_CTX)

        x = out_matmul(x, mods, (o_mla_ctx, o_mla_lat), (o_na_ctx, o_na_lat), (conv_ctx, conv_lat), w_out_b, l, 2)

        q_peer, ht = mod_matmul(x, g_ffn, mods, peer_wq_b, l, 3, 4, tn=D_MODEL, emit_ht=True, out_dtype=BF16)
        e1, lb1, r2, e2 = peer_topk(q_peer, sub_keys_b, l)
        x = peer_dense(ht, peer_u_b, peer_v_b, e1, lb1, r2, e2, x, mods, l, 5)

        ckv_out.append(ckv_n[:T_CTX].reshape(N_CTX_SEQ, CTX_SEQ, MLA_KV_LORA))
        kpe_out.append(z[:T_CTX, Z_KPE_COL:Z_KPE_COL + MLA_ROPE].reshape(N_CTX_SEQ, CTX_SEQ, MLA_ROPE))
        nak_out.append(k_na_f[:N_CTX_SEQ])
        nav_out.append(v_na_f[:N_CTX_SEQ])

    y_prompt = x[:T_CTX].reshape(N_CTX_SEQ, CTX_SEQ, D_MODEL)
    y_sample = x[T_CTX:].reshape(N_LAT_SEQ, LAT_SEQ, D_MODEL)
    return (y_prompt, y_sample, jnp.stack(ckv_out, axis=1), jnp.stack(kpe_out, axis=1),
            jnp.stack(nak_out, axis=1), jnp.stack(nav_out, axis=1))
```

```python
import functools

import numpy as np
import jax
import jax.numpy as jnp
from jax import lax
from jax.experimental import pallas as pl
from jax.experimental.pallas import tpu as pltpu

F32 = jnp.float32
BF16 = jnp.bfloat16

D_MODEL = 2048
DEPTH = 4
N_CTX_SEQ = 32
CTX_SEQ = 256
N_LAT_SEQ = 2
LAT_SEQ = 1024
PAST_LEN = 512
T_CTX = N_CTX_SEQ * CTX_SEQ
T_LAT = N_LAT_SEQ * LAT_SEQ
T_ALL = T_CTX + T_LAT
T_CACHE = N_LAT_SEQ * PAST_LEN
GRID_W = 64
NORM_EPS = 1e-6
ROPE_THETA = 10000.0
ADA_CHUNKS = 6

MLA_HEADS = 8
MLA_NOPE = 128
MLA_ROPE = 64
MLA_QK_DIM = MLA_NOPE + MLA_ROPE
MLA_Q_LORA = 512
MLA_KV_LORA = 256
MLA_HEAD_PAD = 256
NA_HEADS = 4
NA_HEAD_DIM = 128
NA_WIDTH = NA_HEADS * NA_HEAD_DIM
NA_KR = 8
NA_KC = 16
CONV_CH = 512
Z_BLOCK = 512
Z_NAV_BLOCK = 3
Z_CONV_BLOCK = 4
Z_CKV_BLOCK = 7
D_IN_PAD = (Z_CKV_BLOCK + 1) * Z_BLOCK
Z_KPE_COL = Z_CKV_BLOCK * Z_BLOCK + MLA_KV_LORA
D_IN = MLA_Q_LORA + MLA_KV_LORA + MLA_ROPE + 3 * NA_WIDTH + 3 * CONV_CH
ATTN_Q_BLOCK = 256

PEER_HEADS = 8
PEER_N_KEYS = 128
PEER_N_EXPERTS = PEER_N_KEYS * PEER_N_KEYS
PEER_TOPK = 16

LANES = 128
SUBLANES = 8
VMEM_LIMIT = 60 * 1024 * 1024

ROW_TILE = 1024
SEQ_TILE = CTX_SEQ
PEER_TOK_TILE = 1024
PEER_EXP_TILE = 512


def _params(sem):
    return pltpu.CompilerParams(dimension_semantics=sem, vmem_limit_bytes=VMEM_LIMIT)


def _group_of_row(row):
    return jnp.where(row < T_CTX, 0, 1 + (row - T_CTX) // LAT_SEQ)


def _rms(x, g, n):
    ms = jnp.sum(x * x, axis=-1, keepdims=True) / n
    return x * lax.rsqrt(ms + NORM_EPS) * g


def _ada_kernel(c_ref, w_ref, b_ref, o_ref):
    c = c_ref[...]
    s = c * jax.nn.sigmoid(c)
    o_ref[...] = jnp.dot(s.astype(BF16), w_ref[...].astype(BF16), preferred_element_type=F32) + b_ref[...]


def ada_all(cpad, ada_w, ada_b):
    tn = 1536
    n = ADA_CHUNKS * D_MODEL
    return pl.pallas_call(
        _ada_kernel,
        grid=(DEPTH, n // tn),
        in_specs=[pl.BlockSpec((8, D_MODEL), lambda l, j: (0, 0)),
                  pl.BlockSpec((None, D_MODEL, tn), lambda l, j: (l, 0, j)),
                  pl.BlockSpec((None, 1, tn), lambda l, j: (l, 0, j))],
        out_specs=pl.BlockSpec((None, 8, tn), lambda l, j: (l, 0, j)),
        out_shape=jax.ShapeDtypeStruct((DEPTH, 8, n), F32),
        compiler_params=_params(("arbitrary", "arbitrary")),
        name="ada",
    )(cpad, ada_w, ada_b.reshape(DEPTH, 1, n))


def _mod_matmul_kernel(x_ref, g_ref, sh_ref, sc_ref, w_ref, z_ref, *rest, tm, emit_ht):
    if emit_ht:
        ht_ref, h_scr = rest
    else:
        (h_scr,) = rest
    i = pl.program_id(0)

    @pl.when(pl.program_id(1) == 0)
    def _():
        grp = _group_of_row(i * tm)
        y = _rms(x_ref[...], g_ref[...], D_MODEL)
        h = y * (1.0 + sc_ref[pl.ds(grp, 1), :]) + sh_ref[pl.ds(grp, 1), :]
        h_scr[...] = h.astype(BF16)
        if emit_ht:
            ht_ref[...] = h.T.astype(BF16)

    z_ref[...] = jnp.dot(h_scr[...], w_ref[...], preferred_element_type=F32).astype(z_ref.dtype)


def mod_matmul(x, gain, mods, w, layer, shift_chunk, scale_chunk, *, tn, emit_ht, tm=ROW_TILE, out_dtype=F32):
    t = x.shape[0]
    n = w.shape[-1]
    out_shape = [jax.ShapeDtypeStruct((t, n), out_dtype)]
    out_specs = [pl.BlockSpec((tm, tn), lambda i, j: (i, j))]
    if emit_ht:
        out_shape.append(jax.ShapeDtypeStruct((D_MODEL, t), BF16))
        out_specs.append(pl.BlockSpec((D_MODEL, tm), lambda i, j: (0, i)))
    res = pl.pallas_call(
        functools.partial(_mod_matmul_kernel, tm=tm, emit_ht=emit_ht),
        grid=(t // tm, n // tn),
        in_specs=[pl.BlockSpec((tm, D_MODEL), lambda i, j: (i, 0)),
                  pl.BlockSpec((None, 1, D_MODEL), lambda i, j: (layer, 0, 0)),
                  pl.BlockSpec((None, 8, D_MODEL), lambda i, j: (layer, 0, shift_chunk)),
                  pl.BlockSpec((None, 8, D_MODEL), lambda i, j: (layer, 0, scale_chunk)),
                  pl.BlockSpec((None, D_MODEL, tn), lambda i, j: (layer, 0, j))],
        out_specs=out_specs,
        out_shape=out_shape,
        scratch_shapes=[pltpu.VMEM((tm, D_MODEL), BF16)],
        compiler_params=_params(("arbitrary", "arbitrary")),
        name="mod_matmul_ht" if emit_ht else "mod_matmul",
    )(x, gain, mods, mods, w)
    return res


def _rope128(x, c, s1, s2):
    return x * c + pltpu.roll(x, 96, 1) * s1 + pltpu.roll(x, 32, 1) * s2


def _mla_keys_values(ckv_n, kpe, rc, rs1, rs2, wukv_ref, gn_ref, gp_ref, k_ref, v_ref):
    kv = jnp.dot(ckv_n.astype(BF16), wukv_ref[...], preferred_element_type=F32)
    pe_ss = jnp.sum(kpe * kpe, axis=-1, keepdims=True)
    for h in range(MLA_HEADS):
        kn = kv[:, h * MLA_NOPE:(h + 1) * MLA_NOPE]
        ms = (jnp.sum(kn * kn, axis=-1, keepdims=True) + pe_ss) / MLA_QK_DIM
        r = lax.rsqrt(ms + NORM_EPS)
        k_ref[:, h * MLA_HEAD_PAD:h * MLA_HEAD_PAD + LANES] = (kn * r * gn_ref[...]).astype(BF16)
        k_ref[:, h * MLA_HEAD_PAD + LANES:(h + 1) * MLA_HEAD_PAD] = (
            _rope128(kpe * r * gp_ref[...], rc, rs1, rs2).astype(BF16))
    v_ref[...] = kv[:, MLA_HEADS * MLA_NOPE:].astype(BF16)


def _proj_prep_kernel(cq_ref, naq_ref, nak_ref, nav_ref, ckv_ref, rc_ref, rs1_ref, rs2_ref,
                      qng_ref, wuq_ref, qhg_ref, kvng_ref, naqg_ref, nakg_ref, wukv_ref, gn_ref, gp_ref,
                      qmla_ref, ckvn_ref, qna_ref, knaf_ref, vnaf_ref, knab_ref, vnab_ref, k_ref, v_ref):
    cqn = _rms(cq_ref[...], qng_ref[...], MLA_Q_LORA)
    q = jnp.dot(cqn.astype(BF16), wuq_ref[...], preferred_element_type=F32)
    rc, rs1, rs2 = rc_ref[...], rs1_ref[...], rs2_ref[...]
    for h in range(MLA_HEADS):
        qh = _rms(q[:, h * MLA_HEAD_PAD:(h + 1) * MLA_HEAD_PAD], qhg_ref[...], MLA_QK_DIM)
        qmla_ref[:, h * MLA_HEAD_PAD:h * MLA_HEAD_PAD + LANES] = qh[:, :LANES].astype(BF16)
        qmla_ref[:, h * MLA_HEAD_PAD + LANES:(h + 1) * MLA_HEAD_PAD] = (
            _rope128(qh[:, LANES:], rc, rs1, rs2).astype(BF16))
    ckv_n = _rms(ckv_ref[:, :MLA_KV_LORA], kvng_ref[...], MLA_KV_LORA)
    ckvn_ref[...] = ckv_n
    _mla_keys_values(ckv_n, ckv_ref[:, MLA_KV_LORA:MLA_KV_LORA + LANES], rc, rs1, rs2,
                     wukv_ref, gn_ref, gp_ref, k_ref, v_ref)
    for h in range(NA_HEADS):
        sl = slice(h * NA_HEAD_DIM, (h + 1) * NA_HEAD_DIM)
        qna_ref[:, sl] = _rms(naq_ref[:, sl], naqg_ref[...], NA_HEAD_DIM).astype(BF16)
        kn = _rms(nak_ref[:, sl], nakg_ref[...], NA_HEAD_DIM)
        knaf_ref[h] = kn
        vnaf_ref[h] = nav_ref[:, sl]
        knab_ref[:, sl] = kn.astype(BF16)
    vnab_ref[...] = nav_ref[...].astype(BF16)


def proj_prep(z, rope_c, rope_s1, rope_s2, qng, wuq, qhg, kvng, naqg, nakg, wukv, gn, gp, layer):
    t = z.shape[0]
    tm = SEQ_TILE
    zb = lambda k: pl.BlockSpec((tm, Z_BLOCK), lambda i: (i, k))
    rb = pl.BlockSpec((tm, LANES), lambda i: (i, 0))
    wl = lambda *shape: pl.BlockSpec((None,) + shape, lambda i: (layer,) + (0,) * len(shape))
    ob = lambda w: pl.BlockSpec((tm, w), lambda i: (i, 0))
    cache_b = pl.BlockSpec((None, NA_HEADS, tm, NA_HEAD_DIM), lambda i: (i, 0, 0, 0))
    return pl.pallas_call(
        _proj_prep_kernel,
        grid=(t // tm,),
        in_specs=[zb(0), zb(1), zb(2), zb(Z_NAV_BLOCK), zb(Z_CKV_BLOCK), rb, rb, rb,
                  wl(1, MLA_Q_LORA), wl(MLA_Q_LORA, MLA_HEADS * MLA_HEAD_PAD), wl(1, MLA_HEAD_PAD),
                  wl(1, MLA_KV_LORA), wl(1, NA_HEAD_DIM), wl(1, NA_HEAD_DIM),
                  wl(MLA_KV_LORA, 2 * MLA_HEADS * MLA_NOPE), wl(1, LANES), wl(1, LANES)],
        out_specs=[ob(MLA_HEADS * MLA_HEAD_PAD), ob(MLA_KV_LORA), ob(NA_WIDTH), cache_b, cache_b, ob(NA_WIDTH),
                   ob(NA_WIDTH), ob(MLA_HEADS * MLA_HEAD_PAD), ob(MLA_HEADS * MLA_NOPE)],
        out_shape=[jax.ShapeDtypeStruct((t, MLA_HEADS * MLA_HEAD_PAD), BF16),
                   jax.ShapeDtypeStruct((t, MLA_KV_LORA), F32),
                   jax.ShapeDtypeStruct((t, NA_WIDTH), BF16),
                   jax.ShapeDtypeStruct((t // tm, NA_HEADS, tm, NA_HEAD_DIM), F32),
                   jax.ShapeDtypeStruct((t // tm, NA_HEADS, tm, NA_HEAD_DIM), F32),
                   jax.ShapeDtypeStruct((t, NA_WIDTH), BF16),
                   jax.ShapeDtypeStruct((t, NA_WIDTH), BF16),
                   jax.ShapeDtypeStruct((t, MLA_HEADS * MLA_HEAD_PAD), BF16),
                   jax.ShapeDtypeStruct((t, MLA_HEADS * MLA_NOPE), BF16)],
        compiler_params=_params(("arbitrary",)),
        name="proj_prep",
    )(z, z, z, z, z, rope_c, rope_s1, rope_s2, qng, wuq, qhg, kvng, naqg, nakg, wukv, gn, gp)


def _kv_prep_kernel(ckv_ref, kpe_ref, rc_ref, rs1_ref, rs2_ref, wukv_ref, gn_ref, gp_ref, k_ref, v_ref):
    _mla_keys_values(ckv_ref[...], kpe_ref[...], rc_ref[...], rs1_ref[...], rs2_ref[...],
                     wukv_ref, gn_ref, gp_ref, k_ref, v_ref)


def kv_prep_cache(ckv, kpe, ident_c, ident_s, wukv, gn, gp):
    tm = SEQ_TILE
    rb = lambda w: pl.BlockSpec((None, tm, w), lambda l, i: (l, i, 0))
    tb = pl.BlockSpec((tm, LANES), lambda l, i: (0, 0))
    wl = lambda *shape: pl.BlockSpec((None,) + shape, lambda l, i: (l,) + (0,) * len(shape))
    return pl.pallas_call(
        _kv_prep_kernel,
        grid=(DEPTH, T_CACHE // tm),
        in_specs=[rb(MLA_KV_LORA), rb(LANES), tb, tb, tb,
                  wl(MLA_KV_LORA, 2 * MLA_HEADS * MLA_NOPE), wl(1, LANES), wl(1, LANES)],
        out_specs=[rb(MLA_HEADS * MLA_HEAD_PAD), rb(MLA_HEADS * MLA_NOPE)],
        out_shape=[jax.ShapeDtypeStruct((DEPTH, T_CACHE, MLA_HEADS * MLA_HEAD_PAD), BF16),
                   jax.ShapeDtypeStruct((DEPTH, T_CACHE, MLA_HEADS * MLA_NOPE), BF16)],
        compiler_params=_params(("arbitrary", "arbitrary")),
        name="kv_prep_cache",
    )(ckv, kpe, ident_c, ident_s, ident_s, wukv, gn, gp)


_NT = (((1,), (1,)), ((), ()))


def _attn_kernel(q_ref, k_ref, v_ref, o_ref, *, scale, n_heads, dq):
    dv = NA_HEAD_DIM
    for h in range(n_heads):
        q = q_ref[:, h * dq:(h + 1) * dq]
        k = k_ref[:, h * dq:(h + 1) * dq]
        s = lax.dot_general(q, k, _NT, preferred_element_type=F32) * scale
        m = jnp.max(s, axis=-1, keepdims=True)
        p = jnp.exp(s - m)
        p = p / jnp.sum(p, axis=-1, keepdims=True)
        o = jnp.dot(p.astype(BF16), v_ref[:, h * dv:(h + 1) * dv], preferred_element_type=F32)
        o_ref[:, h * dv:(h + 1) * dv] = o.astype(BF16)


def attention(q, k, v, *, n_seq, n_heads, dq, sq_total, sk, q_row_off, k_row_off, scale):
    sq = ATTN_Q_BLOCK
    nq = sq_total // sq
    qoff = q_row_off // sq
    koff = k_row_off // sk
    dv = NA_HEAD_DIM
    return pl.pallas_call(
        functools.partial(_attn_kernel, scale=scale, n_heads=n_heads, dq=dq),
        grid=(n_seq, nq),
        in_specs=[pl.BlockSpec((sq, n_heads * dq), lambda b, qi: (qoff + b * nq + qi, 0)),
                  pl.BlockSpec((sk, n_heads * dq), lambda b, qi: (koff + b, 0)),
                  pl.BlockSpec((sk, n_heads * dv), lambda b, qi: (koff + b, 0))],
        out_specs=pl.BlockSpec((sq, n_heads * dv), lambda b, qi: (b * nq + qi, 0)),
        out_shape=jax.ShapeDtypeStruct((n_seq * sq_total, n_heads * dv), BF16),
        compiler_params=_params(("arbitrary", "arbitrary")),
        name="attention",
    )(q, k, v)


def _attn_cached_kernel(q_ref, k_ref, v_ref, kc_ref, vc_ref, o_ref, *, scale, n_heads, dq):
    dv = NA_HEAD_DIM
    for h in range(n_heads):
        qk = slice(h * dq, (h + 1) * dq)
        vv = slice(h * dv, (h + 1) * dv)
        q = q_ref[:, qk]
        s_own = lax.dot_general(q, k_ref[:, qk], _NT, preferred_element_type=F32) * scale
        s_ctx = lax.dot_general(q, kc_ref[:, qk], _NT, preferred_element_type=F32) * scale
        m = jnp.maximum(jnp.max(s_own, axis=-1, keepdims=True), jnp.max(s_ctx, axis=-1, keepdims=True))
        p_own = jnp.exp(s_own - m)
        p_ctx = jnp.exp(s_ctx - m)
        denom = jnp.sum(p_own, axis=-1, keepdims=True) + jnp.sum(p_ctx, axis=-1, keepdims=True)
        o = jnp.dot((p_ctx / denom).astype(BF16), vc_ref[:, vv], preferred_element_type=F32)
        o = o + jnp.dot((p_own / denom).astype(BF16), v_ref[:, vv], preferred_element_type=F32)
        o_ref[:, vv] = o.astype(BF16)


def latent_mla_attention(q, k, v, k_cache, v_cache, layer, *, scale):
    sq = ATTN_Q_BLOCK
    nq = LAT_SEQ // sq
    qoff = T_CTX // sq
    koff = T_CTX // LAT_SEQ
    wq = MLA_HEADS * MLA_HEAD_PAD
    wv = MLA_HEADS * NA_HEAD_DIM
    return pl.pallas_call(
        functools.partial(_attn_cached_kernel, scale=scale, n_heads=MLA_HEADS, dq=MLA_HEAD_PAD),
        grid=(N_LAT_SEQ, nq),
        in_specs=[pl.BlockSpec((sq, wq), lambda b, qi: (qoff + b * nq + qi, 0)),
                  pl.BlockSpec((LAT_SEQ, wq), lambda b, qi: (koff + b, 0)),
                  pl.BlockSpec((LAT_SEQ, wv), lambda b, qi: (koff + b, 0)),
                  pl.BlockSpec((None, PAST_LEN, wq), lambda b, qi: (layer, b, 0)),
                  pl.BlockSpec((None, PAST_LEN, wv), lambda b, qi: (layer, b, 0))],
        out_specs=pl.BlockSpec((sq, wv), lambda b, qi: (b * nq + qi, 0)),
        out_shape=jax.ShapeDtypeStruct((T_LAT, wv), BF16),
        compiler_params=_params(("arbitrary", "arbitrary")),
        name="latent_mla",
    )(q, k, v, k_cache, v_cache)


NA_ROWS = LAT_SEQ // GRID_W
NA_LOCAL = NA_KR * GRID_W


def _na_lat_kernel(q_ref, k_ref, v_ref, kc_ref, vc_ref, b_ref, o_ref, *, scale):
    rq = pl.program_id(1)
    row_start = jnp.clip(rq - NA_KR // 2, 0, NA_ROWS - NA_KR)
    start = pl.multiple_of(row_start * GRID_W, GRID_W)
    for h in range(NA_HEADS):
        cols = slice(h * NA_HEAD_DIM, (h + 1) * NA_HEAD_DIM)
        q = q_ref[:, cols]
        k_loc = k_ref[pl.ds(start, NA_LOCAL), cols]
        v_loc = v_ref[pl.ds(start, NA_LOCAL), cols]
        s_loc = lax.dot_general(q, k_loc, _NT, preferred_element_type=F32) * scale + b_ref[h, rq - row_start]
        s_ctx = lax.dot_general(q, kc_ref[:, cols], _NT, preferred_element_type=F32) * scale
        m = jnp.maximum(jnp.max(s_loc, axis=-1, keepdims=True), jnp.max(s_ctx, axis=-1, keepdims=True))
        p_loc = jnp.exp(s_loc - m)
        p_ctx = jnp.exp(s_ctx - m)
        denom = jnp.sum(p_loc, axis=-1, keepdims=True) + jnp.sum(p_ctx, axis=-1, keepdims=True)
        o = jnp.dot((p_loc / denom).astype(BF16), v_loc, preferred_element_type=F32)
        o = o + jnp.dot((p_ctx / denom).astype(BF16), vc_ref[:, cols], preferred_element_type=F32)
        o_ref[:, cols] = o.astype(BF16)


def na_latent_attention(q, k, v, k_cache, v_cache, bias_tab, layer, *, scale):
    qoff = T_CTX // GRID_W
    koff = T_CTX // LAT_SEQ
    w = NA_WIDTH
    return pl.pallas_call(
        functools.partial(_na_lat_kernel, scale=scale),
        grid=(N_LAT_SEQ, NA_ROWS),
        in_specs=[pl.BlockSpec((GRID_W, w), lambda b, r: (qoff + b * NA_ROWS + r, 0)),
                  pl.BlockSpec((LAT_SEQ, w), lambda b, r: (koff + b, 0)),
                  pl.BlockSpec((LAT_SEQ, w), lambda b, r: (koff + b, 0)),
                  pl.BlockSpec((None, None, PAST_LEN, w), lambda b, r: (layer, b, 0, 0)),
                  pl.BlockSpec((None, None, PAST_LEN, w), lambda b, r: (layer, b, 0, 0)),
                  pl.BlockSpec((None, NA_HEADS, NA_KR, GRID_W, NA_LOCAL), lambda b, r: (layer, 0, 0, 0, 0))],
        out_specs=pl.BlockSpec((GRID_W, w), lambda b, r: (b * NA_ROWS + r, 0)),
        out_shape=jax.ShapeDtypeStruct((T_LAT, NA_WIDTH), BF16),
        compiler_params=_params(("arbitrary", "arbitrary")),
        name="na_latent",
    )(q, k, v, k_cache, v_cache, bias_tab)


def _conv_kernel(gb_ref, gc_ref, u_ref, w_ref, o_ref):
    gu = gc_ref[...] * u_ref[...]
    s = gu.shape[0]
    row = lax.broadcasted_iota(jnp.int32, gu.shape, 0)
    prev = jnp.where(row == 0, 0.0, pltpu.roll(gu, 1, 0))
    nxt = jnp.where(row == s - 1, 0.0, pltpu.roll(gu, s - 1, 0))
    y = prev * w_ref[0:1, :] + gu * w_ref[1:2, :] + nxt * w_ref[2:3, :]
    o_ref[...] = (gb_ref[...] * y).astype(BF16)


def short_conv(z, conv_w8, layer, *, n_seq, seq, row_off):
    off = row_off // seq
    zb = lambda k: pl.BlockSpec((seq, CONV_CH), lambda i: (off + i, k))
    return pl.pallas_call(
        _conv_kernel,
        grid=(n_seq,),
        in_specs=[zb(Z_CONV_BLOCK), zb(Z_CONV_BLOCK + 1), zb(Z_CONV_BLOCK + 2),
                  pl.BlockSpec((None, 8, CONV_CH), lambda i: (layer, 0, 0))],
        out_specs=pl.BlockSpec((seq, CONV_CH), lambda i: (i, 0)),
        out_shape=jax.ShapeDtypeStruct((n_seq * seq, CONV_CH), BF16),
        compiler_params=_params(("arbitrary",)),
        name="short_conv",
    )(z, z, z, conv_w8)


def _out_matmul_kernel(x_ref, g_ref, a0c_ref, a0l_ref, a1c_ref, a1l_ref, a2c_ref, a2l_ref,
                       w0_ref, w1_ref, w2_ref, o_ref, *, tm):
    row = pl.program_id(0) * tm
    grp = _group_of_row(row)
    is_ctx = row < T_CTX
    acc = jnp.dot(jnp.where(is_ctx, a0c_ref[...], a0l_ref[...]), w0_ref[...], preferred_element_type=F32)
    acc = acc + jnp.dot(jnp.where(is_ctx, a1c_ref[...], a1l_ref[...]), w1_ref[...], preferred_element_type=F32)
    acc = acc + jnp.dot(jnp.where(is_ctx, a2c_ref[...], a2l_ref[...]), w2_ref[...], preferred_element_type=F32)
    o_ref[...] = x_ref[...] + g_ref[pl.ds(grp, 1), :] * acc


def out_matmul(x, mods, o_mla, o_na, conv, w_out, layer, gate_chunk):
    t = x.shape[0]
    tm, tn = ROW_TILE, 1024
    nj = D_MODEL // tn
    w_mla = MLA_HEADS * MLA_NOPE
    n_ctx = T_CTX // tm
    ctx = lambda w: pl.BlockSpec((tm, w), lambda i, j: (jnp.minimum(i, n_ctx - 1), 0))
    lat = lambda w: pl.BlockSpec((tm, w), lambda i, j: (jnp.maximum(i - n_ctx, 0), 0))
    return pl.pallas_call(
        functools.partial(_out_matmul_kernel, tm=tm),
        grid=(t // tm, nj),
        in_specs=[pl.BlockSpec((tm, tn), lambda i, j: (i, j)),
                  pl.BlockSpec((None, 8, tn), lambda i, j: (layer, 0, gate_chunk * nj + j)),
                  ctx(w_mla), lat(w_mla), ctx(NA_WIDTH), lat(NA_WIDTH), ctx(CONV_CH), lat(CONV_CH),
                  pl.BlockSpec((None, w_mla, tn), lambda i, j: (layer, 0, j)),
                  pl.BlockSpec((None, NA_WIDTH, tn), lambda i, j: (layer, w_mla // NA_WIDTH, j)),
                  pl.BlockSpec((None, CONV_CH, tn), lambda i, j: (layer, (w_mla + NA_WIDTH) // CONV_CH, j))],
        out_specs=pl.BlockSpec((tm, tn), lambda i, j: (i, j)),
        out_shape=jax.ShapeDtypeStruct((t, D_MODEL), F32),
        compiler_params=_params(("arbitrary", "arbitrary")),
        name="out_matmul",
    )(x, mods, *o_mla, *o_na, *conv, w_out, w_out, w_out)


def _argmax_step(s, pos, big):
    vals = [s[k:k + SUBLANES] for k in range(0, s.shape[0], SUBLANES)]
    idxs = [pos[k:k + SUBLANES] for k in range(0, s.shape[0], SUBLANES)]
    while len(vals) > 1:
        nxt_v, nxt_i = [], []
        for a in range(0, len(vals) - 1, 2):
            take_hi = vals[a + 1] > vals[a]
            nxt_v.append(jnp.maximum(vals[a], vals[a + 1]))
            nxt_i.append(jnp.where(take_hi, idxs[a + 1], idxs[a]))
        if len(vals) % 2:
            nxt_v.append(vals[-1])
            nxt_i.append(idxs[-1])
        vals, idxs = nxt_v, nxt_i
    m = jnp.max(vals[0], axis=0, keepdims=True)
    first = jnp.min(jnp.where(vals[0] == m, idxs[0], big), axis=0, keepdims=True)
    return m, first, pos == first


def _half_ranks(s1, s2):
    n, width = s1.shape
    pos = lax.broadcasted_iota(jnp.int32, (n, width), 0).astype(F32)
    iota16 = lax.broadcasted_iota(jnp.int32, (PEER_TOPK, width), 0)

    def body(j, carry):
        s1c, v1c, i1c, s2c, v2c, r2c = carry
        m1, first1, hit1 = _argmax_step(s1c, pos, float(n))
        m2, _, hit2 = _argmax_step(s2c, pos, float(n))
        return (jnp.where(hit1, -jnp.inf, s1c), jnp.where(iota16 == j, m1, v1c), jnp.where(iota16 == j, first1, i1c),
                jnp.where(hit2, -jnp.inf, s2c), jnp.where(iota16 == j, m2, v2c), jnp.where(hit2, j, r2c))

    zero16 = jnp.zeros((PEER_TOPK, width), F32)
    rank0 = jnp.full((n, width), float(PEER_TOPK), F32)
    _, v1, i1, _, v2, r2 = lax.fori_loop(0, PEER_TOPK, body, (s1, zero16, zero16, s2, zero16, rank0), unroll=True)
    return pos, v1, i1, v2, r2


_CAND_ROWS = 16 + 7 * 8 + 8


def _cand_positions():
    p = [j2 for j2 in range(16)]
    p += [j1 * 16 + j2 for j1 in range(1, 8) for j2 in range(8)]
    p += [j1 * 16 for j1 in range(8, 16)]
    return np.tile(np.asarray(p, np.float32)[:, None], (1, LANES))


def _staircase(v1, v2, cpos):
    cand = jnp.concatenate([v1[0:1] + v2] + [v1[j:j + 1] + v2[0:8] for j in range(1, 8)] + [v1[8:16] + v2[0:1]],
                           axis=0)
    cmax = cand[0:1]
    row16 = lax.broadcasted_iota(jnp.int32, (PEER_TOPK, cand.shape[1]), 0).astype(F32)

    def body(j, carry):
        s, counts, z = carry
        m, first, hit = _argmax_step(s, cpos, float(PEER_TOPK * PEER_TOPK))
        j1 = jnp.floor(first * (1.0 / PEER_TOPK))
        counts = jnp.where(row16 == j1, counts + 1.0, counts)
        return jnp.where(hit, -jnp.inf, s), counts, z + jnp.exp(m - cmax)

    _, counts, z = lax.fori_loop(0, PEER_TOPK, body, (cand, jnp.zeros_like(row16), jnp.zeros_like(cmax)),
                                 unroll=True)
    return counts, z


def _peer_topk_kernel(q_ref, keys_ref, cpos_ref, e1_ref, lb1_ref, r2_ref, e2_ref, *, heads):
    nt = (((1,), (1,)), ((), ()))
    for h in range(heads):
        q = q_ref[:, h * 2 * LANES:(h + 1) * 2 * LANES]
        s1 = lax.dot_general(keys_ref[h, 0], q[:, :LANES], nt, preferred_element_type=F32)
        s2 = lax.dot_general(keys_ref[h, 1], q[:, LANES:], nt, preferred_element_type=F32)
        pos, v1, i1, v2, r2 = _half_ranks(s1, s2)
        counts, z = _staircase(v1, v2, cpos_ref[...])
        lb1 = jnp.zeros_like(s1)
        for j in range(PEER_TOPK):
            lb1 = jnp.where(pos == i1[j:j + 1], counts[j:j + 1], lb1)
        e1_ref[h] = jnp.exp(s1 - v1[0:1])
        lb1_ref[h] = lb1
        r2_ref[h] = r2
        e2_ref[h] = jnp.exp(s2 - v2[0:1]) / z


def peer_topk(q, sub_keys, layer):
    t = q.shape[0]
    heads = PEER_HEADS
    ob = pl.BlockSpec((heads, PEER_N_KEYS, LANES), lambda i, h: (h, 0, i))
    shp = jax.ShapeDtypeStruct((PEER_HEADS, PEER_N_KEYS, t), F32)
    return pl.pallas_call(
        functools.partial(_peer_topk_kernel, heads=heads),
        grid=(t // LANES, PEER_HEADS // heads),
        in_specs=[pl.BlockSpec((LANES, heads * 2 * LANES), lambda i, h: (i, h)),
                  pl.BlockSpec((None, heads, 2, PEER_N_KEYS, LANES), lambda i, h: (layer, h, 0, 0, 0)),
                  pl.BlockSpec((_CAND_ROWS, LANES), lambda i, h: (0, 0))],
        out_specs=[ob, ob, ob, ob],
        out_shape=[shp, shp, shp, shp],
        compiler_params=_params(("arbitrary", "arbitrary")),
        name="peer_topk",
    )(q, sub_keys, jnp.asarray(_cand_positions()))


_SQRT_HALF = float(np.sqrt(0.5))


def _peer_dense_kernel(ht_ref, u_ref, v_ref, e1_ref, lb1_ref, r2_ref, e2_ref, x_ref, g_ref, o_ref,
                       acc_ref, at_ref, wg_ref, *, tt, ec):
    i = pl.program_id(0)
    c = pl.program_id(1)

    @pl.when(c == 0)
    def _():
        acc_ref[...] = jnp.zeros_like(acc_ref)

    at_ref[...] = jnp.dot(u_ref[...], ht_ref[...], preferred_element_type=F32)
    n_a = ec // PEER_N_KEYS
    grp0 = pl.multiple_of((c * n_a) // 8 * 8, 8)
    off = (c * n_a) % 8

    def key_row(ref, h, al, lanes):
        blk = ref[h, pl.ds(grp0, 8), lanes]
        row = blk[al:al + 1]
        for o in range(n_a, 8, n_a):
            row = jnp.where(off == o, blk[o + al:o + al + 1], row)
        return row

    for al in range(n_a):
        rows = slice(al * PEER_N_KEYS, (al + 1) * PEER_N_KEYS)
        for lt in range(tt // LANES):
            lanes = slice(lt * LANES, (lt + 1) * LANES)
            gate = None
            for h in range(PEER_HEADS):
                lb = key_row(lb1_ref, h, al, lanes)
                e1 = key_row(e1_ref, h, al, lanes)
                term = jnp.where(r2_ref[h, :, lanes] < lb, e2_ref[h, :, lanes], 0.0) * e1
                gate = term if gate is None else gate + term
            act = at_ref[rows, lanes]
            wg_ref[rows, lanes] = 0.5 * act * (1.0 + lax.erf(act * _SQRT_HALF)) * gate
    acc_ref[...] += jnp.dot(wg_ref[...].T.astype(BF16), v_ref[...], preferred_element_type=F32)

    @pl.when(c == pl.num_programs(1) - 1)
    def _():
        grp = _group_of_row(i * tt)
        o_ref[...] = x_ref[...] + g_ref[pl.ds(grp, 1), :] * acc_ref[...]


def peer_dense(ht, peer_u, peer_v, e1, lb1, r2, e2, x, mods, layer, gate_chunk):
    t = x.shape[0]
    tt, ec = PEER_TOK_TILE, PEER_EXP_TILE
    once = pl.Buffered(1)
    kb = pl.BlockSpec((PEER_HEADS, PEER_N_KEYS, tt), lambda i, c: (0, 0, i), pipeline_mode=once)
    return pl.pallas_call(
        functools.partial(_peer_dense_kernel, tt=tt, ec=ec),
        grid=(t // tt, PEER_N_EXPERTS // ec),
        in_specs=[pl.BlockSpec((D_MODEL, tt), lambda i, c: (0, i), pipeline_mode=once),
                  pl.BlockSpec((None, ec, D_MODEL), lambda i, c: (layer, c, 0)),
                  pl.BlockSpec((None, ec, D_MODEL), lambda i, c: (layer, c, 0)),
                  kb, kb, kb, kb,
                  pl.BlockSpec((tt, D_MODEL), lambda i, c: (i, 0), pipeline_mode=once),
                  pl.BlockSpec((None, 8, D_MODEL), lambda i, c: (layer, 0, gate_chunk))],
        out_specs=pl.BlockSpec((tt, D_MODEL), lambda i, c: (i, 0), pipeline_mode=once),
        out_shape=jax.ShapeDtypeStruct((t, D_MODEL), F32),
        scratch_shapes=[pltpu.VMEM((tt, D_MODEL), F32),
                        pltpu.VMEM((ec, tt), F32),
                        pltpu.VMEM((ec, tt), F32)],
        compiler_params=_params(("arbitrary", "arbitrary")),
        name="peer_dense",
    )(ht, peer_u, peer_v, e1, lb1, r2, e2, x, mods)


def _rope_tables():
    t = jnp.arange(LAT_SEQ)
    row = (t // GRID_W).astype(F32)
    col = (t % GRID_W).astype(F32)
    n_freq = MLA_ROPE // 4
    inv = ROPE_THETA ** (-jnp.arange(n_freq, dtype=F32) / n_freq)
    ang = jnp.concatenate([row[:, None] * inv, col[:, None] * inv], axis=-1)
    cos, sin = jnp.cos(ang), jnp.sin(ang)
    zero = jnp.zeros_like(cos)
    pad = jnp.zeros((LAT_SEQ, LANES - MLA_ROPE), F32)
    c_lat = jnp.concatenate([cos, cos, pad], axis=-1)
    s1_lat = jnp.concatenate([-sin, zero, pad], axis=-1)
    s2_lat = jnp.concatenate([zero, sin, pad], axis=-1)
    ones = jnp.concatenate([jnp.ones((1, MLA_ROPE), F32), jnp.zeros((1, LANES - MLA_ROPE), F32)], axis=-1)

    def full(lat, ident):
        return jnp.concatenate([jnp.broadcast_to(ident, (T_CTX, LANES)), jnp.tile(lat, (N_LAT_SEQ, 1)),
                                jnp.broadcast_to(ident, (T_CACHE, LANES))], axis=0)

    zeros = jnp.zeros((1, LANES), F32)
    return full(c_lat, ones), full(s1_lat, zeros), full(s2_lat, zeros)


def _na_bias_table(rel_bias):
    cq = np.arange(GRID_W)
    col_start = np.clip(cq - NA_KC // 2, 0, GRID_W - NA_KC)
    valid = (cq[None, :] >= col_start[:, None]) & (cq[None, :] < col_start[:, None] + NA_KC)
    coff = np.clip(cq[None, :] - cq[:, None], -(NA_KC - 1), NA_KC - 1) + (NA_KC - 1)
    onehot = (coff[:, :, None] == np.arange(2 * NA_KC - 1)[None, None, :]).astype(np.float32)
    toep = jnp.einsum('lhrc,qkc->lhrqk', rel_bias, jnp.asarray(onehot), precision=lax.Precision.HIGHEST)
    toep = jnp.where(jnp.asarray(valid)[None, None, None], toep, -jnp.inf)
    tabs = []
    for d in range(NA_KR):
        rows = toep[:, :, NA_KR - 1 - d:2 * NA_KR - 1 - d]
        tabs.append(rows.transpose(0, 1, 3, 2, 4).reshape(DEPTH, NA_HEADS, GRID_W, NA_LOCAL))
    return jnp.stack(tabs, axis=2)


def kernel(x_prompt, x_sample, cache_mla_ckv, cache_mla_kpe, cache_na_k, cache_na_v, c, c_ctx, ada_w, ada_b, norm_mix_g, norm_ffn_g, w_in, mla_q_norm_g, mla_w_uq, mla_kv_norm_g, mla_w_ukv, mla_q_head_g, mla_k_head_g, na_q_head_g, na_k_head_g, na_rel_bias, conv_w, w_out, peer_w_q, peer_sub_keys, peer_u, peer_v):
    lat_end = MLA_Q_LORA + MLA_KV_LORA + MLA_ROPE
    w_in_p = jnp.concatenate([w_in[..., :MLA_Q_LORA], w_in[..., lat_end:], w_in[..., MLA_Q_LORA:lat_end],
                              jnp.zeros((DEPTH, D_MODEL, D_IN_PAD - D_IN), F32)], axis=-1).astype(BF16)
    wuq_p = jnp.pad(mla_w_uq.reshape(DEPTH, MLA_Q_LORA, MLA_HEADS, MLA_QK_DIM),
                    ((0, 0), (0, 0), (0, 0), (0, MLA_HEAD_PAD - MLA_QK_DIM))
                    ).reshape(DEPTH, MLA_Q_LORA, MLA_HEADS * MLA_HEAD_PAD).astype(BF16)
    qhg_p = jnp.pad(mla_q_head_g, ((0, 0), (0, MLA_HEAD_PAD - MLA_QK_DIM))).reshape(DEPTH, 1, MLA_HEAD_PAD)
    wukv4 = mla_w_ukv.reshape(DEPTH, MLA_KV_LORA, MLA_HEADS, 2 * MLA_NOPE)
    wukv_p = jnp.concatenate([wukv4[..., :MLA_NOPE].reshape(DEPTH, MLA_KV_LORA, -1),
                              wukv4[..., MLA_NOPE:].reshape(DEPTH, MLA_KV_LORA, -1)], axis=-1).astype(BF16)
    khg_n = mla_k_head_g[:, :MLA_NOPE].reshape(DEPTH, 1, LANES)
    khg_p = jnp.pad(mla_k_head_g[:, MLA_NOPE:], ((0, 0), (0, LANES - MLA_ROPE))).reshape(DEPTH, 1, LANES)
    conv_w8 = jnp.pad(conv_w.transpose(0, 2, 1), ((0, 0), (0, 5), (0, 0)))
    w_out_b = w_out.astype(BF16)
    peer_wq_b = peer_w_q.astype(BF16)
    sub_keys_b = peer_sub_keys.astype(BF16)
    peer_u_b = peer_u.astype(BF16)
    peer_v_b = peer_v.astype(BF16)
    g_mix = norm_mix_g.reshape(DEPTH, 1, D_MODEL)
    g_ffn = norm_ffn_g.reshape(DEPTH, 1, D_MODEL)
    qng = mla_q_norm_g.reshape(DEPTH, 1, MLA_Q_LORA)
    kvng = mla_kv_norm_g.reshape(DEPTH, 1, MLA_KV_LORA)
    naqg = na_q_head_g.reshape(DEPTH, 1, NA_HEAD_DIM)
    nakg = na_k_head_g.reshape(DEPTH, 1, NA_HEAD_DIM)
    rope_c, rope_s1, rope_s2 = _rope_tables()
    na_bias_tab = _na_bias_table(na_rel_bias)
    cache_kpe_p = jnp.pad(cache_mla_kpe, ((0, 0), (0, 0), (0, 0), (0, LANES - MLA_ROPE)))
    cache_nak = cache_na_k.transpose(1, 0, 3, 2, 4).reshape(DEPTH, N_LAT_SEQ, PAST_LEN, NA_WIDTH).astype(BF16)
    cache_nav = cache_na_v.transpose(1, 0, 3, 2, 4).reshape(DEPTH, N_LAT_SEQ, PAST_LEN, NA_WIDTH).astype(BF16)

    cpad = jnp.concatenate([c_ctx[None, :], c, jnp.zeros((8 - 1 - N_LAT_SEQ, D_MODEL), F32)], axis=0)
    mods = ada_all(cpad, ada_w, ada_b)

    x = jnp.concatenate([x_prompt.reshape(T_CTX, D_MODEL), x_sample.reshape(T_LAT, D_MODEL)], axis=0)
    ckv_out, kpe_out, nak_out, nav_out = [], [], [], []
    mla_scale = MLA_QK_DIM ** -0.5
    na_scale = NA_HEAD_DIM ** -0.5

    ident_c = jnp.broadcast_to(rope_c[:1], (SEQ_TILE, LANES))
    ident_s = jnp.zeros((SEQ_TILE, LANES), F32)
    k_cache, v_cache = kv_prep_cache(
        cache_mla_ckv.transpose(1, 0, 2, 3).reshape(DEPTH, T_CACHE, MLA_KV_LORA),
        cache_kpe_p.transpose(1, 0, 2, 3).reshape(DEPTH, T_CACHE, LANES), ident_c, ident_s, wukv_p, khg_n, khg_p)

    for l in range(DEPTH):
        (z,) = mod_matmul(x, g_mix, mods, w_in_p, l, 0, 1, tn=1024, emit_ht=False)
        q_mla, ckv_n, q_na, k_na_f, v_na_f, k_na_b, v_na_b, k_mla, v_mla = proj_prep(
            z, rope_c, rope_s1, rope_s2, qng, wuq_p, qhg_p, kvng, naqg, nakg, wukv_p, khg_n, khg_p, l)

        o_mla_ctx = attention(q_mla, k_mla, v_mla, n_seq=N_CTX_SEQ, n_heads=MLA_HEADS, dq=MLA_HEAD_PAD,
                              sq_total=CTX_SEQ, sk=CTX_SEQ, q_row_off=0, k_row_off=0, scale=mla_scale)
        o_mla_lat = latent_mla_attention(q_mla, k_mla, v_mla, k_cache, v_cache, l, scale=mla_scale)
        o_na_ctx = attention(q_na, k_na_b, v_na_b, n_seq=N_CTX_SEQ, n_heads=NA_HEADS, dq=NA_HEAD_DIM,
                             sq_total=CTX_SEQ, sk=CTX_SEQ, q_row_off=0, k_row_off=0, scale=na_scale)
        o_na_lat = na_latent_attention(q_na, k_na_b, v_na_b, cache_nak, cache_nav, na_bias_tab, l, scale=na_scale)
        conv_ctx = short_conv(z, conv_w8, l, n_seq=N_CTX_SEQ, seq=CTX_SEQ, row_off=0)
        conv_lat = short_conv(z, conv_w8, l, n_seq=N_LAT_SEQ, seq=LAT_SEQ, row_off=T_CTX)

        x = out_matmul(x, mods, (o_mla_ctx, o_mla_lat), (o_na_ctx, o_na_lat), (conv_ctx, conv_lat), w_out_b, l, 2)

        q_peer, ht = mod_matmul(x, g_ffn, mods, peer_wq_b, l, 3, 4, tn=D_MODEL, emit_ht=True, out_dtype=BF16)
        e1, lb1, r2, e2 = peer_topk(q_peer, sub_keys_b, l)
        x = peer_dense(ht, peer_u_b, peer_v_b, e1, lb1, r2, e2, x, mods, l, 5)

        ckv_out.append(ckv_n[:T_CTX].reshape(N_CTX_SEQ, CTX_SEQ, MLA_KV_LORA))
        kpe_out.append(z[:T_CTX, Z_KPE_COL:Z_KPE_COL + MLA_ROPE].reshape(N_CTX_SEQ, CTX_SEQ, MLA_ROPE))
        nak_out.append(k_na_f[:N_CTX_SEQ])
        nav_out.append(v_na_f[:N_CTX_SEQ])

    y_prompt = x[:T_CTX].reshape(N_CTX_SEQ, CTX_SEQ, D_MODEL)
    y_sample = x[T_CTX:].reshape(N_LAT_SEQ, LAT_SEQ, D_MODEL)
    return (y_prompt, y_sample, jnp.stack(ckv_out, axis=1), jnp.stack(kpe_out, axis=1),
            jnp.stack(nak_out, axis=1), jnp.stack(nav_out, axis=1))
```

```python
import functools

import numpy as np
import jax
import jax.numpy as jnp
from jax import lax
from jax.experimental import pallas as pl
from jax.experimental.pallas import tpu as pltpu

F32 = jnp.float32
BF16 = jnp.bfloat16

D_MODEL = 2048
DEPTH = 4
N_CTX_SEQ = 32
CTX_SEQ = 256
N_LAT_SEQ = 2
LAT_SEQ = 1024
PAST_LEN = 512
T_CTX = N_CTX_SEQ * CTX_SEQ
T_LAT = N_LAT_SEQ * LAT_SEQ
T_ALL = T_CTX + T_LAT
T_CACHE = N_LAT_SEQ * PAST_LEN
GRID_W = 64
NORM_EPS = 1e-6
ROPE_THETA = 10000.0
ADA_CHUNKS = 6

MLA_HEADS = 8
MLA_NOPE = 128
MLA_ROPE = 64
MLA_QK_DIM = MLA_NOPE + MLA_ROPE
MLA_Q_LORA = 512
MLA_KV_LORA = 256
MLA_HEAD_PAD = 256
NA_HEADS = 4
NA_HEAD_DIM = 128
NA_WIDTH = NA_HEADS * NA_HEAD_DIM
NA_KR = 8
NA_KC = 16
CONV_CH = 512
Z_BLOCK = 512
Z_NAV_BLOCK = 3
Z_CONV_BLOCK = 4
Z_CKV_BLOCK = 7
D_IN_PAD = (Z_CKV_BLOCK + 1) * Z_BLOCK
Z_KPE_COL = Z_CKV_BLOCK * Z_BLOCK + MLA_KV_LORA
D_IN = MLA_Q_LORA + MLA_KV_LORA + MLA_ROPE + 3 * NA_WIDTH + 3 * CONV_CH
ATTN_Q_BLOCK = 256

PEER_HEADS = 8
PEER_N_KEYS = 128
PEER_N_EXPERTS = PEER_N_KEYS * PEER_N_KEYS
PEER_TOPK = 16

LANES = 128
SUBLANES = 8
VMEM_LIMIT = 60 * 1024 * 1024

ROW_TILE = 1024
SEQ_TILE = CTX_SEQ
PEER_TOK_TILE = 1024
PEER_EXP_TILE = 512


def _params(sem):
    return pltpu.CompilerParams(dimension_semantics=sem, vmem_limit_bytes=VMEM_LIMIT)


def _group_of_row(row):
    return jnp.where(row < T_CTX, 0, 1 + (row - T_CTX) // LAT_SEQ)


def _rms(x, g, n):
    ms = jnp.sum(x * x, axis=-1, keepdims=True) / n
    return x * lax.rsqrt(ms + NORM_EPS) * g


def _ada_kernel(c_ref, w_ref, b_ref, o_ref):
    c = c_ref[...]
    s = c * jax.nn.sigmoid(c)
    o_ref[...] = jnp.dot(s.astype(BF16), w_ref[...].astype(BF16), preferred_element_type=F32) + b_ref[...]


def ada_all(cpad, ada_w, ada_b):
    tn = 1536
    n = ADA_CHUNKS * D_MODEL
    return pl.pallas_call(
        _ada_kernel,
        grid=(DEPTH, n // tn),
        in_specs=[pl.BlockSpec((8, D_MODEL), lambda l, j: (0, 0)),
                  pl.BlockSpec((None, D_MODEL, tn), lambda l, j: (l, 0, j)),
                  pl.BlockSpec((None, 1, tn), lambda l, j: (l, 0, j))],
        out_specs=pl.BlockSpec((None, 8, tn), lambda l, j: (l, 0, j)),
        out_shape=jax.ShapeDtypeStruct((DEPTH, 8, n), F32),
        compiler_params=_params(("arbitrary", "arbitrary")),
        name="ada",
    )(cpad, ada_w, ada_b.reshape(DEPTH, 1, n))


def _mod_matmul_kernel(x_ref, g_ref, sh_ref, sc_ref, w_ref, z_ref, *rest, tm, emit_ht):
    if emit_ht:
        ht_ref, h_scr = rest
    else:
        (h_scr,) = rest
    i = pl.program_id(0)

    @pl.when(pl.program_id(1) == 0)
    def _():
        grp = _group_of_row(i * tm)
        y = _rms(x_ref[...], g_ref[...], D_MODEL)
        h = y * (1.0 + sc_ref[pl.ds(grp, 1), :]) + sh_ref[pl.ds(grp, 1), :]
        h_scr[...] = h.astype(BF16)
        if emit_ht:
            ht_ref[...] = h.T.astype(BF16)

    z_ref[...] = jnp.dot(h_scr[...], w_ref[...], preferred_element_type=F32).astype(z_ref.dtype)


def mod_matmul(x, gain, mods, w, layer, shift_chunk, scale_chunk, *, tn, emit_ht, tm=ROW_TILE, out_dtype=F32):
    t = x.shape[0]
    n = w.shape[-1]
    out_shape = [jax.ShapeDtypeStruct((t, n), out_dtype)]
    out_specs = [pl.BlockSpec((tm, tn), lambda i, j: (i, j))]
    if emit_ht:
        out_shape.append(jax.ShapeDtypeStruct((D_MODEL, t), BF16))
        out_specs.append(pl.BlockSpec((D_MODEL, tm), lambda i, j: (0, i)))
    res = pl.pallas_call(
        functools.partial(_mod_matmul_kernel, tm=tm, emit_ht=emit_ht),
        grid=(t // tm, n // tn),
        in_specs=[pl.BlockSpec((tm, D_MODEL), lambda i, j: (i, 0)),
                  pl.BlockSpec((None, 1, D_MODEL), lambda i, j: (layer, 0, 0)),
                  pl.BlockSpec((None, 8, D_MODEL), lambda i, j: (layer, 0, shift_chunk)),
                  pl.BlockSpec((None, 8, D_MODEL), lambda i, j: (layer, 0, scale_chunk)),
                  pl.BlockSpec((None, D_MODEL, tn), lambda i, j: (layer, 0, j))],
        out_specs=out_specs,
        out_shape=out_shape,
        scratch_shapes=[pltpu.VMEM((tm, D_MODEL), BF16)],
        compiler_params=_params(("arbitrary", "arbitrary")),
        name="mod_matmul_ht" if emit_ht else "mod_matmul",
    )(x, gain, mods, mods, w)
    return res


def _rope128(x, c, s1, s2):
    return x * c + pltpu.roll(x, 96, 1) * s1 + pltpu.roll(x, 32, 1) * s2


def _mla_keys_values(ckv_n, kpe, rc, rs1, rs2, wukv_ref, gn_ref, gp_ref, k_ref, v_ref):
    kv = jnp.dot(ckv_n.astype(BF16), wukv_ref[...], preferred_element_type=F32)
    pe_ss = jnp.sum(kpe * kpe, axis=-1, keepdims=True)
    for h in range(MLA_HEADS):
        kn = kv[:, h * MLA_NOPE:(h + 1) * MLA_NOPE]
        ms = (jnp.sum(kn * kn, axis=-1, keepdims=True) + pe_ss) / MLA_QK_DIM
        r = lax.rsqrt(ms + NORM_EPS)
        k_ref[:, h * MLA_HEAD_PAD:h * MLA_HEAD_PAD + LANES] = (kn * r * gn_ref[...]).astype(BF16)
        k_ref[:, h * MLA_HEAD_PAD + LANES:(h + 1) * MLA_HEAD_PAD] = (
            _rope128(kpe * r * gp_ref[...], rc, rs1, rs2).astype(BF16))
    v_ref[...] = kv[:, MLA_HEADS * MLA_NOPE:].astype(BF16)


def _proj_prep_kernel(cq_ref, naq_ref, nak_ref, nav_ref, ckv_ref, rc_ref, rs1_ref, rs2_ref,
                      qng_ref, wuq_ref, qhg_ref, kvng_ref, naqg_ref, nakg_ref, wukv_ref, gn_ref, gp_ref,
                      qmla_ref, ckvn_ref, qna_ref, knaf_ref, vnaf_ref, knab_ref, vnab_ref, k_ref, v_ref):
    cqn = _rms(cq_ref[...], qng_ref[...], MLA_Q_LORA)
    q = jnp.dot(cqn.astype(BF16), wuq_ref[...], preferred_element_type=F32)
    rc, rs1, rs2 = rc_ref[...], rs1_ref[...], rs2_ref[...]
    for h in range(MLA_HEADS):
        qh = _rms(q[:, h * MLA_HEAD_PAD:(h + 1) * MLA_HEAD_PAD], qhg_ref[...], MLA_QK_DIM)
        qmla_ref[:, h * MLA_HEAD_PAD:h * MLA_HEAD_PAD + LANES] = qh[:, :LANES].astype(BF16)
        qmla_ref[:, h * MLA_HEAD_PAD + LANES:(h + 1) * MLA_HEAD_PAD] = (
            _rope128(qh[:, LANES:], rc, rs1, rs2).astype(BF16))
    ckv_n = _rms(ckv_ref[:, :MLA_KV_LORA], kvng_ref[...], MLA_KV_LORA)
    ckvn_ref[...] = ckv_n
    _mla_keys_values(ckv_n, ckv_ref[:, MLA_KV_LORA:MLA_KV_LORA + LANES], rc, rs1, rs2,
                     wukv_ref, gn_ref, gp_ref, k_ref, v_ref)
    for h in range(NA_HEADS):
        sl = slice(h * NA_HEAD_DIM, (h + 1) * NA_HEAD_DIM)
        qna_ref[:, sl] = _rms(naq_ref[:, sl], naqg_ref[...], NA_HEAD_DIM).astype(BF16)
        kn = _rms(nak_ref[:, sl], nakg_ref[...], NA_HEAD_DIM)
        knaf_ref[h] = kn
        vnaf_ref[h] = nav_ref[:, sl]
        knab_ref[:, sl] = kn.astype(BF16)
    vnab_ref[...] = nav_ref[...].astype(BF16)


def proj_prep(z, rope_c, rope_s1, rope_s2, qng, wuq, qhg, kvng, naqg, nakg, wukv, gn, gp, layer):
    t = z.shape[0]
    tm = SEQ_TILE
    zb = lambda k: pl.BlockSpec((tm, Z_BLOCK), lambda i: (i, k))
    rb = pl.BlockSpec((tm, LANES), lambda i: (i, 0))
    wl = lambda *shape: pl.BlockSpec((None,) + shape, lambda i: (layer,) + (0,) * len(shape))
    ob = lambda w: pl.BlockSpec((tm, w), lambda i: (i, 0))
    cache_b = pl.BlockSpec((None, NA_HEADS, tm, NA_HEAD_DIM), lambda i: (i, 0, 0, 0))
    return pl.pallas_call(
        _proj_prep_kernel,
        grid=(t // tm,),
        in_specs=[zb(0), zb(1), zb(2), zb(Z_NAV_BLOCK), zb(Z_CKV_BLOCK), rb, rb, rb,
                  wl(1, MLA_Q_LORA), wl(MLA_Q_LORA, MLA_HEADS * MLA_HEAD_PAD), wl(1, MLA_HEAD_PAD),
                  wl(1, MLA_KV_LORA), wl(1, NA_HEAD_DIM), wl(1, NA_HEAD_DIM),
                  wl(MLA_KV_LORA, 2 * MLA_HEADS * MLA_NOPE), wl(1, LANES), wl(1, LANES)],
        out_specs=[ob(MLA_HEADS * MLA_HEAD_PAD), ob(MLA_KV_LORA), ob(NA_WIDTH), cache_b, cache_b, ob(NA_WIDTH),
                   ob(NA_WIDTH), ob(MLA_HEADS * MLA_HEAD_PAD), ob(MLA_HEADS * MLA_NOPE)],
        out_shape=[jax.ShapeDtypeStruct((t, MLA_HEADS * MLA_HEAD_PAD), BF16),
                   jax.ShapeDtypeStruct((t, MLA_KV_LORA), F32),
                   jax.ShapeDtypeStruct((t, NA_WIDTH), BF16),
                   jax.ShapeDtypeStruct((t // tm, NA_HEADS, tm, NA_HEAD_DIM), F32),
                   jax.ShapeDtypeStruct((t // tm, NA_HEADS, tm, NA_HEAD_DIM), F32),
                   jax.ShapeDtypeStruct((t, NA_WIDTH), BF16),
                   jax.ShapeDtypeStruct((t, NA_WIDTH), BF16),
                   jax.ShapeDtypeStruct((t, MLA_HEADS * MLA_HEAD_PAD), BF16),
                   jax.ShapeDtypeStruct((t, MLA_HEADS * MLA_NOPE), BF16)],
        compiler_params=_params(("arbitrary",)),
        name="proj_prep",
    )(z, z, z, z, z, rope_c, rope_s1, rope_s2, qng, wuq, qhg, kvng, naqg, nakg, wukv, gn, gp)


def _kv_prep_kernel(ckv_ref, kpe_ref, rc_ref, rs1_ref, rs2_ref, wukv_ref, gn_ref, gp_ref, k_ref, v_ref):
    _mla_keys_values(ckv_ref[...], kpe_ref[...], rc_ref[...], rs1_ref[...], rs2_ref[...],
                     wukv_ref, gn_ref, gp_ref, k_ref, v_ref)


def kv_prep_cache(ckv, kpe, ident_c, ident_s, wukv, gn, gp):
    tm = SEQ_TILE
    rb = lambda w: pl.BlockSpec((None, tm, w), lambda l, i: (l, i, 0))
    tb = pl.BlockSpec((tm, LANES), lambda l, i: (0, 0))
    wl = lambda *shape: pl.BlockSpec((None,) + shape, lambda l, i: (l,) + (0,) * len(shape))
    return pl.pallas_call(
        _kv_prep_kernel,
        grid=(DEPTH, T_CACHE // tm),
        in_specs=[rb(MLA_KV_LORA), rb(LANES), tb, tb, tb,
                  wl(MLA_KV_LORA, 2 * MLA_HEADS * MLA_NOPE), wl(1, LANES), wl(1, LANES)],
        out_specs=[rb(MLA_HEADS * MLA_HEAD_PAD), rb(MLA_HEADS * MLA_NOPE)],
        out_shape=[jax.ShapeDtypeStruct((DEPTH, T_CACHE, MLA_HEADS * MLA_HEAD_PAD), BF16),
                   jax.ShapeDtypeStruct((DEPTH, T_CACHE, MLA_HEADS * MLA_NOPE), BF16)],
        compiler_params=_params(("arbitrary", "arbitrary")),
        name="kv_prep_cache",
    )(ckv, kpe, ident_c, ident_s, ident_s, wukv, gn, gp)


_NT = (((1,), (1,)), ((), ()))


def _attn_kernel(q_ref, k_ref, v_ref, o_ref, *, scale, n_heads, dq):
    dv = NA_HEAD_DIM
    for h in range(n_heads):
        q = q_ref[:, h * dq:(h + 1) * dq]
        k = k_ref[:, h * dq:(h + 1) * dq]
        s = lax.dot_general(q, k, _NT, preferred_element_type=F32) * scale
        m = jnp.max(s, axis=-1, keepdims=True)
        p = jnp.exp(s - m)
        p = p / jnp.sum(p, axis=-1, keepdims=True)
        o = jnp.dot(p.astype(BF16), v_ref[:, h * dv:(h + 1) * dv], preferred_element_type=F32)
        o_ref[:, h * dv:(h + 1) * dv] = o.astype(BF16)


def attention(q, k, v, *, n_seq, n_heads, dq, sq_total, sk, q_row_off, k_row_off, scale):
    sq = ATTN_Q_BLOCK
    nq = sq_total // sq
    qoff = q_row_off // sq
    koff = k_row_off // sk
    dv = NA_HEAD_DIM
    return pl.pallas_call(
        functools.partial(_attn_kernel, scale=scale, n_heads=n_heads, dq=dq),
        grid=(n_seq, nq),
        in_specs=[pl.BlockSpec((sq, n_heads * dq), lambda b, qi: (qoff + b * nq + qi, 0)),
                  pl.BlockSpec((sk, n_heads * dq), lambda b, qi: (koff + b, 0)),
                  pl.BlockSpec((sk, n_heads * dv), lambda b, qi: (koff + b, 0))],
        out_specs=pl.BlockSpec((sq, n_heads * dv), lambda b, qi: (b * nq + qi, 0)),
        out_shape=jax.ShapeDtypeStruct((n_seq * sq_total, n_heads * dv), BF16),
        compiler_params=_params(("arbitrary", "arbitrary")),
        name="attention",
    )(q, k, v)


def _attn_cached_kernel(q_ref, k_ref, v_ref, kc_ref, vc_ref, o_ref, *, scale, n_heads, dq):
    dv = NA_HEAD_DIM
    for h in range(n_heads):
        qk = slice(h * dq, (h + 1) * dq)
        vv = slice(h * dv, (h + 1) * dv)
        q = q_ref[:, qk]
        s_own = lax.dot_general(q, k_ref[:, qk], _NT, preferred_element_type=F32) * scale
        s_ctx = lax.dot_general(q, kc_ref[:, qk], _NT, preferred_element_type=F32) * scale
        m = jnp.maximum(jnp.max(s_own, axis=-1, keepdims=True), jnp.max(s_ctx, axis=-1, keepdims=True))
        p_own = jnp.exp(s_own - m)
        p_ctx = jnp.exp(s_ctx - m)
        denom = jnp.sum(p_own, axis=-1, keepdims=True) + jnp.sum(p_ctx, axis=-1, keepdims=True)
        o = jnp.dot((p_ctx / denom).astype(BF16), vc_ref[:, vv], preferred_element_type=F32)
        o = o + jnp.dot((p_own / denom).astype(BF16), v_ref[:, vv], preferred_element_type=F32)
        o_ref[:, vv] = o.astype(BF16)


def latent_mla_attention(q, k, v, k_cache, v_cache, layer, *, scale):
    sq = ATTN_Q_BLOCK
    nq = LAT_SEQ // sq
    qoff = T_CTX // sq
    koff = T_CTX // LAT_SEQ
    wq = MLA_HEADS * MLA_HEAD_PAD
    wv = MLA_HEADS * NA_HEAD_DIM
    return pl.pallas_call(
        functools.partial(_attn_cached_kernel, scale=scale, n_heads=MLA_HEADS, dq=MLA_HEAD_PAD),
        grid=(N_LAT_SEQ, nq),
        in_specs=[pl.BlockSpec((sq, wq), lambda b, qi: (qoff + b * nq + qi, 0)),
                  pl.BlockSpec((LAT_SEQ, wq), lambda b, qi: (koff + b, 0)),
                  pl.BlockSpec((LAT_SEQ, wv), lambda b, qi: (koff + b, 0)),
                  pl.BlockSpec((None, PAST_LEN, wq), lambda b, qi: (layer, b, 0)),
                  pl.BlockSpec((None, PAST_LEN, wv), lambda b, qi: (layer, b, 0))],
        out_specs=pl.BlockSpec((sq, wv), lambda b, qi: (b * nq + qi, 0)),
        out_shape=jax.ShapeDtypeStruct((T_LAT, wv), BF16),
        compiler_params=_params(("arbitrary", "arbitrary")),
        name="latent_mla",
    )(q, k, v, k_cache, v_cache)


NA_ROWS = LAT_SEQ // GRID_W
NA_LOCAL = NA_KR * GRID_W


def _na_lat_kernel(q_ref, k_ref, v_ref, kc_ref, vc_ref, b_ref, o_ref, *, scale):
    rq = pl.program_id(1)
    row_start = jnp.clip(rq - NA_KR // 2, 0, NA_ROWS - NA_KR)
    start = pl.multiple_of(row_start * GRID_W, GRID_W)
    for h in range(NA_HEADS):
        cols = slice(h * NA_HEAD_DIM, (h + 1) * NA_HEAD_DIM)
        q = q_ref[:, cols]
        k_loc = k_ref[pl.ds(start, NA_LOCAL), cols]
        v_loc = v_ref[pl.ds(start, NA_LOCAL), cols]
        s_loc = lax.dot_general(q, k_loc, _NT, preferred_element_type=F32) * scale + b_ref[h, rq - row_start]
        s_ctx = lax.dot_general(q, kc_ref[:, cols], _NT, preferred_element_type=F32) * scale
        m = jnp.maximum(jnp.max(s_loc, axis=-1, keepdims=True), jnp.max(s_ctx, axis=-1, keepdims=True))
        p_loc = jnp.exp(s_loc - m)
        p_ctx = jnp.exp(s_ctx - m)
        denom = jnp.sum(p_loc, axis=-1, keepdims=True) + jnp.sum(p_ctx, axis=-1, keepdims=True)
        o = jnp.dot((p_loc / denom).astype(BF16), v_loc, preferred_element_type=F32)
        o = o + jnp.dot((p_ctx / denom).astype(BF16), vc_ref[:, cols], preferred_element_type=F32)
        o_ref[:, cols] = o.astype(BF16)


def na_latent_attention(q, k, v, k_cache, v_cache, bias_tab, layer, *, scale):
    qoff = T_CTX // GRID_W
    koff = T_CTX // LAT_SEQ
    w = NA_WIDTH
    return pl.pallas_call(
        functools.partial(_na_lat_kernel, scale=scale),
        grid=(N_LAT_SEQ, NA_ROWS),
        in_specs=[pl.BlockSpec((GRID_W, w), lambda b, r: (qoff + b * NA_ROWS + r, 0)),
                  pl.BlockSpec((LAT_SEQ, w), lambda b, r: (koff + b, 0)),
                  pl.BlockSpec((LAT_SEQ, w), lambda b, r: (koff + b, 0)),
                  pl.BlockSpec((None, None, PAST_LEN, w), lambda b, r: (layer, b, 0, 0)),
                  pl.BlockSpec((None, None, PAST_LEN, w), lambda b, r: (layer, b, 0, 0)),
                  pl.BlockSpec((None, NA_HEADS, NA_KR, GRID_W, NA_LOCAL), lambda b, r: (layer, 0, 0, 0, 0))],
        out_specs=pl.BlockSpec((GRID_W, w), lambda b, r: (b * NA_ROWS + r, 0)),
        out_shape=jax.ShapeDtypeStruct((T_LAT, NA_WIDTH), BF16),
        compiler_params=_params(("arbitrary", "arbitrary")),
        name="na_latent",
    )(q, k, v, k_cache, v_cache, bias_tab)


def _conv_kernel(gb_ref, gc_ref, u_ref, w_ref, o_ref):
    gu = gc_ref[...] * u_ref[...]
    s = gu.shape[0]
    row = lax.broadcasted_iota(jnp.int32, gu.shape, 0)
    prev = jnp.where(row == 0, 0.0, pltpu.roll(gu, 1, 0))
    nxt = jnp.where(row == s - 1, 0.0, pltpu.roll(gu, s - 1, 0))
    y = prev * w_ref[0:1, :] + gu * w_ref[1:2, :] + nxt * w_ref[2:3, :]
    o_ref[...] = (gb_ref[...] * y).astype(BF16)


def short_conv(z, conv_w8, layer, *, n_seq, seq, row_off):
    off = row_off // seq
    zb = lambda k: pl.BlockSpec((seq, CONV_CH), lambda i: (off + i, k))
    return pl.pallas_call(
        _conv_kernel,
        grid=(n_seq,),
        in_specs=[zb(Z_CONV_BLOCK), zb(Z_CONV_BLOCK + 1), zb(Z_CONV_BLOCK + 2),
                  pl.BlockSpec((None, 8, CONV_CH), lambda i: (layer, 0, 0))],
        out_specs=pl.BlockSpec((seq, CONV_CH), lambda i: (i, 0)),
        out_shape=jax.ShapeDtypeStruct((n_seq * seq, CONV_CH), BF16),
        compiler_params=_params(("arbitrary",)),
        name="short_conv",
    )(z, z, z, conv_w8)


def _out_matmul_kernel(x_ref, g_ref, a0c_ref, a0l_ref, a1c_ref, a1l_ref, a2c_ref, a2l_ref,
                       w0_ref, w1_ref, w2_ref, o_ref, *, tm):
    row = pl.program_id(0) * tm
    grp = _group_of_row(row)
    is_ctx = row < T_CTX
    acc = jnp.dot(jnp.where(is_ctx, a0c_ref[...], a0l_ref[...]), w0_ref[...], preferred_element_type=F32)
    acc = acc + jnp.dot(jnp.where(is_ctx, a1c_ref[...], a1l_ref[...]), w1_ref[...], preferred_element_type=F32)
    acc = acc + jnp.dot(jnp.where(is_ctx, a2c_ref[...], a2l_ref[...]), w2_ref[...], preferred_element_type=F32)
    o_ref[...] = x_ref[...] + g_ref[pl.ds(grp, 1), :] * acc


def out_matmul(x, mods, o_mla, o_na, conv, w_out, layer, gate_chunk):
    t = x.shape[0]
    tm, tn = ROW_TILE, 1024
    nj = D_MODEL // tn
    w_mla = MLA_HEADS * MLA_NOPE
    n_ctx = T_CTX // tm
    ctx = lambda w: pl.BlockSpec((tm, w), lambda i, j: (jnp.minimum(i, n_ctx - 1), 0))
    lat = lambda w: pl.BlockSpec((tm, w), lambda i, j: (jnp.maximum(i - n_ctx, 0), 0))
    return pl.pallas_call(
        functools.partial(_out_matmul_kernel, tm=tm),
        grid=(t // tm, nj),
        in_specs=[pl.BlockSpec((tm, tn), lambda i, j: (i, j)),
                  pl.BlockSpec((None, 8, tn), lambda i, j: (layer, 0, gate_chunk * nj + j)),
                  ctx(w_mla), lat(w_mla), ctx(NA_WIDTH), lat(NA_WIDTH), ctx(CONV_CH), lat(CONV_CH),
                  pl.BlockSpec((None, w_mla, tn), lambda i, j: (layer, 0, j)),
                  pl.BlockSpec((None, NA_WIDTH, tn), lambda i, j: (layer, w_mla // NA_WIDTH, j)),
                  pl.BlockSpec((None, CONV_CH, tn), lambda i, j: (layer, (w_mla + NA_WIDTH) // CONV_CH, j))],
        out_specs=pl.BlockSpec((tm, tn), lambda i, j: (i, j)),
        out_shape=jax.ShapeDtypeStruct((t, D_MODEL), F32),
        compiler_params=_params(("arbitrary", "arbitrary")),
        name="out_matmul",
    )(x, mods, *o_mla, *o_na, *conv, w_out, w_out, w_out)


def _argmax_step(s, pos, big):
    vals = [s[k:k + SUBLANES] for k in range(0, s.shape[0], SUBLANES)]
    idxs = [pos[k:k + SUBLANES] for k in range(0, s.shape[0], SUBLANES)]
    while len(vals) > 1:
        nxt_v, nxt_i = [], []
        for a in range(0, len(vals) - 1, 2):
            take_hi = vals[a + 1] > vals[a]
            nxt_v.append(jnp.maximum(vals[a], vals[a + 1]))
            nxt_i.append(jnp.where(take_hi, idxs[a + 1], idxs[a]))
        if len(vals) % 2:
            nxt_v.append(vals[-1])
            nxt_i.append(idxs[-1])
        vals, idxs = nxt_v, nxt_i
    m = jnp.max(vals[0], axis=0, keepdims=True)
    first = jnp.min(jnp.where(vals[0] == m, idxs[0], big), axis=0, keepdims=True)
    return m, first, pos == first


def _half_ranks(s1, s2):
    n, width = s1.shape
    pos = lax.broadcasted_iota(jnp.int32, (n, width), 0).astype(F32)
    iota16 = lax.broadcasted_iota(jnp.int32, (PEER_TOPK, width), 0)

    def body(j, carry):
        s1c, v1c, i1c, s2c, v2c, r2c = carry
        m1, first1, hit1 = _argmax_step(s1c, pos, float(n))
        m2, _, hit2 = _argmax_step(s2c, pos, float(n))
        return (jnp.where(hit1, -jnp.inf, s1c), jnp.where(iota16 == j, m1, v1c), jnp.where(iota16 == j, first1, i1c),
                jnp.where(hit2, -jnp.inf, s2c), jnp.where(iota16 == j, m2, v2c), jnp.where(hit2, j, r2c))

    zero16 = jnp.zeros((PEER_TOPK, width), F32)
    rank0 = jnp.full((n, width), float(PEER_TOPK), F32)
    _, v1, i1, _, v2, r2 = lax.fori_loop(0, PEER_TOPK, body, (s1, zero16, zero16, s2, zero16, rank0), unroll=True)
    return pos, v1, i1, v2, r2


_CAND_ROWS = 16 + 7 * 8 + 8


def _cand_positions():
    p = [j2 for j2 in range(16)]
    p += [j1 * 16 + j2 for j1 in range(1, 8) for j2 in range(8)]
    p += [j1 * 16 for j1 in range(8, 16)]
    return np.tile(np.asarray(p, np.float32)[:, None], (1, LANES))


def _staircase(v1, v2, cpos):
    cand = jnp.concatenate([v1[0:1] + v2] + [v1[j:j + 1] + v2[0:8] for j in range(1, 8)] + [v1[8:16] + v2[0:1]],
                           axis=0)
    cmax = cand[0:1]
    row16 = lax.broadcasted_iota(jnp.int32, (PEER_TOPK, cand.shape[1]), 0).astype(F32)

    def body(j, carry):
        s, counts, z = carry
        m, first, hit = _argmax_step(s, cpos, float(PEER_TOPK * PEER_TOPK))
        j1 = jnp.floor(first * (1.0 / PEER_TOPK))
        counts = jnp.where(row16 == j1, counts + 1.0, counts)
        return jnp.where(hit, -jnp.inf, s), counts, z + jnp.exp(m - cmax)

    _, counts, z = lax.fori_loop(0, PEER_TOPK, body, (cand, jnp.zeros_like(row16), jnp.zeros_like(cmax)),
                                 unroll=True)
    return counts, z


def _peer_topk_kernel(q_ref, keys_ref, cpos_ref, e1_ref, lb1_ref, r2_ref, e2_ref, *, heads):
    nt = (((1,), (1,)), ((), ()))
    for h in range(heads):
        q = q_ref[:, h * 2 * LANES:(h + 1) * 2 * LANES]
        s1 = lax.dot_general(keys_ref[h, 0], q[:, :LANES], nt, preferred_element_type=F32)
        s2 = lax.dot_general(keys_ref[h, 1], q[:, LANES:], nt, preferred_element_type=F32)
        pos, v1, i1, v2, r2 = _half_ranks(s1, s2)
        counts, z = _staircase(v1, v2, cpos_ref[...])
        lb1 = jnp.zeros_like(s1)
        for j in range(PEER_TOPK):
            lb1 = jnp.where(pos == i1[j:j + 1], counts[j:j + 1], lb1)
        e1_ref[h] = jnp.exp(s1 - v1[0:1])
        lb1_ref[h] = lb1
        r2_ref[h] = r2
        e2_ref[h] = jnp.exp(s2 - v2[0:1]) / z


def peer_topk(q, sub_keys, layer):
    t = q.shape[0]
    heads = PEER_HEADS
    ob = pl.BlockSpec((heads, PEER_N_KEYS, LANES), lambda i, h: (h, 0, i))
    shp = jax.ShapeDtypeStruct((PEER_HEADS, PEER_N_KEYS, t), F32)
    return pl.pallas_call(
        functools.partial(_peer_topk_kernel, heads=heads),
        grid=(t // LANES, PEER_HEADS // heads),
        in_specs=[pl.BlockSpec((LANES, heads * 2 * LANES), lambda i, h: (i, h)),
                  pl.BlockSpec((None, heads, 2, PEER_N_KEYS, LANES), lambda i, h: (layer, h, 0, 0, 0)),
                  pl.BlockSpec((_CAND_ROWS, LANES), lambda i, h: (0, 0))],
        out_specs=[ob, ob, ob, ob],
        out_shape=[shp, shp, shp, shp],
        compiler_params=_params(("arbitrary", "arbitrary")),
        name="peer_topk",
    )(q, sub_keys, jnp.asarray(_cand_positions()))


_SQRT_HALF = float(np.sqrt(0.5))


def _peer_dense_kernel(ht_ref, u_ref, v_ref, e1_ref, lb1_ref, r2_ref, e2_ref, x_ref, g_ref, o_ref,
                       acc_ref, at_ref, wg_ref, *, tt, ec):
    i = pl.program_id(0)
    c = pl.program_id(1)

    @pl.when(c == 0)
    def _():
        acc_ref[...] = jnp.zeros_like(acc_ref)

    at_ref[...] = jnp.dot(u_ref[...], ht_ref[...], preferred_element_type=F32)
    n_a = ec // PEER_N_KEYS
    grp0 = pl.multiple_of((c * n_a) // 8 * 8, 8)
    off = (c * n_a) % 8

    def key_row(ref, h, al, lanes):
        blk = ref[h, pl.ds(grp0, 8), lanes]
        row = blk[al:al + 1]
        for o in range(n_a, 8, n_a):
            row = jnp.where(off == o, blk[o + al:o + al + 1], row)
        return row

    pair = 2
    for lt in range(tt // LANES):
        lanes = slice(lt * LANES, (lt + 1) * LANES)
        for al0 in range(0, n_a, pair):
            gates = [None] * pair
            for h in range(PEER_HEADS):
                r2 = r2_ref[h, :, lanes]
                e2 = e2_ref[h, :, lanes]
                for k in range(pair):
                    lb = key_row(lb1_ref, h, al0 + k, lanes)
                    e1 = key_row(e1_ref, h, al0 + k, lanes)
                    term = jnp.where(r2 < lb, e2, 0.0) * e1
                    gates[k] = term if gates[k] is None else gates[k] + term
            for k in range(pair):
                rows = slice((al0 + k) * PEER_N_KEYS, (al0 + k + 1) * PEER_N_KEYS)
                act = at_ref[rows, lanes]
                wg_ref[rows, lanes] = 0.5 * act * (1.0 + lax.erf(act * _SQRT_HALF)) * gates[k]
    acc_ref[...] += jnp.dot(wg_ref[...].T.astype(BF16), v_ref[...], preferred_element_type=F32)

    @pl.when(c == pl.num_programs(1) - 1)
    def _():
        grp = _group_of_row(i * tt)
        o_ref[...] = x_ref[...] + g_ref[pl.ds(grp, 1), :] * acc_ref[...]


def peer_dense(ht, peer_u, peer_v, e1, lb1, r2, e2, x, mods, layer, gate_chunk):
    t = x.shape[0]
    tt, ec = PEER_TOK_TILE, PEER_EXP_TILE
    once = pl.Buffered(1)
    kb = pl.BlockSpec((PEER_HEADS, PEER_N_KEYS, tt), lambda i, c: (0, 0, i), pipeline_mode=once)
    return pl.pallas_call(
        functools.partial(_peer_dense_kernel, tt=tt, ec=ec),
        grid=(t // tt, PEER_N_EXPERTS // ec),
        in_specs=[pl.BlockSpec((D_MODEL, tt), lambda i, c: (0, i), pipeline_mode=once),
                  pl.BlockSpec((None, ec, D_MODEL), lambda i, c: (layer, c, 0)),
                  pl.BlockSpec((None, ec, D_MODEL), lambda i, c: (layer, c, 0)),
                  kb, kb, kb, kb,
                  pl.BlockSpec((tt, D_MODEL), lambda i, c: (i, 0), pipeline_mode=once),
                  pl.BlockSpec((None, 8, D_MODEL), lambda i, c: (layer, 0, gate_chunk))],
        out_specs=pl.BlockSpec((tt, D_MODEL), lambda i, c: (i, 0), pipeline_mode=once),
        out_shape=jax.ShapeDtypeStruct((t, D_MODEL), F32),
        scratch_shapes=[pltpu.VMEM((tt, D_MODEL), F32),
                        pltpu.VMEM((ec, tt), F32),
                        pltpu.VMEM((ec, tt), F32)],
        compiler_params=_params(("arbitrary", "arbitrary")),
        name="peer_dense",
    )(ht, peer_u, peer_v, e1, lb1, r2, e2, x, mods)


def _rope_tables():
    t = jnp.arange(LAT_SEQ)
    row = (t // GRID_W).astype(F32)
    col = (t % GRID_W).astype(F32)
    n_freq = MLA_ROPE // 4
    inv = ROPE_THETA ** (-jnp.arange(n_freq, dtype=F32) / n_freq)
    ang = jnp.concatenate([row[:, None] * inv, col[:, None] * inv], axis=-1)
    cos, sin = jnp.cos(ang), jnp.sin(ang)
    zero = jnp.zeros_like(cos)
    pad = jnp.zeros((LAT_SEQ, LANES - MLA_ROPE), F32)
    c_lat = jnp.concatenate([cos, cos, pad], axis=-1)
    s1_lat = jnp.concatenate([-sin, zero, pad], axis=-1)
    s2_lat = jnp.concatenate([zero, sin, pad], axis=-1)
    ones = jnp.concatenate([jnp.ones((1, MLA_ROPE), F32), jnp.zeros((1, LANES - MLA_ROPE), F32)], axis=-1)

    def full(lat, ident):
        return jnp.concatenate([jnp.broadcast_to(ident, (T_CTX, LANES)), jnp.tile(lat, (N_LAT_SEQ, 1)),
                                jnp.broadcast_to(ident, (T_CACHE, LANES))], axis=0)

    zeros = jnp.zeros((1, LANES), F32)
    return full(c_lat, ones), full(s1_lat, zeros), full(s2_lat, zeros)


def _na_bias_table(rel_bias):
    cq = np.arange(GRID_W)
    col_start = np.clip(cq - NA_KC // 2, 0, GRID_W - NA_KC)
    valid = (cq[None, :] >= col_start[:, None]) & (cq[None, :] < col_start[:, None] + NA_KC)
    coff = np.clip(cq[None, :] - cq[:, None], -(NA_KC - 1), NA_KC - 1) + (NA_KC - 1)
    onehot = (coff[:, :, None] == np.arange(2 * NA_KC - 1)[None, None, :]).astype(np.float32)
    toep = jnp.einsum('lhrc,qkc->lhrqk', rel_bias, jnp.asarray(onehot), precision=lax.Precision.HIGHEST)
    toep = jnp.where(jnp.asarray(valid)[None, None, None], toep, -jnp.inf)
    tabs = []
    for d in range(NA_KR):
        rows = toep[:, :, NA_KR - 1 - d:2 * NA_KR - 1 - d]
        tabs.append(rows.transpose(0, 1, 3, 2, 4).reshape(DEPTH, NA_HEADS, GRID_W, NA_LOCAL))
    return jnp.stack(tabs, axis=2)


def kernel(x_prompt, x_sample, cache_mla_ckv, cache_mla_kpe, cache_na_k, cache_na_v, c, c_ctx, ada_w, ada_b, norm_mix_g, norm_ffn_g, w_in, mla_q_norm_g, mla_w_uq, mla_kv_norm_g, mla_w_ukv, mla_q_head_g, mla_k_head_g, na_q_head_g, na_k_head_g, na_rel_bias, conv_w, w_out, peer_w_q, peer_sub_keys, peer_u, peer_v):
    lat_end = MLA_Q_LORA + MLA_KV_LORA + MLA_ROPE
    w_in_p = jnp.concatenate([w_in[..., :MLA_Q_LORA], w_in[..., lat_end:], w_in[..., MLA_Q_LORA:lat_end],
                              jnp.zeros((DEPTH, D_MODEL, D_IN_PAD - D_IN), F32)], axis=-1).astype(BF16)
    wuq_p = jnp.pad(mla_w_uq.reshape(DEPTH, MLA_Q_LORA, MLA_HEADS, MLA_QK_DIM),
                    ((0, 0), (0, 0), (0, 0), (0, MLA_HEAD_PAD - MLA_QK_DIM))
                    ).reshape(DEPTH, MLA_Q_LORA, MLA_HEADS * MLA_HEAD_PAD).astype(BF16)
    qhg_p = jnp.pad(mla_q_head_g, ((0, 0), (0, MLA_HEAD_PAD - MLA_QK_DIM))).reshape(DEPTH, 1, MLA_HEAD_PAD)
    wukv4 = mla_w_ukv.reshape(DEPTH, MLA_KV_LORA, MLA_HEADS, 2 * MLA_NOPE)
    wukv_p = jnp.concatenate([wukv4[..., :MLA_NOPE].reshape(DEPTH, MLA_KV_LORA, -1),
                              wukv4[..., MLA_NOPE:].reshape(DEPTH, MLA_KV_LORA, -1)], axis=-1).astype(BF16)
    khg_n = mla_k_head_g[:, :MLA_NOPE].reshape(DEPTH, 1, LANES)
    khg_p = jnp.pad(mla_k_head_g[:, MLA_NOPE:], ((0, 0), (0, LANES - MLA_ROPE))).reshape(DEPTH, 1, LANES)
    conv_w8 = jnp.pad(conv_w.transpose(0, 2, 1), ((0, 0), (0, 5), (0, 0)))
    w_out_b = w_out.astype(BF16)
    peer_wq_b = peer_w_q.astype(BF16)
    sub_keys_b = peer_sub_keys.astype(BF16)
    peer_u_b = peer_u.astype(BF16)
    peer_v_b = peer_v.astype(BF16)
    g_mix = norm_mix_g.reshape(DEPTH, 1, D_MODEL)
    g_ffn = norm_ffn_g.reshape(DEPTH, 1, D_MODEL)
    qng = mla_q_norm_g.reshape(DEPTH, 1, MLA_Q_LORA)
    kvng = mla_kv_norm_g.reshape(DEPTH, 1, MLA_KV_LORA)
    naqg = na_q_head_g.reshape(DEPTH, 1, NA_HEAD_DIM)
    nakg = na_k_head_g.reshape(DEPTH, 1, NA_HEAD_DIM)
    rope_c, rope_s1, rope_s2 = _rope_tables()
    na_bias_tab = _na_bias_table(na_rel_bias)
    cache_kpe_p = jnp.pad(cache_mla_kpe, ((0, 0), (0, 0), (0, 0), (0, LANES - MLA_ROPE)))
    cache_nak = cache_na_k.transpose(1, 0, 3, 2, 4).reshape(DEPTH, N_LAT_SEQ, PAST_LEN, NA_WIDTH).astype(BF16)
    cache_nav = cache_na_v.transpose(1, 0, 3, 2, 4).reshape(DEPTH, N_LAT_SEQ, PAST_LEN, NA_WIDTH).astype(BF16)

    cpad = jnp.concatenate([c_ctx[None, :], c, jnp.zeros((8 - 1 - N_LAT_SEQ, D_MODEL), F32)], axis=0)
    mods = ada_all(cpad, ada_w, ada_b)

    x = jnp.concatenate([x_prompt.reshape(T_CTX, D_MODEL), x_sample.reshape(T_LAT, D_MODEL)], axis=0)
    ckv_out, kpe_out, nak_out, nav_out = [], [], [], []
    mla_scale = MLA_QK_DIM ** -0.5
    na_scale = NA_HEAD_DIM ** -0.5

    ident_c = jnp.broadcast_to(rope_c[:1], (SEQ_TILE, LANES))
    ident_s = jnp.zeros((SEQ_TILE, LANES), F32)
    k_cache, v_cache = kv_prep_cache(
        cache_mla_ckv.transpose(1, 0, 2, 3).reshape(DEPTH, T_CACHE, MLA_KV_LORA),
        cache_kpe_p.transpose(1, 0, 2, 3).reshape(DEPTH, T_CACHE, LANES), ident_c, ident_s, wukv_p, khg_n, khg_p)

    for l in range(DEPTH):
        (z,) = mod_matmul(x, g_mix, mods, w_in_p, l, 0, 1, tn=1024, emit_ht=False)
        q_mla, ckv_n, q_na, k_na_f, v_na_f, k_na_b, v_na_b, k_mla, v_mla = proj_prep(
            z, rope_c, rope_s1, rope_s2, qng, wuq_p, qhg_p, kvng, naqg, nakg, wukv_p, khg_n, khg_p, l)

        o_mla_ctx = attention(q_mla, k_mla, v_mla, n_seq=N_CTX_SEQ, n_heads=MLA_HEADS, dq=MLA_HEAD_PAD,
                              sq_total=CTX_SEQ, sk=CTX_SEQ, q_row_off=0, k_row_off=0, scale=mla_scale)
        o_mla_lat = latent_mla_attention(q_mla, k_mla, v_mla, k_cache, v_cache, l, scale=mla_scale)
        o_na_ctx = attention(q_na, k_na_b, v_na_b, n_seq=N_CTX_SEQ, n_heads=NA_HEADS, dq=NA_HEAD_DIM,
                             sq_total=CTX_SEQ, sk=CTX_SEQ, q_row_off=0, k_row_off=0, scale=na_scale)
        o_na_lat = na_latent_attention(q_na, k_na_b, v_na_b, cache_nak, cache_nav, na_bias_tab, l, scale=na_scale)
        conv_ctx = short_conv(z, conv_w8, l, n_seq=N_CTX_SEQ, seq=CTX_SEQ, row_off=0)
        conv_lat = short_conv(z, conv_w8, l, n_seq=N_LAT_SEQ, seq=LAT_SEQ, row_off=T_CTX)

        x = out_matmul(x, mods, (o_mla_ctx, o_mla_lat), (o_na_ctx, o_na_lat), (conv_ctx, conv_lat), w_out_b, l, 2)

        q_peer, ht = mod_matmul(x, g_ffn, mods, peer_wq_b, l, 3, 4, tn=D_MODEL, emit_ht=True, out_dtype=BF16)
        e1, lb1, r2, e2 = peer_topk(q_peer, sub_keys_b, l)
        x = peer_dense(ht, peer_u_b, peer_v_b, e1, lb1, r2, e2, x, mods, l, 5)

        ckv_out.append(ckv_n[:T_CTX].reshape(N_CTX_SEQ, CTX_SEQ, MLA_KV_LORA))
        kpe_out.append(z[:T_CTX, Z_KPE_COL:Z_KPE_COL + MLA_ROPE].reshape(N_CTX_SEQ, CTX_SEQ, MLA_ROPE))
        nak_out.append(k_na_f[:N_CTX_SEQ])
        nav_out.append(v_na_f[:N_CTX_SEQ])

    y_prompt = x[:T_CTX].reshape(N_CTX_SEQ, CTX_SEQ, D_MODEL)
    y_sample = x[T_CTX:].reshape(N_LAT_SEQ, LAT_SEQ, D_MODEL)
    return (y_prompt, y_sample, jnp.stack(ckv_out, axis=1), jnp.stack(kpe_out, axis=1),
            jnp.stack(nak_out, axis=1), jnp.stack(nav_out, axis=1))
```

```python
import functools

import numpy as np
import jax
import jax.numpy as jnp
from jax import lax
from jax.experimental import pallas as pl
from jax.experimental.pallas import tpu as pltpu

F32 = jnp.float32
BF16 = jnp.bfloat16

D_MODEL = 2048
DEPTH = 4
N_CTX_SEQ = 32
CTX_SEQ = 256
N_LAT_SEQ = 2
LAT_SEQ = 1024
PAST_LEN = 512
T_CTX = N_CTX_SEQ * CTX_SEQ
T_LAT = N_LAT_SEQ * LAT_SEQ
T_ALL = T_CTX + T_LAT
T_CACHE = N_LAT_SEQ * PAST_LEN
GRID_W = 64
NORM_EPS = 1e-6
ROPE_THETA = 10000.0
ADA_CHUNKS = 6

MLA_HEADS = 8
MLA_NOPE = 128
MLA_ROPE = 64
MLA_QK_DIM = MLA_NOPE + MLA_ROPE
MLA_Q_LORA = 512
MLA_KV_LORA = 256
MLA_HEAD_PAD = 256
NA_HEADS = 4
NA_HEAD_DIM = 128
NA_WIDTH = NA_HEADS * NA_HEAD_DIM
NA_KR = 8
NA_KC = 16
CONV_CH = 512
Z_BLOCK = 512
Z_NAV_BLOCK = 3
Z_CONV_BLOCK = 4
Z_CKV_BLOCK = 7
D_IN_PAD = (Z_CKV_BLOCK + 1) * Z_BLOCK
Z_KPE_COL = Z_CKV_BLOCK * Z_BLOCK + MLA_KV_LORA
D_IN = MLA_Q_LORA + MLA_KV_LORA + MLA_ROPE + 3 * NA_WIDTH + 3 * CONV_CH
ATTN_Q_BLOCK = 256

PEER_HEADS = 8
PEER_N_KEYS = 128
PEER_N_EXPERTS = PEER_N_KEYS * PEER_N_KEYS
PEER_TOPK = 16

LANES = 128
SUBLANES = 8
VMEM_LIMIT = 60 * 1024 * 1024

ROW_TILE = 1024
SEQ_TILE = CTX_SEQ
PEER_TOK_TILE = 1024
PEER_EXP_TILE = 512


def _params(sem):
    return pltpu.CompilerParams(dimension_semantics=sem, vmem_limit_bytes=VMEM_LIMIT)


def _group_of_row(row):
    return jnp.where(row < T_CTX, 0, 1 + (row - T_CTX) // LAT_SEQ)


def _rms(x, g, n):
    ms = jnp.sum(x * x, axis=-1, keepdims=True) / n
    return x * lax.rsqrt(ms + NORM_EPS) * g


def _ada_kernel(c_ref, w_ref, b_ref, o_ref):
    c = c_ref[...]
    s = c * jax.nn.sigmoid(c)
    o_ref[...] = jnp.dot(s.astype(BF16), w_ref[...].astype(BF16), preferred_element_type=F32) + b_ref[...]


def ada_all(cpad, ada_w, ada_b):
    tn = 1536
    n = ADA_CHUNKS * D_MODEL
    return pl.pallas_call(
        _ada_kernel,
        grid=(DEPTH, n // tn),
        in_specs=[pl.BlockSpec((SUBLANES, D_MODEL), lambda l, j: (0, 0)),
                  pl.BlockSpec((None, D_MODEL, tn), lambda l, j: (l, 0, j)),
                  pl.BlockSpec((None, 1, tn), lambda l, j: (l, 0, j))],
        out_specs=pl.BlockSpec((None, SUBLANES, tn), lambda l, j: (l, 0, j)),
        out_shape=jax.ShapeDtypeStruct((DEPTH, SUBLANES, n), F32),
        compiler_params=_params(("arbitrary", "arbitrary")),
        name="ada",
    )(cpad, ada_w, ada_b.reshape(DEPTH, 1, n))


def _mod_matmul_kernel(x_ref, g_ref, sh_ref, sc_ref, w_ref, z_ref, *rest, tm, emit_ht):
    if emit_ht:
        ht_ref, h_scr = rest
    else:
        (h_scr,) = rest
    i = pl.program_id(0)

    @pl.when(pl.program_id(1) == 0)
    def _():
        grp = _group_of_row(i * tm)
        y = _rms(x_ref[...], g_ref[...], D_MODEL)
        h = y * (1.0 + sc_ref[pl.ds(grp, 1), :]) + sh_ref[pl.ds(grp, 1), :]
        h_scr[...] = h.astype(BF16)
        if emit_ht:
            ht_ref[...] = h.T.astype(BF16)

    z_ref[...] = jnp.dot(h_scr[...], w_ref[...], preferred_element_type=F32).astype(z_ref.dtype)


def mod_matmul(x, gain, mods, w, layer, shift_chunk, scale_chunk, *, tn, emit_ht, tm=ROW_TILE, out_dtype=F32):
    t = x.shape[0]
    n = w.shape[-1]
    out_shape = [jax.ShapeDtypeStruct((t, n), out_dtype)]
    out_specs = [pl.BlockSpec((tm, tn), lambda i, j: (i, j))]
    if emit_ht:
        out_shape.append(jax.ShapeDtypeStruct((D_MODEL, t), BF16))
        out_specs.append(pl.BlockSpec((D_MODEL, tm), lambda i, j: (0, i)))
    res = pl.pallas_call(
        functools.partial(_mod_matmul_kernel, tm=tm, emit_ht=emit_ht),
        grid=(t // tm, n // tn),
        in_specs=[pl.BlockSpec((tm, D_MODEL), lambda i, j: (i, 0)),
                  pl.BlockSpec((None, 1, D_MODEL), lambda i, j: (layer, 0, 0)),
                  pl.BlockSpec((None, SUBLANES, D_MODEL), lambda i, j: (layer, 0, shift_chunk)),
                  pl.BlockSpec((None, SUBLANES, D_MODEL), lambda i, j: (layer, 0, scale_chunk)),
                  pl.BlockSpec((None, D_MODEL, tn), lambda i, j: (layer, 0, j))],
        out_specs=out_specs,
        out_shape=out_shape,
        scratch_shapes=[pltpu.VMEM((tm, D_MODEL), BF16)],
        compiler_params=_params(("arbitrary", "arbitrary")),
        name="mod_matmul_ht" if emit_ht else "mod_matmul",
    )(x, gain, mods, mods, w)
    return res


def _rope128(x, c, s1, s2):
    return x * c + pltpu.roll(x, 96, 1) * s1 + pltpu.roll(x, 32, 1) * s2


def _mla_keys_values(ckv_n, kpe, rc, rs1, rs2, wukv_ref, gn_ref, gp_ref, k_ref, v_ref):
    kv = jnp.dot(ckv_n.astype(BF16), wukv_ref[...], preferred_element_type=F32)
    pe_ss = jnp.sum(kpe * kpe, axis=-1, keepdims=True)
    for h in range(MLA_HEADS):
        kn = kv[:, h * MLA_NOPE:(h + 1) * MLA_NOPE]
        ms = (jnp.sum(kn * kn, axis=-1, keepdims=True) + pe_ss) / MLA_QK_DIM
        r = lax.rsqrt(ms + NORM_EPS)
        k_ref[:, h * MLA_HEAD_PAD:h * MLA_HEAD_PAD + LANES] = (kn * r * gn_ref[...]).astype(BF16)
        k_ref[:, h * MLA_HEAD_PAD + LANES:(h + 1) * MLA_HEAD_PAD] = (
            _rope128(kpe * r * gp_ref[...], rc, rs1, rs2).astype(BF16))
    v_ref[...] = kv[:, MLA_HEADS * MLA_NOPE:].astype(BF16)


def _proj_prep_kernel(cq_ref, naq_ref, nak_ref, nav_ref, ckv_ref, rc_ref, rs1_ref, rs2_ref,
                      qng_ref, wuq_ref, qhg_ref, kvng_ref, naqg_ref, nakg_ref, wukv_ref, gn_ref, gp_ref,
                      qmla_ref, ckvn_ref, qna_ref, knaf_ref, vnaf_ref, knab_ref, vnab_ref, k_ref, v_ref):
    cqn = _rms(cq_ref[...], qng_ref[...], MLA_Q_LORA)
    q = jnp.dot(cqn.astype(BF16), wuq_ref[...], preferred_element_type=F32)
    rc, rs1, rs2 = rc_ref[...], rs1_ref[...], rs2_ref[...]
    for h in range(MLA_HEADS):
        qh = _rms(q[:, h * MLA_HEAD_PAD:(h + 1) * MLA_HEAD_PAD], qhg_ref[...], MLA_QK_DIM)
        qmla_ref[:, h * MLA_HEAD_PAD:h * MLA_HEAD_PAD + LANES] = qh[:, :LANES].astype(BF16)
        qmla_ref[:, h * MLA_HEAD_PAD + LANES:(h + 1) * MLA_HEAD_PAD] = (
            _rope128(qh[:, LANES:], rc, rs1, rs2).astype(BF16))
    ckv_n = _rms(ckv_ref[:, :MLA_KV_LORA], kvng_ref[...], MLA_KV_LORA)
    ckvn_ref[...] = ckv_n
    _mla_keys_values(ckv_n, ckv_ref[:, MLA_KV_LORA:MLA_KV_LORA + LANES], rc, rs1, rs2,
                     wukv_ref, gn_ref, gp_ref, k_ref, v_ref)
    for h in range(NA_HEADS):
        sl = slice(h * NA_HEAD_DIM, (h + 1) * NA_HEAD_DIM)
        qna_ref[:, sl] = _rms(naq_ref[:, sl], naqg_ref[...], NA_HEAD_DIM).astype(BF16)
        kn = _rms(nak_ref[:, sl], nakg_ref[...], NA_HEAD_DIM)
        knaf_ref[h] = kn
        vnaf_ref[h] = nav_ref[:, sl]
        knab_ref[:, sl] = kn.astype(BF16)
    vnab_ref[...] = nav_ref[...].astype(BF16)


def proj_prep(z, rope_c, rope_s1, rope_s2, qng, wuq, qhg, kvng, naqg, nakg, wukv, gn, gp, layer):
    t = z.shape[0]
    tm = SEQ_TILE
    zb = lambda k: pl.BlockSpec((tm, Z_BLOCK), lambda i: (i, k))
    rb = pl.BlockSpec((tm, LANES), lambda i: (i, 0))
    wl = lambda *shape: pl.BlockSpec((None,) + shape, lambda i: (layer,) + (0,) * len(shape))
    ob = lambda w: pl.BlockSpec((tm, w), lambda i: (i, 0))
    cache_b = pl.BlockSpec((None, NA_HEADS, tm, NA_HEAD_DIM), lambda i: (i, 0, 0, 0))
    return pl.pallas_call(
        _proj_prep_kernel,
        grid=(t // tm,),
        in_specs=[zb(0), zb(1), zb(2), zb(Z_NAV_BLOCK), zb(Z_CKV_BLOCK), rb, rb, rb,
                  wl(1, MLA_Q_LORA), wl(MLA_Q_LORA, MLA_HEADS * MLA_HEAD_PAD), wl(1, MLA_HEAD_PAD),
                  wl(1, MLA_KV_LORA), wl(1, NA_HEAD_DIM), wl(1, NA_HEAD_DIM),
                  wl(MLA_KV_LORA, 2 * MLA_HEADS * MLA_NOPE), wl(1, LANES), wl(1, LANES)],
        out_specs=[ob(MLA_HEADS * MLA_HEAD_PAD), ob(MLA_KV_LORA), ob(NA_WIDTH), cache_b, cache_b, ob(NA_WIDTH),
                   ob(NA_WIDTH), ob(MLA_HEADS * MLA_HEAD_PAD), ob(MLA_HEADS * MLA_NOPE)],
        out_shape=[jax.ShapeDtypeStruct((t, MLA_HEADS * MLA_HEAD_PAD), BF16),
                   jax.ShapeDtypeStruct((t, MLA_KV_LORA), F32),
                   jax.ShapeDtypeStruct((t, NA_WIDTH), BF16),
                   jax.ShapeDtypeStruct((t // tm, NA_HEADS, tm, NA_HEAD_DIM), F32),
                   jax.ShapeDtypeStruct((t // tm, NA_HEADS, tm, NA_HEAD_DIM), F32),
                   jax.ShapeDtypeStruct((t, NA_WIDTH), BF16),
                   jax.ShapeDtypeStruct((t, NA_WIDTH), BF16),
                   jax.ShapeDtypeStruct((t, MLA_HEADS * MLA_HEAD_PAD), BF16),
                   jax.ShapeDtypeStruct((t, MLA_HEADS * MLA_NOPE), BF16)],
        compiler_params=_params(("arbitrary",)),
        name="proj_prep",
    )(z, z, z, z, z, rope_c, rope_s1, rope_s2, qng, wuq, qhg, kvng, naqg, nakg, wukv, gn, gp)


def _kv_prep_kernel(ckv_ref, kpe_ref, rc_ref, rs1_ref, rs2_ref, wukv_ref, gn_ref, gp_ref, k_ref, v_ref):
    _mla_keys_values(ckv_ref[...], kpe_ref[...], rc_ref[...], rs1_ref[...], rs2_ref[...],
                     wukv_ref, gn_ref, gp_ref, k_ref, v_ref)


def kv_prep_cache(ckv, kpe, ident_c, ident_s, wukv, gn, gp):
    tm = SEQ_TILE
    rb = lambda w: pl.BlockSpec((None, tm, w), lambda l, i: (l, i, 0))
    tb = pl.BlockSpec((tm, LANES), lambda l, i: (0, 0))
    wl = lambda *shape: pl.BlockSpec((None,) + shape, lambda l, i: (l,) + (0,) * len(shape))
    return pl.pallas_call(
        _kv_prep_kernel,
        grid=(DEPTH, T_CACHE // tm),
        in_specs=[rb(MLA_KV_LORA), rb(LANES), tb, tb, tb,
                  wl(MLA_KV_LORA, 2 * MLA_HEADS * MLA_NOPE), wl(1, LANES), wl(1, LANES)],
        out_specs=[rb(MLA_HEADS * MLA_HEAD_PAD), rb(MLA_HEADS * MLA_NOPE)],
        out_shape=[jax.ShapeDtypeStruct((DEPTH, T_CACHE, MLA_HEADS * MLA_HEAD_PAD), BF16),
                   jax.ShapeDtypeStruct((DEPTH, T_CACHE, MLA_HEADS * MLA_NOPE), BF16)],
        compiler_params=_params(("arbitrary", "arbitrary")),
        name="kv_prep_cache",
    )(ckv, kpe, ident_c, ident_s, ident_s, wukv, gn, gp)


_NT = (((1,), (1,)), ((), ()))


def _attn_kernel(q_ref, k_ref, v_ref, o_ref, *, scale, n_heads, dq):
    dv = NA_HEAD_DIM
    for h in range(n_heads):
        q = q_ref[:, h * dq:(h + 1) * dq]
        k = k_ref[:, h * dq:(h + 1) * dq]
        s = lax.dot_general(q, k, _NT, preferred_element_type=F32) * scale
        m = jnp.max(s, axis=-1, keepdims=True)
        p = jnp.exp(s - m)
        p = p / jnp.sum(p, axis=-1, keepdims=True)
        o = jnp.dot(p.astype(BF16), v_ref[:, h * dv:(h + 1) * dv], preferred_element_type=F32)
        o_ref[:, h * dv:(h + 1) * dv] = o.astype(BF16)


def attention(q, k, v, *, n_seq, n_heads, dq, sq_total, sk, q_row_off, k_row_off, scale):
    sq = ATTN_Q_BLOCK
    nq = sq_total // sq
    qoff = q_row_off // sq
    koff = k_row_off // sk
    dv = NA_HEAD_DIM
    return pl.pallas_call(
        functools.partial(_attn_kernel, scale=scale, n_heads=n_heads, dq=dq),
        grid=(n_seq, nq),
        in_specs=[pl.BlockSpec((sq, n_heads * dq), lambda b, qi: (qoff + b * nq + qi, 0)),
                  pl.BlockSpec((sk, n_heads * dq), lambda b, qi: (koff + b, 0)),
                  pl.BlockSpec((sk, n_heads * dv), lambda b, qi: (koff + b, 0))],
        out_specs=pl.BlockSpec((sq, n_heads * dv), lambda b, qi: (b * nq + qi, 0)),
        out_shape=jax.ShapeDtypeStruct((n_seq * sq_total, n_heads * dv), BF16),
        compiler_params=_params(("arbitrary", "arbitrary")),
        name="attention",
    )(q, k, v)


def _attn_cached_kernel(q_ref, k_ref, v_ref, kc_ref, vc_ref, o_ref, *, scale, n_heads, dq):
    dv = NA_HEAD_DIM
    for h in range(n_heads):
        qk = slice(h * dq, (h + 1) * dq)
        vv = slice(h * dv, (h + 1) * dv)
        q = q_ref[:, qk]
        s_own = lax.dot_general(q, k_ref[:, qk], _NT, preferred_element_type=F32) * scale
        s_ctx = lax.dot_general(q, kc_ref[:, qk], _NT, preferred_element_type=F32) * scale
        m = jnp.maximum(jnp.max(s_own, axis=-1, keepdims=True), jnp.max(s_ctx, axis=-1, keepdims=True))
        p_own = jnp.exp(s_own - m)
        p_ctx = jnp.exp(s_ctx - m)
        denom = jnp.sum(p_own, axis=-1, keepdims=True) + jnp.sum(p_ctx, axis=-1, keepdims=True)
        o = jnp.dot((p_ctx / denom).astype(BF16), vc_ref[:, vv], preferred_element_type=F32)
        o = o + jnp.dot((p_own / denom).astype(BF16), v_ref[:, vv], preferred_element_type=F32)
        o_ref[:, vv] = o.astype(BF16)


def latent_mla_attention(q, k, v, k_cache, v_cache, layer, *, scale):
    sq = ATTN_Q_BLOCK
    nq = LAT_SEQ // sq
    qoff = T_CTX // sq
    koff = T_CTX // LAT_SEQ
    wq = MLA_HEADS * MLA_HEAD_PAD
    wv = MLA_HEADS * NA_HEAD_DIM
    return pl.pallas_call(
        functools.partial(_attn_cached_kernel, scale=scale, n_heads=MLA_HEADS, dq=MLA_HEAD_PAD),
        grid=(N_LAT_SEQ, nq),
        in_specs=[pl.BlockSpec((sq, wq), lambda b, qi: (qoff + b * nq + qi, 0)),
                  pl.BlockSpec((LAT_SEQ, wq), lambda b, qi: (koff + b, 0)),
                  pl.BlockSpec((LAT_SEQ, wv), lambda b, qi: (koff + b, 0)),
                  pl.BlockSpec((None, PAST_LEN, wq), lambda b, qi: (layer, b, 0)),
                  pl.BlockSpec((None, PAST_LEN, wv), lambda b, qi: (layer, b, 0))],
        out_specs=pl.BlockSpec((sq, wv), lambda b, qi: (b * nq + qi, 0)),
        out_shape=jax.ShapeDtypeStruct((T_LAT, wv), BF16),
        compiler_params=_params(("arbitrary", "arbitrary")),
        name="latent_mla",
    )(q, k, v, k_cache, v_cache)


NA_ROWS = LAT_SEQ // GRID_W
NA_LOCAL = NA_KR * GRID_W


def _na_lat_kernel(q_ref, k_ref, v_ref, kc_ref, vc_ref, b_ref, o_ref, *, scale):
    rq = pl.program_id(1)
    row_start = jnp.clip(rq - NA_KR // 2, 0, NA_ROWS - NA_KR)
    start = pl.multiple_of(row_start * GRID_W, GRID_W)
    for h in range(NA_HEADS):
        cols = slice(h * NA_HEAD_DIM, (h + 1) * NA_HEAD_DIM)
        q = q_ref[:, cols]
        k_loc = k_ref[pl.ds(start, NA_LOCAL), cols]
        v_loc = v_ref[pl.ds(start, NA_LOCAL), cols]
        s_loc = lax.dot_general(q, k_loc, _NT, preferred_element_type=F32) * scale + b_ref[h, rq - row_start]
        s_ctx = lax.dot_general(q, kc_ref[:, cols], _NT, preferred_element_type=F32) * scale
        m = jnp.maximum(jnp.max(s_loc, axis=-1, keepdims=True), jnp.max(s_ctx, axis=-1, keepdims=True))
        p_loc = jnp.exp(s_loc - m)
        p_ctx = jnp.exp(s_ctx - m)
        denom = jnp.sum(p_loc, axis=-1, keepdims=True) + jnp.sum(p_ctx, axis=-1, keepdims=True)
        o = jnp.dot((p_loc / denom).astype(BF16), v_loc, preferred_element_type=F32)
        o = o + jnp.dot((p_ctx / denom).astype(BF16), vc_ref[:, cols], preferred_element_type=F32)
        o_ref[:, cols] = o.astype(BF16)


def na_latent_attention(q, k, v, k_cache, v_cache, bias_tab, layer, *, scale):
    qoff = T_CTX // GRID_W
    koff = T_CTX // LAT_SEQ
    w = NA_WIDTH
    return pl.pallas_call(
        functools.partial(_na_lat_kernel, scale=scale),
        grid=(N_LAT_SEQ, NA_ROWS),
        in_specs=[pl.BlockSpec((GRID_W, w), lambda b, r: (qoff + b * NA_ROWS + r, 0)),
                  pl.BlockSpec((LAT_SEQ, w), lambda b, r: (koff + b, 0)),
                  pl.BlockSpec((LAT_SEQ, w), lambda b, r: (koff + b, 0)),
                  pl.BlockSpec((None, None, PAST_LEN, w), lambda b, r: (layer, b, 0, 0)),
                  pl.BlockSpec((None, None, PAST_LEN, w), lambda b, r: (layer, b, 0, 0)),
                  pl.BlockSpec((None, NA_HEADS, NA_KR, GRID_W, NA_LOCAL), lambda b, r: (layer, 0, 0, 0, 0))],
        out_specs=pl.BlockSpec((GRID_W, w), lambda b, r: (b * NA_ROWS + r, 0)),
        out_shape=jax.ShapeDtypeStruct((T_LAT, NA_WIDTH), BF16),
        compiler_params=_params(("arbitrary", "arbitrary")),
        name="na_latent",
    )(q, k, v, k_cache, v_cache, bias_tab)


def _conv_kernel(gb_ref, gc_ref, u_ref, w_ref, o_ref):
    gu = gc_ref[...] * u_ref[...]
    s = gu.shape[0]
    row = lax.broadcasted_iota(jnp.int32, gu.shape, 0)
    prev = jnp.where(row == 0, 0.0, pltpu.roll(gu, 1, 0))
    nxt = jnp.where(row == s - 1, 0.0, pltpu.roll(gu, s - 1, 0))
    y = prev * w_ref[0:1, :] + gu * w_ref[1:2, :] + nxt * w_ref[2:3, :]
    o_ref[...] = (gb_ref[...] * y).astype(BF16)


def short_conv(z, conv_w8, layer, *, n_seq, seq, row_off):
    off = row_off // seq
    zb = lambda k: pl.BlockSpec((seq, CONV_CH), lambda i: (off + i, k))
    return pl.pallas_call(
        _conv_kernel,
        grid=(n_seq,),
        in_specs=[zb(Z_CONV_BLOCK), zb(Z_CONV_BLOCK + 1), zb(Z_CONV_BLOCK + 2),
                  pl.BlockSpec((None, SUBLANES, CONV_CH), lambda i: (layer, 0, 0))],
        out_specs=pl.BlockSpec((seq, CONV_CH), lambda i: (i, 0)),
        out_shape=jax.ShapeDtypeStruct((n_seq * seq, CONV_CH), BF16),
        compiler_params=_params(("arbitrary",)),
        name="short_conv",
    )(z, z, z, conv_w8)


def _out_matmul_kernel(x_ref, g_ref, a0c_ref, a0l_ref, a1c_ref, a1l_ref, a2c_ref, a2l_ref,
                       w0_ref, w1_ref, w2_ref, o_ref, *, tm):
    row = pl.program_id(0) * tm
    grp = _group_of_row(row)
    is_ctx = row < T_CTX
    acc = jnp.dot(jnp.where(is_ctx, a0c_ref[...], a0l_ref[...]), w0_ref[...], preferred_element_type=F32)
    acc = acc + jnp.dot(jnp.where(is_ctx, a1c_ref[...], a1l_ref[...]), w1_ref[...], preferred_element_type=F32)
    acc = acc + jnp.dot(jnp.where(is_ctx, a2c_ref[...], a2l_ref[...]), w2_ref[...], preferred_element_type=F32)
    o_ref[...] = x_ref[...] + g_ref[pl.ds(grp, 1), :] * acc


def out_matmul(x, mods, o_mla, o_na, conv, w_out, layer, gate_chunk):
    t = x.shape[0]
    tm, tn = ROW_TILE, 1024
    nj = D_MODEL // tn
    w_mla = MLA_HEADS * MLA_NOPE
    n_ctx = T_CTX // tm
    ctx = lambda w: pl.BlockSpec((tm, w), lambda i, j: (jnp.minimum(i, n_ctx - 1), 0))
    lat = lambda w: pl.BlockSpec((tm, w), lambda i, j: (jnp.maximum(i - n_ctx, 0), 0))
    return pl.pallas_call(
        functools.partial(_out_matmul_kernel, tm=tm),
        grid=(t // tm, nj),
        in_specs=[pl.BlockSpec((tm, tn), lambda i, j: (i, j)),
                  pl.BlockSpec((None, SUBLANES, tn), lambda i, j: (layer, 0, gate_chunk * nj + j)),
                  ctx(w_mla), lat(w_mla), ctx(NA_WIDTH), lat(NA_WIDTH), ctx(CONV_CH), lat(CONV_CH),
                  pl.BlockSpec((None, w_mla, tn), lambda i, j: (layer, 0, j)),
                  pl.BlockSpec((None, NA_WIDTH, tn), lambda i, j: (layer, w_mla // NA_WIDTH, j)),
                  pl.BlockSpec((None, CONV_CH, tn), lambda i, j: (layer, (w_mla + NA_WIDTH) // CONV_CH, j))],
        out_specs=pl.BlockSpec((tm, tn), lambda i, j: (i, j)),
        out_shape=jax.ShapeDtypeStruct((t, D_MODEL), F32),
        compiler_params=_params(("arbitrary", "arbitrary")),
        name="out_matmul",
    )(x, mods, *o_mla, *o_na, *conv, w_out, w_out, w_out)


def _argmax_step(s, pos, big):
    vals = [s[k:k + SUBLANES] for k in range(0, s.shape[0], SUBLANES)]
    idxs = [pos[k:k + SUBLANES] for k in range(0, s.shape[0], SUBLANES)]
    while len(vals) > 1:
        nxt_v, nxt_i = [], []
        for a in range(0, len(vals) - 1, 2):
            take_hi = vals[a + 1] > vals[a]
            nxt_v.append(jnp.maximum(vals[a], vals[a + 1]))
            nxt_i.append(jnp.where(take_hi, idxs[a + 1], idxs[a]))
        if len(vals) % 2:
            nxt_v.append(vals[-1])
            nxt_i.append(idxs[-1])
        vals, idxs = nxt_v, nxt_i
    m = jnp.max(vals[0], axis=0, keepdims=True)
    first = jnp.min(jnp.where(vals[0] == m, idxs[0], big), axis=0, keepdims=True)
    return m, first, pos == first


def _half_ranks(s1, s2):
    n, width = s1.shape
    pos = lax.broadcasted_iota(jnp.int32, (n, width), 0).astype(F32)
    iota16 = lax.broadcasted_iota(jnp.int32, (PEER_TOPK, width), 0)

    def body(j, carry):
        s1c, v1c, i1c, s2c, v2c, r2c = carry
        m1, first1, hit1 = _argmax_step(s1c, pos, float(n))
        m2, _, hit2 = _argmax_step(s2c, pos, float(n))
        return (jnp.where(hit1, -jnp.inf, s1c), jnp.where(iota16 == j, m1, v1c), jnp.where(iota16 == j, first1, i1c),
                jnp.where(hit2, -jnp.inf, s2c), jnp.where(iota16 == j, m2, v2c), jnp.where(hit2, j, r2c))

    zero16 = jnp.zeros((PEER_TOPK, width), F32)
    rank0 = jnp.full((n, width), float(PEER_TOPK), F32)
    _, v1, i1, _, v2, r2 = lax.fori_loop(0, PEER_TOPK, body, (s1, zero16, zero16, s2, zero16, rank0), unroll=True)
    return pos, v1, i1, v2, r2


_CAND_ROWS = 16 + 7 * 8 + 8


def _cand_positions():
    p = [j2 for j2 in range(16)]
    p += [j1 * 16 + j2 for j1 in range(1, 8) for j2 in range(8)]
    p += [j1 * 16 for j1 in range(8, 16)]
    return np.tile(np.asarray(p, np.float32)[:, None], (1, LANES))


def _staircase(v1, v2, cpos):
    cand = jnp.concatenate([v1[0:1] + v2] + [v1[j:j + 1] + v2[0:8] for j in range(1, 8)] + [v1[8:16] + v2[0:1]],
                           axis=0)
    cmax = cand[0:1]
    row16 = lax.broadcasted_iota(jnp.int32, (PEER_TOPK, cand.shape[1]), 0).astype(F32)

    def body(j, carry):
        s, counts, z = carry
        m, first, hit = _argmax_step(s, cpos, float(PEER_TOPK * PEER_TOPK))
        j1 = jnp.floor(first * (1.0 / PEER_TOPK))
        counts = jnp.where(row16 == j1, counts + 1.0, counts)
        return jnp.where(hit, -jnp.inf, s), counts, z + jnp.exp(m - cmax)

    _, counts, z = lax.fori_loop(0, PEER_TOPK, body, (cand, jnp.zeros_like(row16), jnp.zeros_like(cmax)),
                                 unroll=True)
    return counts, z


def _peer_topk_kernel(q_ref, keys_ref, cpos_ref, e1_ref, lb1_ref, r2_ref, e2_ref, *, heads):
    nt = (((1,), (1,)), ((), ()))
    for h in range(heads):
        q = q_ref[:, h * 2 * LANES:(h + 1) * 2 * LANES]
        s1 = lax.dot_general(keys_ref[h, 0], q[:, :LANES], nt, preferred_element_type=F32)
        s2 = lax.dot_general(keys_ref[h, 1], q[:, LANES:], nt, preferred_element_type=F32)
        pos, v1, i1, v2, r2 = _half_ranks(s1, s2)
        counts, z = _staircase(v1, v2, cpos_ref[...])
        lb1 = jnp.zeros_like(s1)
        for j in range(PEER_TOPK):
            lb1 = jnp.where(pos == i1[j:j + 1], counts[j:j + 1], lb1)
        e1_ref[h] = jnp.exp(s1 - v1[0:1])
        lb1_ref[h] = lb1
        r2_ref[h] = r2
        e2_ref[h] = jnp.exp(s2 - v2[0:1]) / z


def peer_topk(q, sub_keys, layer):
    t = q.shape[0]
    heads = PEER_HEADS
    ob = pl.BlockSpec((heads, PEER_N_KEYS, LANES), lambda i, h: (h, 0, i))
    shp = jax.ShapeDtypeStruct((PEER_HEADS, PEER_N_KEYS, t), F32)
    return pl.pallas_call(
        functools.partial(_peer_topk_kernel, heads=heads),
        grid=(t // LANES, PEER_HEADS // heads),
        in_specs=[pl.BlockSpec((LANES, heads * 2 * LANES), lambda i, h: (i, h)),
                  pl.BlockSpec((None, heads, 2, PEER_N_KEYS, LANES), lambda i, h: (layer, h, 0, 0, 0)),
                  pl.BlockSpec((_CAND_ROWS, LANES), lambda i, h: (0, 0))],
        out_specs=[ob, ob, ob, ob],
        out_shape=[shp, shp, shp, shp],
        compiler_params=_params(("arbitrary", "arbitrary")),
        name="peer_topk",
    )(q, sub_keys, jnp.asarray(_cand_positions()))


_SQRT_HALF = float(np.sqrt(0.5))


def _peer_dense_kernel(ht_ref, u_ref, v_ref, e1_ref, lb1_ref, r2_ref, e2_ref, x_ref, g_ref, o_ref,
                       acc_ref, at_ref, wg_ref, *, tt, ec):
    i = pl.program_id(0)
    c = pl.program_id(1)

    @pl.when(c == 0)
    def _():
        acc_ref[...] = jnp.zeros_like(acc_ref)

    at_ref[...] = jnp.dot(u_ref[...], ht_ref[...], preferred_element_type=F32)
    n_a = ec // PEER_N_KEYS
    grp0 = pl.multiple_of((c * n_a) // SUBLANES * SUBLANES, SUBLANES)
    off = (c * n_a) % SUBLANES

    def key_row(ref, h, al, lanes):
        blk = ref[h, pl.ds(grp0, SUBLANES), lanes]
        row = blk[al:al + 1]
        for o in range(n_a, SUBLANES, n_a):
            row = jnp.where(off == o, blk[o + al:o + al + 1], row)
        return row

    pair = 2
    for lt in range(tt // LANES):
        lanes = slice(lt * LANES, (lt + 1) * LANES)
        for al0 in range(0, n_a, pair):
            gates = [None] * pair
            for h in range(PEER_HEADS):
                r2 = r2_ref[h, :, lanes]
                e2 = e2_ref[h, :, lanes]
                for k in range(pair):
                    lb = key_row(lb1_ref, h, al0 + k, lanes)
                    e1 = key_row(e1_ref, h, al0 + k, lanes)
                    term = jnp.where(r2 < lb, e2, 0.0) * e1
                    gates[k] = term if gates[k] is None else gates[k] + term
            for k in range(pair):
                rows = slice((al0 + k) * PEER_N_KEYS, (al0 + k + 1) * PEER_N_KEYS)
                act = at_ref[rows, lanes]
                wg_ref[rows, lanes] = 0.5 * act * (1.0 + lax.erf(act * _SQRT_HALF)) * gates[k]
    acc_ref[...] += jnp.dot(wg_ref[...].T.astype(BF16), v_ref[...], preferred_element_type=F32)

    @pl.when(c == pl.num_programs(1) - 1)
    def _():
        grp = _group_of_row(i * tt)
        o_ref[...] = x_ref[...] + g_ref[pl.ds(grp, 1), :] * acc_ref[...]


def peer_dense(ht, peer_u, peer_v, e1, lb1, r2, e2, x, mods, layer, gate_chunk):
    t = x.shape[0]
    tt, ec = PEER_TOK_TILE, PEER_EXP_TILE
    once = pl.Buffered(1)
    kb = pl.BlockSpec((PEER_HEADS, PEER_N_KEYS, tt), lambda i, c: (0, 0, i), pipeline_mode=once)
    return pl.pallas_call(
        functools.partial(_peer_dense_kernel, tt=tt, ec=ec),
        grid=(t // tt, PEER_N_EXPERTS // ec),
        in_specs=[pl.BlockSpec((D_MODEL, tt), lambda i, c: (0, i), pipeline_mode=once),
                  pl.BlockSpec((None, ec, D_MODEL), lambda i, c: (layer, c, 0)),
                  pl.BlockSpec((None, ec, D_MODEL), lambda i, c: (layer, c, 0)),
                  kb, kb, kb, kb,
                  pl.BlockSpec((tt, D_MODEL), lambda i, c: (i, 0), pipeline_mode=once),
                  pl.BlockSpec((None, SUBLANES, D_MODEL), lambda i, c: (layer, 0, gate_chunk))],
        out_specs=pl.BlockSpec((tt, D_MODEL), lambda i, c: (i, 0), pipeline_mode=once),
        out_shape=jax.ShapeDtypeStruct((t, D_MODEL), F32),
        scratch_shapes=[pltpu.VMEM((tt, D_MODEL), F32),
                        pltpu.VMEM((ec, tt), F32),
                        pltpu.VMEM((ec, tt), F32)],
        compiler_params=_params(("arbitrary", "arbitrary")),
        name="peer_dense",
    )(ht, peer_u, peer_v, e1, lb1, r2, e2, x, mods)


def _rope_tables():
    t = jnp.arange(LAT_SEQ)
    row = (t // GRID_W).astype(F32)
    col = (t % GRID_W).astype(F32)
    n_freq = MLA_ROPE // 4
    inv = ROPE_THETA ** (-jnp.arange(n_freq, dtype=F32) / n_freq)
    ang = jnp.concatenate([row[:, None] * inv, col[:, None] * inv], axis=-1)
    cos, sin = jnp.cos(ang), jnp.sin(ang)
    zero = jnp.zeros_like(cos)
    pad = jnp.zeros((LAT_SEQ, LANES - MLA_ROPE), F32)
    c_lat = jnp.concatenate([cos, cos, pad], axis=-1)
    s1_lat = jnp.concatenate([-sin, zero, pad], axis=-1)
    s2_lat = jnp.concatenate([zero, sin, pad], axis=-1)
    ones = jnp.concatenate([jnp.ones((1, MLA_ROPE), F32), jnp.zeros((1, LANES - MLA_ROPE), F32)], axis=-1)

    def full(lat, ident):
        return jnp.concatenate([jnp.broadcast_to(ident, (T_CTX, LANES)), jnp.tile(lat, (N_LAT_SEQ, 1)),
                                jnp.broadcast_to(ident, (T_CACHE, LANES))], axis=0)

    zeros = jnp.zeros((1, LANES), F32)
    return full(c_lat, ones), full(s1_lat, zeros), full(s2_lat, zeros)


def _na_bias_table(rel_bias):
    cq = np.arange(GRID_W)
    col_start = np.clip(cq - NA_KC // 2, 0, GRID_W - NA_KC)
    valid = (cq[None, :] >= col_start[:, None]) & (cq[None, :] < col_start[:, None] + NA_KC)
    coff = np.clip(cq[None, :] - cq[:, None], -(NA_KC - 1), NA_KC - 1) + (NA_KC - 1)
    onehot = (coff[:, :, None] == np.arange(2 * NA_KC - 1)[None, None, :]).astype(np.float32)
    toep = jnp.einsum('lhrc,qkc->lhrqk', rel_bias, jnp.asarray(onehot), precision=lax.Precision.HIGHEST)
    toep = jnp.where(jnp.asarray(valid)[None, None, None], toep, -jnp.inf)
    tabs = []
    for d in range(NA_KR):
        rows = toep[:, :, NA_KR - 1 - d:2 * NA_KR - 1 - d]
        tabs.append(rows.transpose(0, 1, 3, 2, 4).reshape(DEPTH, NA_HEADS, GRID_W, NA_LOCAL))
    return jnp.stack(tabs, axis=2)


def kernel(x_prompt, x_sample, cache_mla_ckv, cache_mla_kpe, cache_na_k, cache_na_v, c, c_ctx, ada_w, ada_b, norm_mix_g, norm_ffn_g, w_in, mla_q_norm_g, mla_w_uq, mla_kv_norm_g, mla_w_ukv, mla_q_head_g, mla_k_head_g, na_q_head_g, na_k_head_g, na_rel_bias, conv_w, w_out, peer_w_q, peer_sub_keys, peer_u, peer_v):
    lat_end = MLA_Q_LORA + MLA_KV_LORA + MLA_ROPE
    w_in_p = jnp.concatenate([w_in[..., :MLA_Q_LORA], w_in[..., lat_end:], w_in[..., MLA_Q_LORA:lat_end],
                              jnp.zeros((DEPTH, D_MODEL, D_IN_PAD - D_IN), F32)], axis=-1).astype(BF16)
    wuq_p = jnp.pad(mla_w_uq.reshape(DEPTH, MLA_Q_LORA, MLA_HEADS, MLA_QK_DIM),
                    ((0, 0), (0, 0), (0, 0), (0, MLA_HEAD_PAD - MLA_QK_DIM))
                    ).reshape(DEPTH, MLA_Q_LORA, MLA_HEADS * MLA_HEAD_PAD).astype(BF16)
    qhg_p = jnp.pad(mla_q_head_g, ((0, 0), (0, MLA_HEAD_PAD - MLA_QK_DIM))).reshape(DEPTH, 1, MLA_HEAD_PAD)
    wukv4 = mla_w_ukv.reshape(DEPTH, MLA_KV_LORA, MLA_HEADS, 2 * MLA_NOPE)
    wukv_p = jnp.concatenate([wukv4[..., :MLA_NOPE].reshape(DEPTH, MLA_KV_LORA, -1),
                              wukv4[..., MLA_NOPE:].reshape(DEPTH, MLA_KV_LORA, -1)], axis=-1).astype(BF16)
    khg_n = mla_k_head_g[:, :MLA_NOPE].reshape(DEPTH, 1, LANES)
    khg_p = jnp.pad(mla_k_head_g[:, MLA_NOPE:], ((0, 0), (0, LANES - MLA_ROPE))).reshape(DEPTH, 1, LANES)
    conv_w8 = jnp.pad(conv_w.transpose(0, 2, 1),
                      ((0, 0), (0, SUBLANES - conv_w.shape[-1]), (0, 0)))
    w_out_b = w_out.astype(BF16)
    peer_wq_b = peer_w_q.astype(BF16)
    sub_keys_b = peer_sub_keys.astype(BF16)
    peer_u_b = peer_u.astype(BF16)
    peer_v_b = peer_v.astype(BF16)
    g_mix = norm_mix_g.reshape(DEPTH, 1, D_MODEL)
    g_ffn = norm_ffn_g.reshape(DEPTH, 1, D_MODEL)
    qng = mla_q_norm_g.reshape(DEPTH, 1, MLA_Q_LORA)
    kvng = mla_kv_norm_g.reshape(DEPTH, 1, MLA_KV_LORA)
    naqg = na_q_head_g.reshape(DEPTH, 1, NA_HEAD_DIM)
    nakg = na_k_head_g.reshape(DEPTH, 1, NA_HEAD_DIM)
    rope_c, rope_s1, rope_s2 = _rope_tables()
    na_bias_tab = _na_bias_table(na_rel_bias)
    cache_kpe_p = jnp.pad(cache_mla_kpe, ((0, 0), (0, 0), (0, 0), (0, LANES - MLA_ROPE)))
    cache_nak = cache_na_k.transpose(1, 0, 3, 2, 4).reshape(DEPTH, N_LAT_SEQ, PAST_LEN, NA_WIDTH).astype(BF16)
    cache_nav = cache_na_v.transpose(1, 0, 3, 2, 4).reshape(DEPTH, N_LAT_SEQ, PAST_LEN, NA_WIDTH).astype(BF16)

    cpad = jnp.concatenate([c_ctx[None, :], c, jnp.zeros((SUBLANES - 1 - N_LAT_SEQ, D_MODEL), F32)], axis=0)
    mods = ada_all(cpad, ada_w, ada_b)

    x = jnp.concatenate([x_prompt.reshape(T_CTX, D_MODEL), x_sample.reshape(T_LAT, D_MODEL)], axis=0)
    ckv_out, kpe_out, nak_out, nav_out = [], [], [], []
    mla_scale = MLA_QK_DIM ** -0.5
    na_scale = NA_HEAD_DIM ** -0.5

    ident_c = jnp.broadcast_to(rope_c[:1], (SEQ_TILE, LANES))
    ident_s = jnp.zeros((SEQ_TILE, LANES), F32)
    k_cache, v_cache = kv_prep_cache(
        cache_mla_ckv.transpose(1, 0, 2, 3).reshape(DEPTH, T_CACHE, MLA_KV_LORA),
        cache_kpe_p.transpose(1, 0, 2, 3).reshape(DEPTH, T_CACHE, LANES), ident_c, ident_s, wukv_p, khg_n, khg_p)

    for l in range(DEPTH):
        (z,) = mod_matmul(x, g_mix, mods, w_in_p, l, 0, 1, tn=1024, emit_ht=False)
        q_mla, ckv_n, q_na, k_na_f, v_na_f, k_na_b, v_na_b, k_mla, v_mla = proj_prep(
            z, rope_c, rope_s1, rope_s2, qng, wuq_p, qhg_p, kvng, naqg, nakg, wukv_p, khg_n, khg_p, l)

        o_mla_ctx = attention(q_mla, k_mla, v_mla, n_seq=N_CTX_SEQ, n_heads=MLA_HEADS, dq=MLA_HEAD_PAD,
                              sq_total=CTX_SEQ, sk=CTX_SEQ, q_row_off=0, k_row_off=0, scale=mla_scale)
        o_mla_lat = latent_mla_attention(q_mla, k_mla, v_mla, k_cache, v_cache, l, scale=mla_scale)
        o_na_ctx = attention(q_na, k_na_b, v_na_b, n_seq=N_CTX_SEQ, n_heads=NA_HEADS, dq=NA_HEAD_DIM,
                             sq_total=CTX_SEQ, sk=CTX_SEQ, q_row_off=0, k_row_off=0, scale=na_scale)
        o_na_lat = na_latent_attention(q_na, k_na_b, v_na_b, cache_nak, cache_nav, na_bias_tab, l, scale=na_scale)
        conv_ctx = short_conv(z, conv_w8, l, n_seq=N_CTX_SEQ, seq=CTX_SEQ, row_off=0)
        conv_lat = short_conv(z, conv_w8, l, n_seq=N_LAT_SEQ, seq=LAT_SEQ, row_off=T_CTX)

        x = out_matmul(x, mods, (o_mla_ctx, o_mla_lat), (o_na_ctx, o_na_lat), (conv_ctx, conv_lat), w_out_b, l, 2)

        q_peer, ht = mod_matmul(x, g_ffn, mods, peer_wq_b, l, 3, 4, tn=D_MODEL, emit_ht=True, out_dtype=BF16)
        e1, lb1, r2, e2 = peer_topk(q_peer, sub_keys_b, l)
        x = peer_dense(ht, peer_u_b, peer_v_b, e1, lb1, r2, e2, x, mods, l, 5)

        ckv_out.append(ckv_n[:T_CTX].reshape(N_CTX_SEQ, CTX_SEQ, MLA_KV_LORA))
        kpe_out.append(z[:T_CTX, Z_KPE_COL:Z_KPE_COL + MLA_ROPE].reshape(N_CTX_SEQ, CTX_SEQ, MLA_ROPE))
        nak_out.append(k_na_f[:N_CTX_SEQ])
        nav_out.append(v_na_f[:N_CTX_SEQ])

    y_prompt = x[:T_CTX].reshape(N_CTX_SEQ, CTX_SEQ, D_MODEL)
    y_sample = x[T_CTX:].reshape(N_LAT_SEQ, LAT_SEQ, D_MODEL)
    return (y_prompt, y_sample, jnp.stack(ckv_out, axis=1), jnp.stack(kpe_out, axis=1),
            jnp.stack(nak_out, axis=1), jnp.stack(nav_out, axis=1))
```
